```python
import math
import jax, jax.numpy as jnp
from jax import lax
import numpy as np

D_MODEL = 2048
BATCH = 4
SEQ = 4096
DEPTH = 1
DEC_BATCH = 32
DEC_SEQ = 4
PAST_LEN = 16384
PAGE_SIZE = 128

MIX_WIDTH = D_MODEL
GLA_HEADS = 4
GLA_DV = (MIX_WIDTH // 2) // GLA_HEADS
GLA_DK = GLA_DV // 2
GLA_GATE_RANK = 16
GLA_TAU = 16.0
GLA_CHUNK = 64
GLA_QK_W = GLA_HEADS * GLA_DK
GLA_V_W = GLA_HEADS * GLA_DV
SWA_HEADS = 8
SWA_HD = (MIX_WIDTH // 2) // SWA_HEADS
SWA_W = SWA_HEADS * SWA_HD
SWA_PATTERNS = ((128, 1), (512, 4), (2048, 16))
SWA_MAX_WINDOW = 2048
D_FF = 5632
CONV_W = 3
EPS = 1e-6

IN_SIZES = (GLA_QK_W, GLA_QK_W, GLA_V_W, GLA_V_W, GLA_GATE_RANK, SWA_W, SWA_W, SWA_W)
IN_COLS = sum(IN_SIZES)

kernel_name = "hybrid_gla_dilated_swa_convffn_step"

F32 = jnp.float32


def rms_norm(x, g):
    xf = x.astype(F32)
    y = xf * lax.rsqrt(jnp.mean(xf * xf, axis=-1, keepdims=True) + EPS) * g.astype(F32)
    return y.astype(x.dtype)


def alibi_slopes(n):
    return 2.0 ** (-8.0 * jnp.arange(1, n + 1, dtype=F32) / n)


def gla_recurrence(q, k, v, log_a, s0):
    b, t, h, dk = q.shape
    dv = v.shape[-1]
    c = min(GLA_CHUNK, t)
    n = -(-t // c)
    pad = n * c - t
    q, k, v, log_a = [a.astype(F32) for a in (q, k, v, log_a)]
    if pad:
        padw = ((0, 0), (0, pad), (0, 0), (0, 0))
        q, k, v, log_a = [jnp.pad(a, padw) for a in (q, k, v, log_a)]

    def blocks(a):
        return a.reshape(b, n, c, h, a.shape[-1]).transpose(1, 0, 3, 2, 4)

    qc, kc, vc, gc = map(blocks, (q, k, v, log_a))
    cum = jnp.cumsum(gc, axis=3)
    total = cum[:, :, :, -1:, :]
    q_dec = qc * jnp.exp(cum)
    k_dec = kc * jnp.exp(-cum)
    k_to_end = kc * jnp.exp(total - cum)
    causal = jnp.tril(jnp.ones((c, c), dtype=bool))
    attn = jnp.where(causal, jnp.einsum('nbhtd,nbhsd->nbhts', q_dec, k_dec), 0.0)
    o_intra = jnp.einsum('nbhts,nbhsv->nbhtv', attn, vc)

    def step(s, inp):
        qd, kte, vv, tot = inp
        o_inter = jnp.einsum('bhtd,bhdv->bhtv', qd, s)
        s_new = s * jnp.exp(tot[:, :, 0, :])[..., None] + jnp.einsum('bhsd,bhsv->bhdv', kte, vv)
        return s_new, o_inter

    s_fin, o_inter = lax.scan(step, s0.astype(F32), (q_dec, k_to_end, vc, total))
    o = (o_intra + o_inter).transpose(1, 0, 3, 2, 4).reshape(b, n * c, h, dv)[:, :t]
    return o, s_fin


def dilated_branch_prompt(q, k, v, window, dil, slopes):
    b, t, h, hd = q.shape
    n = window // dil
    L = t // dil
    nb = -(-L // n)
    pad = nb * n - L

    def by_residue(a):
        a = a.reshape(b, L, dil, h, hd).transpose(0, 2, 3, 1, 4)
        a = jnp.pad(a, ((0, 0), (0, 0), (0, 0), (0, pad), (0, 0)))
        return a.reshape(b, dil, h, nb, n, hd)

    def with_prev(a):
        prev = jnp.pad(a, ((0, 0), (0, 0), (0, 0), (1, 0), (0, 0), (0, 0)))[:, :, :, :-1]
        return jnp.concatenate([prev, a], axis=4)

    qb = by_residue(q)
    k2 = with_prev(by_residue(k))
    v2 = with_prev(by_residue(v))
    s = jnp.einsum('brhnqd,brhnkd->brhnqk', qb, k2) * (SWA_HD ** -0.5)
    steps = jnp.arange(n)[:, None] + n - jnp.arange(2 * n)[None, :]
    blk = jnp.arange(nb)[:, None, None]
    valid = (steps >= 0) & (steps <= n) & (blk * n + jnp.arange(2 * n)[None, None, :] - n >= 0)
    s = s - slopes[:, None, None, None] * (dil * steps).astype(F32)
    s = jnp.where(valid, s, -jnp.inf)
    lse = jax.nn.logsumexp(s, axis=-1)
    o = jnp.einsum('brhnqk,brhnkd->brhnqd', jnp.exp(s - lse[..., None]), v2)

    def back(a):
        a = a.reshape((b, dil, h, nb * n) + a.shape[5:])[:, :, :, :L]
        a = jnp.moveaxis(a, 3, 1)
        return a.reshape((b, t, h) + a.shape[4:])

    return back(o), back(lse)


def dilated_branch_sample(q, k_all, v_all, window, dil, slopes):
    b, tn, h, hd = q.shape
    w_rows = k_all.shape[1] - tn
    n = window // dil
    steps = jnp.arange(n + 1)
    idx = w_rows + jnp.arange(tn)[:, None] - dil * steps[None, :]
    valid = idx >= 0
    idx = jnp.clip(idx, 0)
    kg = k_all[:, idx]
    vg = v_all[:, idx]
    s = jnp.einsum('bqhd,bqkhd->bhqk', q, kg) * (SWA_HD ** -0.5)
    s = s - slopes[:, None, None] * (dil * steps).astype(F32)
    s = jnp.where(valid, s, -jnp.inf)
    lse = jax.nn.logsumexp(s, axis=-1)
    o = jnp.einsum('bhqk,bqkhd->bqhd', jnp.exp(s - lse[..., None]), vg)
    return o, lse.transpose(0, 2, 1)


def combine_branches(outs, lses):
    w = jax.nn.softmax(jnp.stack(lses), axis=0)
    return jnp.einsum('gbth,gbthd->bthd', w, jnp.stack(outs))


def conv_ffn(x, conv_s0, w_up, w_conv, b_conv, w_down):
    t = x.shape[1]
    u = x @ w_up
    u_ext = jnp.concatenate([conv_s0.astype(u.dtype), u], axis=1)
    c = b_conv
    for j in range(CONV_W):
        c = c + w_conv[j] * u_ext[:, j:j + t]
    a, val = jnp.split(c, 2, axis=-1)
    y = (jax.nn.silu(a) * val) @ w_down
    return y, u_ext[:, t:]


def hybrid_layer(x, gla_s0, conv_s0, swa_past, g_attn_norm, w_in, w_gate_up, b_gate, g_gla_norm, w_out,
                 g_ffn_norm, w_up, w_conv, b_conv, w_down):
    b, t, _ = x.shape
    xn = rms_norm(x, g_attn_norm)
    proj = xn @ w_in
    qa, ka, va, ga, za, qb, kb, vb = jnp.split(proj, list(np.cumsum(IN_SIZES)[:-1]), axis=-1)

    log_a = jax.nn.log_sigmoid((za @ w_gate_up + b_gate).astype(F32)) / GLA_TAU
    o_a, s_a = gla_recurrence(
        qa.reshape(b, t, GLA_HEADS, GLA_DK).astype(F32) * (GLA_DK ** -0.5),
        ka.reshape(b, t, GLA_HEADS, GLA_DK),
        va.reshape(b, t, GLA_HEADS, GLA_DV),
        log_a.reshape(b, t, GLA_HEADS, GLA_DK),
        gla_s0)
    o_a = rms_norm(o_a, g_gla_norm.reshape(GLA_HEADS, GLA_DV)).reshape(b, t, GLA_V_W)
    o_a = (o_a * jax.nn.silu(ga.astype(F32))).astype(x.dtype)

    qh = qb.reshape(b, t, SWA_HEADS, SWA_HD)
    kh = kb.reshape(b, t, SWA_HEADS, SWA_HD)
    vh = vb.reshape(b, t, SWA_HEADS, SWA_HD)
    slopes = alibi_slopes(SWA_HEADS)
    if swa_past is None:
        qf, kf, vf = qh.astype(F32), kh.astype(F32), vh.astype(F32)
        res = [dilated_branch_prompt(qf, kf, vf, w, d, slopes) for (w, d) in SWA_PATTERNS]
        keep = min(SWA_MAX_WINDOW, t)
        k_buf, v_buf = kh[:, t - keep:], vh[:, t - keep:]
    else:
        k_all = jnp.concatenate([swa_past[0].astype(kh.dtype), kh], axis=1)
        v_all = jnp.concatenate([swa_past[1].astype(vh.dtype), vh], axis=1)
        qf, kf, vf = qh.astype(F32), k_all.astype(F32), v_all.astype(F32)
        res = [dilated_branch_sample(qf, kf, vf, w, d, slopes) for (w, d) in SWA_PATTERNS]
        keep = min(SWA_MAX_WINDOW, k_all.shape[1])
        k_buf, v_buf = k_all[:, k_all.shape[1] - keep:], v_all[:, v_all.shape[1] - keep:]
    o_b = combine_branches([r[0] for r in res], [r[1] for r in res]).reshape(b, t, SWA_W).astype(x.dtype)

    h = x + jnp.concatenate([o_a, o_b], axis=-1) @ w_out
    f, conv_new = conv_ffn(rms_norm(h, g_ffn_norm), conv_s0, w_up, w_conv, b_conv, w_down)
    return h + f, s_a, k_buf, v_buf, conv_new


def setup_inputs(seed: int = 0) -> dict:
    key = jax.random.key(seed)
    ks = jax.random.split(key, 20)
    w_past = min(SWA_MAX_WINDOW, PAST_LEN)
    nrm = lambda k, shape: jax.random.normal(k, shape, F32)
    return {
        "x_prompt": nrm(ks[0], (BATCH, SEQ, D_MODEL)),
        "x_sample": nrm(ks[1], (DEC_BATCH, DEC_SEQ, D_MODEL)),
        "state_gla": nrm(ks[2], (DEPTH, DEC_BATCH, GLA_HEADS, GLA_DK, GLA_DV)),
        "cache_swa_k": nrm(ks[3], (DEPTH, DEC_BATCH, w_past, SWA_HEADS, SWA_HD)),
        "cache_swa_v": nrm(ks[4], (DEPTH, DEC_BATCH, w_past, SWA_HEADS, SWA_HD)),
        "state_ffn_conv": nrm(ks[5], (DEPTH, DEC_BATCH, CONV_W - 1, 2 * D_FF)),
        "g_attn_norm": 1.0 + 0.02 * nrm(ks[6], (DEPTH, D_MODEL)),
        "w_in": nrm(ks[7], (DEPTH, D_MODEL, IN_COLS)) * D_MODEL ** -0.5,
        "w_gate_up": nrm(ks[8], (DEPTH, GLA_GATE_RANK, GLA_QK_W)) * GLA_GATE_RANK ** -0.5,
        "b_gate": 0.1 * nrm(ks[9], (DEPTH, GLA_QK_W)),
        "g_gla_norm": 1.0 + 0.02 * nrm(ks[10], (DEPTH, GLA_V_W)),
        "w_out": nrm(ks[11], (DEPTH, MIX_WIDTH, D_MODEL)) * MIX_WIDTH ** -0.5,
        "g_ffn_norm": 1.0 + 0.02 * nrm(ks[12], (DEPTH, D_MODEL)),
        "w_up": nrm(ks[13], (DEPTH, D_MODEL, 2 * D_FF)) * D_MODEL ** -0.5,
        "w_conv": nrm(ks[14], (DEPTH, CONV_W, 2 * D_FF)) * CONV_W ** -0.5,
        "b_conv": 0.02 * nrm(ks[15], (DEPTH, 2 * D_FF)),
        "w_down": nrm(ks[16], (DEPTH, D_FF, D_MODEL)) * D_FF ** -0.5,
        "g_final": 1.0 + 0.02 * nrm(ks[17], (D_MODEL,)),
    }


def reference(x_prompt, x_sample, state_gla, cache_swa_k, cache_swa_v, state_ffn_conv,
              g_attn_norm, w_in, w_gate_up, b_gate, g_gla_norm, w_out,
              g_ffn_norm, w_up, w_conv, b_conv, w_down, g_final):
    yp, ys = x_prompt, x_sample
    gla_p, gla_s, kp, ksm, vp, vsm, cp, csm = [], [], [], [], [], [], [], []
    for l in range(DEPTH):
        params = (g_attn_norm[l], w_in[l], w_gate_up[l], b_gate[l], g_gla_norm[l], w_out[l],
                  g_ffn_norm[l], w_up[l], w_conv[l], b_conv[l], w_down[l])
        gla0 = jnp.zeros((yp.shape[0], GLA_HEADS, GLA_DK, GLA_DV), F32)
        conv0 = jnp.zeros((yp.shape[0], CONV_W - 1, 2 * D_FF), yp.dtype)
        yp, s1, k1, v1, c1 = hybrid_layer(yp, gla0, conv0, None, *params)
        ys, s2, k2, v2, c2 = hybrid_layer(ys, state_gla[l], state_ffn_conv[l],
                                          (cache_swa_k[l], cache_swa_v[l]), *params)
        gla_p.append(s1); gla_s.append(s2)
        kp.append(k1); ksm.append(k2)
        vp.append(v1); vsm.append(v2)
        cp.append(c1); csm.append(c2)
    y_prompt = rms_norm(yp, g_final)
    y_sample = rms_norm(ys, g_final)
    state_gla_prompt = jnp.stack(gla_p)
    state_gla_sample = jnp.stack(gla_s)
    cache_swa_k_prompt = jnp.stack(kp)
    cache_swa_k_sample = jnp.stack(ksm)
    cache_swa_v_prompt = jnp.stack(vp)
    cache_swa_v_sample = jnp.stack(vsm)
    state_ffn_conv_prompt = jnp.stack(cp)
    state_ffn_conv_sample = jnp.stack(csm)
    return (y_prompt, y_sample, state_gla_prompt, state_gla_sample, cache_swa_k_prompt, cache_swa_k_sample,
            cache_swa_v_prompt, cache_swa_v_sample, state_ffn_conv_prompt, state_ffn_conv_sample)
```

```python
import functools

import jax
import jax.numpy as jnp
from jax import lax
from jax.experimental import pallas as pl
from jax.experimental.pallas import tpu as pltpu

F32 = jnp.float32
BF16 = jnp.bfloat16

GLA_HEADS = 4
GLA_DK = 128
GLA_DV = 256
GLA_GATE_RANK = 16
GLA_TAU = 16.0
GLA_CHUNK = 64
SWA_HEADS = 8
SWA_HD = 128
SWA_PATTERNS = ((128, 1), (512, 4), (2048, 16))
SWA_BLOCK = 128
SWA_MAX_WINDOW = 2048
CONV_W = 3
EPS = 1e-6
NEG = -1e30

GLA_QK_W = GLA_HEADS * GLA_DK
GLA_V_W = GLA_HEADS * GLA_DV
SWA_W = SWA_HEADS * SWA_HD
PROJ_W = 2 * GLA_QK_W + 2 * GLA_V_W + 3 * SWA_W
LANE = 128
VMEM_LIMIT = 56 * 1024 * 1024

ALIBI_SLOPES = tuple(2.0 ** (-8.0 * (h + 1) / SWA_HEADS) for h in range(SWA_HEADS))


def _params(semantics):
    return pltpu.CompilerParams(dimension_semantics=semantics, vmem_limit_bytes=VMEM_LIMIT)


def _rms_rows(x, g):
    return x * lax.rsqrt(jnp.mean(x * x, axis=-1, keepdims=True) + EPS) * g


def _silu(x):
    return x / (1.0 + jnp.exp(-x))


def _log_sigmoid(x):
    return jnp.minimum(x, 0.0) - jnp.log1p(jnp.exp(-jnp.abs(x)))


def _dot(a, b):
    return jnp.dot(a, b, preferred_element_type=F32)


def _dot_nt(a, b):
    return lax.dot_general(a, b, (((1,), (1,)), ((), ())), preferred_element_type=F32)


def _dot_tn(a, b):
    return lax.dot_general(a, b, (((0,), (0,)), ((), ())), preferred_element_type=F32)


def _inproj_kernel(x_ref, g_ref, w_ref, wz_ref, wg_ref, bg_ref, proj_ref, loga_ref, xn_ref):
    @pl.when(pl.program_id(1) == 0)
    def _():
        xn_ref[...] = _rms_rows(x_ref[...], g_ref[...]).astype(BF16)
        za = _dot(xn_ref[...], wz_ref[...])
        gate = _dot(za.astype(BF16), wg_ref[...]) + bg_ref[...]
        loga_ref[...] = _log_sigmoid(gate) / GLA_TAU

    proj_ref[...] = _dot(xn_ref[...], w_ref[...])


def _inproj(x, g, w_main, w_z, w_gate, b_gate, *, tm, tn):
    m, d = x.shape
    return pl.pallas_call(
        _inproj_kernel,
        grid=(m // tm, PROJ_W // tn),
        in_specs=[
            pl.BlockSpec((tm, d), lambda i, j: (i, 0)),
            pl.BlockSpec((1, d), lambda i, j: (0, 0)),
            pl.BlockSpec((d, tn), lambda i, j: (0, j)),
            pl.BlockSpec((d, LANE), lambda i, j: (0, 0)),
            pl.BlockSpec((LANE, GLA_QK_W), lambda i, j: (0, 0)),
            pl.BlockSpec((1, GLA_QK_W), lambda i, j: (0, 0)),
        ],
        out_specs=[
            pl.BlockSpec((tm, tn), lambda i, j: (i, j)),
            pl.BlockSpec((tm, GLA_QK_W), lambda i, j: (i, 0)),
        ],
        out_shape=[jax.ShapeDtypeStruct((m, PROJ_W), F32),
                   jax.ShapeDtypeStruct((m, GLA_QK_W), F32)],
        scratch_shapes=[pltpu.VMEM((tm, d), BF16)],
        compiler_params=_params(("parallel", "arbitrary")),
        name="norm_inproj",
    )(x, g, w_main, w_z, w_gate, b_gate)


def _split3(x):
    hi = x.astype(BF16)
    r1 = x - hi.astype(F32)
    mid = r1.astype(BF16)
    lo = (r1 - mid.astype(F32)).astype(BF16)
    return hi, mid, lo


def _gla_kernel(*refs, chunk, nsub, has_s0):
    if has_s0:
        q_ref, k_ref, v_ref, ga_ref, la_ref, gn_ref, s0_ref, o_ref, sfin_ref, s_scr = refs
    else:
        q_ref, k_ref, v_ref, ga_ref, la_ref, gn_ref, o_ref, sfin_ref, s_scr = refs
    c = pl.program_id(2)

    @pl.when(c == 0)
    def _():
        if has_s0:
            s_scr[...] = s0_ref[0, 0]
        else:
            s_scr[...] = jnp.zeros_like(s_scr)

    row = lax.broadcasted_iota(jnp.int32, (chunk, chunk), 0)
    col = lax.broadcasted_iota(jnp.int32, (chunk, chunk), 1)
    causal = row >= col
    tril = jnp.where(causal, 1.0, 0.0).astype(BF16)
    eye = (lax.broadcasted_iota(jnp.int32, (GLA_DK, GLA_DK), 0)
           == lax.broadcasted_iota(jnp.int32, (GLA_DK, GLA_DK), 1))
    gn = gn_ref[...]

    for i in range(nsub):
        sl = pl.ds(i * chunk, chunk)
        q = q_ref[0, sl, :].astype(F32) * (GLA_DK ** -0.5)
        k = k_ref[0, sl, :].astype(F32)
        v = v_ref[0, sl, :].astype(BF16)
        g_hi, g_mid, g_lo = _split3(la_ref[0, sl, :])
        cum = _dot(tril, g_hi) + _dot(tril, g_mid) + _dot(tril, g_lo)
        total = cum[chunk - 1:chunk, :]
        q_dec = (q * jnp.exp(cum)).astype(BF16)
        k_dec = (k * jnp.exp(-cum)).astype(BF16)
        k_end = (k * jnp.exp(total - cum)).astype(BF16)
        attn = jnp.where(causal, _dot_nt(q_dec, k_dec), 0.0).astype(BF16)
        s = s_scr[...]
        o = _dot(attn, v) + _dot(q_dec, s.astype(BF16))
        decay = jnp.sum(jnp.where(eye, jnp.exp(total), 0.0), axis=1, keepdims=True)
        s_scr[...] = s * decay + _dot_tn(k_end, v)
        on = _rms_rows(o, gn)
        o_ref[0, sl, :] = (on * _silu(ga_ref[0, sl, :].astype(F32))).astype(o_ref.dtype)

    @pl.when(c == pl.num_programs(2) - 1)
    def _():
        sfin_ref[0, 0] = s_scr[...]


def _gla(proj3, loga3, g_norm, s0, *, chunk, tc):
    b, t, _ = proj3.shape
    nsub = tc // chunk
    kq, kv = GLA_QK_W // GLA_DK, GLA_V_W // GLA_DV
    in_specs = [
        pl.BlockSpec((1, tc, GLA_DK), lambda bi, h, c: (bi, c, h)),
        pl.BlockSpec((1, tc, GLA_DK), lambda bi, h, c: (bi, c, kq + h)),
        pl.BlockSpec((1, tc, GLA_DV), lambda bi, h, c: (bi, c, kv + h)),
        pl.BlockSpec((1, tc, GLA_DV), lambda bi, h, c: (bi, c, 2 * kv + h)),
        pl.BlockSpec((1, tc, GLA_DK), lambda bi, h, c: (bi, c, h)),
        pl.BlockSpec((1, GLA_DV), lambda bi, h, c: (0, h)),
    ]
    args = [proj3, proj3, proj3, proj3, loga3, g_norm]
    if s0 is not None:
        in_specs.append(pl.BlockSpec((1, 1, GLA_DK, GLA_DV), lambda bi, h, c: (bi, h, 0, 0)))
        args.append(s0)
    return pl.pallas_call(
        functools.partial(_gla_kernel, chunk=chunk, nsub=nsub, has_s0=s0 is not None),
        grid=(b, GLA_HEADS, t // tc),
        in_specs=in_specs,
        out_specs=[
            pl.BlockSpec((1, tc, GLA_DV), lambda bi, h, c: (bi, c, h)),
            pl.BlockSpec((1, 1, GLA_DK, GLA_DV), lambda bi, h, c: (bi, h, 0, 0)),
        ],
        out_shape=[jax.ShapeDtypeStruct((b, t, GLA_V_W), BF16),
                   jax.ShapeDtypeStruct((b, GLA_HEADS, GLA_DK, GLA_DV), F32)],
        scratch_shapes=[pltpu.VMEM((GLA_DK, GLA_DV), F32)],
        compiler_params=_params(("parallel", "parallel", "arbitrary")),
        name="gla",
    )(*args)


def _swa_prompt_kernel(q_ref, k_ref, v_ref, kh_ref, vh_ref, o_ref, lse_ref, *, dil, nq):
    n = SWA_BLOCK
    has_prev = pl.program_id(2) > 0
    qi = lax.broadcasted_iota(jnp.int32, (n, n), 0)
    kj = lax.broadcasted_iota(jnp.int32, (n, n), 1)
    base_c = jnp.where(qi >= kj, -float(dil) * (qi - kj).astype(F32), NEG)
    base_p = jnp.where(kj >= qi, -float(dil) * (qi + n - kj).astype(F32), NEG)
    base_p0 = jnp.where(has_prev, base_p, NEG)
    scale = SWA_HD ** -0.5

    for h in range(SWA_HEADS):
        hs = slice(h * SWA_HD, (h + 1) * SWA_HD)
        slope = ALIBI_SLOPES[h]
        for i in range(nq):
            rows = slice(i * n, (i + 1) * n)
            q = q_ref[0, rows, hs].astype(BF16)
            kc = k_ref[0, rows, hs].astype(BF16)
            vc = v_ref[0, rows, hs].astype(BF16)
            if i == 0:
                kp = kh_ref[0, :, hs].astype(BF16)
                vp = vh_ref[0, :, hs].astype(BF16)
                bp = base_p0
            else:
                prev = slice((i - 1) * n, i * n)
                kp = k_ref[0, prev, hs].astype(BF16)
                vp = v_ref[0, prev, hs].astype(BF16)
                bp = base_p
            s_c = _dot_nt(q, kc) * scale + slope * base_c
            s_p = _dot_nt(q, kp) * scale + slope * bp
            m = jnp.maximum(jnp.max(s_c, axis=1, keepdims=True), jnp.max(s_p, axis=1, keepdims=True))
            p_c = jnp.exp(s_c - m)
            p_p = jnp.exp(s_p - m)
            l = jnp.sum(p_c, axis=1, keepdims=True) + jnp.sum(p_p, axis=1, keepdims=True)
            acc = _dot(p_c.astype(BF16), vc) + _dot(p_p.astype(BF16), vp)
            o_ref[0, rows, hs] = (acc / l).astype(o_ref.dtype)
            lse_ref[0, 0, rows, h:h + 1] = m + jnp.log(l)


def _swa_prompt_branch(proj3, dil, *, lc):
    b, t, _ = proj3.shape
    n = SWA_BLOCK
    length = t // dil
    lc = min(lc, length)
    view = proj3.reshape(b, length, dil * PROJ_W)
    nblk = PROJ_W // SWA_W
    qb, kb, vb = nblk - 3, nblk - 2, nblk - 1
    per = lc // n

    def main(cb):
        return pl.BlockSpec((1, lc, SWA_W), lambda bi, r, c: (bi, c, r * nblk + cb))

    def halo(cb):
        return pl.BlockSpec((1, n, SWA_W), lambda bi, r, c: (bi, jnp.maximum(c * per - 1, 0), r * nblk + cb))

    o, lse = pl.pallas_call(
        functools.partial(_swa_prompt_kernel, dil=dil, nq=per),
        grid=(b, dil, length // lc),
        in_specs=[main(qb), main(kb), main(vb), halo(kb), halo(vb)],
        out_specs=[
            pl.BlockSpec((1, lc, SWA_W), lambda bi, r, c: (bi, c, r)),
            pl.BlockSpec((1, 1, lc, SWA_HEADS), lambda bi, r, c: (bi, r, c, 0)),
        ],
        out_shape=[jax.ShapeDtypeStruct((b, length, dil * SWA_W), BF16),
                   jax.ShapeDtypeStruct((b, dil, length, SWA_HEADS), F32)],
        compiler_params=_params(("parallel", "parallel", "arbitrary")),
        name=f"swa_prompt_d{dil}",
    )(view, view, view, view, view)
    o = o.reshape(b * t, SWA_W)
    lse = lse.transpose(0, 2, 1, 3).reshape(b * t, SWA_HEADS)
    return o, lse


def _branch_multiplicity(d):
    mult = jnp.zeros(d.shape, F32)
    for window, dil in SWA_PATTERNS:
        hit = (d >= 0) & (d <= window) & ((d & (dil - 1)) == 0)
        mult = mult + jnp.where(hit, 1.0, 0.0)
    return mult


def _swa_sample_kernel(q_ref, kn_ref, vn_ref, kc_ref, vc_ref, o_ref, ko_ref, vo_ref, *, w, tn, hg):
    tp = q_ref.shape[1]
    ko_ref[0, 0:w - tn, :] = kc_ref[0, tn:w, :]
    ko_ref[0, w - tn:w, :] = kn_ref[0, 0:tn, :]
    vo_ref[0, 0:w - tn, :] = vc_ref[0, tn:w, :]
    vo_ref[0, w - tn:w, :] = vn_ref[0, 0:tn, :]

    d_c = (w + lax.broadcasted_iota(jnp.int32, (tp, w), 0)
           - lax.broadcasted_iota(jnp.int32, (tp, w), 1))
    mult_c = _branch_multiplicity(d_c)
    i_n = lax.broadcasted_iota(jnp.int32, (tp, tp), 0)
    j_n = lax.broadcasted_iota(jnp.int32, (tp, tp), 1)
    d_n = jnp.where(j_n < tn, i_n - j_n, -1)
    mult_n = _branch_multiplicity(d_n)
    dist_c = d_c.astype(F32)
    dist_n = d_n.astype(F32)
    scale = SWA_HD ** -0.5
    group = pl.program_id(1)

    for hh in range(hg):
        hs = slice(hh * SWA_HD, (hh + 1) * SWA_HD)
        slope = ALIBI_SLOPES[hh]
        for gi in range(1, SWA_HEADS // hg):
            slope = jnp.where(group == gi, ALIBI_SLOPES[gi * hg + hh], slope)
        q = q_ref[0, :, hs].astype(BF16)
        s_c = _dot_nt(q, kc_ref[0, :, hs].astype(BF16)) * scale - slope * dist_c
        s_n = _dot_nt(q, kn_ref[0, :, hs].astype(BF16)) * scale - slope * dist_n
        s_c = jnp.where(mult_c > 0.0, s_c, NEG)
        s_n = jnp.where(mult_n > 0.0, s_n, NEG)
        m = jnp.maximum(jnp.max(s_c, axis=1, keepdims=True), jnp.max(s_n, axis=1, keepdims=True))
        p_c = mult_c * jnp.exp(s_c - m)
        p_n = mult_n * jnp.exp(s_n - m)
        l = jnp.sum(p_c, axis=1, keepdims=True) + jnp.sum(p_n, axis=1, keepdims=True)
        acc = (_dot(p_c.astype(BF16), vc_ref[0, :, hs].astype(BF16))
               + _dot(p_n.astype(BF16), vn_ref[0, :, hs].astype(BF16)))
        o_ref[0, :, hs] = (acc / l).astype(o_ref.dtype)


def _swa_sample(projp, cache_k, cache_v, *, tn, hg):
    b, tp, _ = projp.shape
    w = cache_k.shape[1]
    wg = hg * SWA_HD
    ng = SWA_W // wg
    first = (PROJ_W - 3 * SWA_W) // wg

    def new(part):
        return pl.BlockSpec((1, tp, wg), lambda bi, g: (bi, 0, first + part * ng + g))

    cache = pl.BlockSpec((1, w, wg), lambda bi, g: (bi, 0, g))
    return pl.pallas_call(
        functools.partial(_swa_sample_kernel, w=w, tn=tn, hg=hg),
        grid=(b, ng),
        in_specs=[new(0), new(1), new(2), cache, cache],
        out_specs=[pl.BlockSpec((1, tp, wg), lambda bi, g: (bi, 0, g)), cache, cache],
        out_shape=[jax.ShapeDtypeStruct((b, tp, SWA_W), BF16),
                   jax.ShapeDtypeStruct((b, w, SWA_W), F32),
                   jax.ShapeDtypeStruct((b, w, SWA_W), F32)],
        compiler_params=_params(("parallel", "parallel")),
        name="swa_sample_cache",
    )(projp, projp, projp, cache_k, cache_v)


def _outproj_kernel(*refs, nbranch):
    x_ref, oa_ref = refs[0], refs[1]
    ob_refs = refs[2:2 + nbranch]
    lse_refs = refs[2 + nbranch:2 + 2 * nbranch] if nbranch > 1 else ()
    w_ref, h_ref, lhs_ref = refs[-3], refs[-2], refs[-1]

    @pl.when(pl.program_id(1) == 0)
    def _():
        lhs_ref[:, 0:GLA_V_W] = oa_ref[...]
        if nbranch == 1:
            lhs_ref[:, GLA_V_W:] = ob_refs[0][...]
        else:
            lses = [r[...] for r in lse_refs]
            m = functools.reduce(jnp.maximum, lses)
            es = [jnp.exp(x - m) for x in lses]
            inv = 1.0 / functools.reduce(lambda a, b: a + b, es)
            ws = [e * inv for e in es]
            for h in range(SWA_HEADS):
                hs = slice(h * SWA_HD, (h + 1) * SWA_HD)
                ob = functools.reduce(
                    lambda a, b: a + b,
                    [ws[g][:, h:h + 1] * ob_refs[g][:, hs].astype(F32) for g in range(nbranch)])
                lhs_ref[:, GLA_V_W + h * SWA_HD:GLA_V_W + (h + 1) * SWA_HD] = ob.astype(BF16)

    h_ref[...] = x_ref[...] + _dot(lhs_ref[...], w_ref[...])


def _outproj(x, o_a, o_bs, lses, w_out, *, tm, tn):
    m, d = x.shape
    nbranch = len(o_bs)
    mix = GLA_V_W + SWA_W
    in_specs = [pl.BlockSpec((tm, tn), lambda i, j: (i, j)),
                pl.BlockSpec((tm, GLA_V_W), lambda i, j: (i, 0))]
    in_specs += [pl.BlockSpec((tm, SWA_W), lambda i, j: (i, 0))] * nbranch
    if nbranch > 1:
        in_specs += [pl.BlockSpec((tm, SWA_HEADS), lambda i, j: (i, 0))] * nbranch
    in_specs.append(pl.BlockSpec((mix, tn), lambda i, j: (0, j)))
    args = [x, o_a, *o_bs, *(lses if nbranch > 1 else ()), w_out]
    return pl.pallas_call(
        functools.partial(_outproj_kernel, nbranch=nbranch),
        grid=(m // tm, d // tn),
        in_specs=in_specs,
        out_specs=pl.BlockSpec((tm, tn), lambda i, j: (i, j)),
        out_shape=jax.ShapeDtypeStruct((m, d), F32),
        scratch_shapes=[pltpu.VMEM((tm, mix), BF16)],
        compiler_params=_params(("parallel", "arbitrary")),
        name="combine_outproj",
    )(*args)


FFN_HALO = 16


def _ffn_prompt_kernel(h_ref, halo_ref, g_ref, wa_ref, wv_ref, ca_ref, cv_ref, ba_ref, bv_ref, wd_ref,
                       gf_ref, y_ref, ua_ref, uv_ref, xn_scr, ea_scr, ev_scr, acc_scr, *, tiles_per_seq):
    i, j = pl.program_id(0), pl.program_id(1)
    tm = h_ref.shape[0]

    @pl.when(j == 0)
    def _():
        first = (i % tiles_per_seq) == 0
        halo = _rms_rows(halo_ref[...], g_ref[...])
        xn_scr[0:FFN_HALO, :] = jnp.where(first, 0.0, halo).astype(BF16)
        xn_scr[FFN_HALO:, :] = _rms_rows(h_ref[...], g_ref[...]).astype(BF16)
        acc_scr[...] = jnp.zeros_like(acc_scr)

    xn = xn_scr[...]
    ea_scr[...] = _dot(xn, wa_ref[...])
    ev_scr[...] = _dot(xn, wv_ref[...])

    def conv(e_scr, c_ref, b_ref):
        out = b_ref[...]
        for tap in range(CONV_W):
            off = FFN_HALO - (CONV_W - 1) + tap
            out = out + c_ref[tap:tap + 1, :] * e_scr[off:off + tm, :]
        return out

    gate = (_silu(conv(ea_scr, ca_ref, ba_ref)) * conv(ev_scr, cv_ref, bv_ref)).astype(BF16)
    acc_scr[...] += _dot(gate, wd_ref[...])
    ua_ref[0] = ea_scr[FFN_HALO + tm - 8:FFN_HALO + tm, :]
    uv_ref[0] = ev_scr[FFN_HALO + tm - 8:FFN_HALO + tm, :]

    @pl.when(j == pl.num_programs(1) - 1)
    def _():
        y_ref[...] = _rms_rows(h_ref[...] + acc_scr[...], gf_ref[...])


def _ffn_prompt(h, g_ffn, w_up, w_conv, b_conv, w_down, g_final, *, seq, tm, tf):
    m, d = h.shape
    f = w_down.shape[0]
    nf = f // tf
    ntile = m // tm
    hb = tm // FFN_HALO
    y, ua, uv = pl.pallas_call(
        functools.partial(_ffn_prompt_kernel, tiles_per_seq=seq // tm),
        grid=(ntile, nf),
        in_specs=[
            pl.BlockSpec((tm, d), lambda i, j: (i, 0)),
            pl.BlockSpec((FFN_HALO, d), lambda i, j: (jnp.maximum(i * hb - 1, 0), 0)),
            pl.BlockSpec((1, d), lambda i, j: (0, 0)),
            pl.BlockSpec((d, tf), lambda i, j: (0, j)),
            pl.BlockSpec((d, tf), lambda i, j: (0, nf + j)),
            pl.BlockSpec((CONV_W, tf), lambda i, j: (0, j)),
            pl.BlockSpec((CONV_W, tf), lambda i, j: (0, nf + j)),
            pl.BlockSpec((1, tf), lambda i, j: (0, j)),
            pl.BlockSpec((1, tf), lambda i, j: (0, nf + j)),
            pl.BlockSpec((tf, d), lambda i, j: (j, 0)),
            pl.BlockSpec((1, d), lambda i, j: (0, 0)),
        ],
        out_specs=[
            pl.BlockSpec((tm, d), lambda i, j: (i, 0)),
            pl.BlockSpec((1, 8, tf), lambda i, j: (i, 0, j)),
            pl.BlockSpec((1, 8, tf), lambda i, j: (i, 0, j)),
        ],
        out_shape=[jax.ShapeDtypeStruct((m, d), F32),
                   jax.ShapeDtypeStruct((ntile, 8, f), F32),
                   jax.ShapeDtypeStruct((ntile, 8, f), F32)],
        scratch_shapes=[pltpu.VMEM((tm + FFN_HALO, d), BF16),
                        pltpu.VMEM((tm + FFN_HALO, tf), F32),
                        pltpu.VMEM((tm + FFN_HALO, tf), F32),
                        pltpu.VMEM((tm, d), F32)],
        compiler_params=_params(("parallel", "arbitrary")),
        name="convffn_prompt",
    )(h, h, g_ffn, w_up, w_up, w_conv, w_conv, b_conv, b_conv, w_down, g_final)
    return y, ua, uv


def _ffn_sample_kernel(h_ref, g_ref, sa_ref, sv_ref, wa_ref, wv_ref, ca_ref, cv_ref, ba_ref, bv_ref, wd_ref,
                       gf_ref, y_ref, na_ref, nv_ref, xn_scr, acc_scr, *, nb):
    j = pl.program_id(0)
    m = h_ref.shape[0]
    keep = (CONV_W - 1) * nb

    @pl.when(j == 0)
    def _():
        xn_scr[...] = _rms_rows(h_ref[...], g_ref[...]).astype(BF16)
        acc_scr[...] = jnp.zeros_like(acc_scr)

    xn = xn_scr[...]

    def conv(s_ref, w_ref, c_ref, b_ref, n_ref):
        u = _dot(xn, w_ref[...])
        ext = jnp.concatenate([s_ref[...], u], axis=0)
        n_ref[...] = ext[m:m + keep, :]
        out = b_ref[...]
        for tap in range(CONV_W):
            out = out + c_ref[tap:tap + 1, :] * ext[tap * nb:tap * nb + m, :]
        return out

    gate = (_silu(conv(sa_ref, wa_ref, ca_ref, ba_ref, na_ref))
            * conv(sv_ref, wv_ref, cv_ref, bv_ref, nv_ref)).astype(BF16)
    acc_scr[...] += _dot(gate, wd_ref[...])

    @pl.when(j == pl.num_programs(0) - 1)
    def _():
        y_ref[...] = _rms_rows(h_ref[...] + acc_scr[...], gf_ref[...])


def _ffn_sample(h_tm, state_tm, g_ffn, w_up, w_conv, b_conv, w_down, g_final, *, nb, tf):
    m, d = h_tm.shape
    f = w_down.shape[0]
    nf = f // tf
    keep = (CONV_W - 1) * nb
    return pl.pallas_call(
        functools.partial(_ffn_sample_kernel, nb=nb),
        grid=(nf,),
        in_specs=[
            pl.BlockSpec((m, d), lambda j: (0, 0)),
            pl.BlockSpec((1, d), lambda j: (0, 0)),
            pl.BlockSpec((keep, tf), lambda j: (0, j)),
            pl.BlockSpec((keep, tf), lambda j: (0, nf + j)),
            pl.BlockSpec((d, tf), lambda j: (0, j)),
            pl.BlockSpec((d, tf), lambda j: (0, nf + j)),
            pl.BlockSpec((CONV_W, tf), lambda j: (0, j)),
            pl.BlockSpec((CONV_W, tf), lambda j: (0, nf + j)),
            pl.BlockSpec((1, tf), lambda j: (0, j)),
            pl.BlockSpec((1, tf), lambda j: (0, nf + j)),
            pl.BlockSpec((tf, d), lambda j: (j, 0)),
            pl.BlockSpec((1, d), lambda j: (0, 0)),
        ],
        out_specs=[
            pl.BlockSpec((m, d), lambda j: (0, 0)),
            pl.BlockSpec((keep, tf), lambda j: (0, j)),
            pl.BlockSpec((keep, tf), lambda j: (0, j)),
        ],
        out_shape=[jax.ShapeDtypeStruct((m, d), F32),
                   jax.ShapeDtypeStruct((keep, f), F32),
                   jax.ShapeDtypeStruct((keep, f), F32)],
        scratch_shapes=[pltpu.VMEM((m, d), BF16), pltpu.VMEM((m, d), F32)],
        compiler_params=_params(("arbitrary",)),
        name="convffn_sample",
    )(h_tm, g_ffn, state_tm, state_tm, w_up, w_up, w_conv, w_conv, b_conv, b_conv, w_down, g_final)


def _pack_weights(g_attn_norm, w_in, w_gate_up, b_gate, g_gla_norm, w_out, g_ffn_norm, w_up, w_conv, b_conv,
                  w_down):
    z0 = 2 * GLA_QK_W + 2 * GLA_V_W
    z1 = z0 + GLA_GATE_RANK
    w_main = jnp.concatenate([w_in[:, :z0], w_in[:, z1:]], axis=1).astype(BF16)
    w_z = jnp.pad(w_in[:, z0:z1], ((0, 0), (0, LANE - GLA_GATE_RANK))).astype(BF16)
    w_gate = jnp.pad(w_gate_up, ((0, LANE - GLA_GATE_RANK), (0, 0))).astype(BF16)
    return dict(
        g_attn=g_attn_norm[None, :], w_main=w_main, w_z=w_z, w_gate=w_gate, b_gate=b_gate[None, :],
        g_gla=g_gla_norm[None, :], w_out=w_out.astype(BF16), g_ffn=g_ffn_norm[None, :],
        w_up=w_up.astype(BF16), w_conv=w_conv, b_conv=b_conv[None, :], w_down=w_down.astype(BF16))


def _prompt_layer(x, p, g_final):
    b, t, d = x.shape
    m = b * t
    f = p["w_down"].shape[0]
    x2 = x.reshape(m, d)
    proj, loga = _inproj(x2, p["g_attn"], p["w_main"], p["w_z"], p["w_gate"], p["b_gate"], tm=1024, tn=512)
    proj3 = proj.reshape(b, t, PROJ_W)
    o_a, s_fin = _gla(proj3, loga.reshape(b, t, GLA_QK_W), p["g_gla"], None, chunk=GLA_CHUNK, tc=512)
    branches = [_swa_prompt_branch(proj3, dil, lc=1024) for _, dil in SWA_PATTERNS]
    h = _outproj(x2, o_a.reshape(m, GLA_V_W), [o for o, _ in branches], [l for _, l in branches], p["w_out"],
                 tm=512, tn=1024)
    tm = 512
    y, ua, uv = _ffn_prompt(h, p["g_ffn"], p["w_up"], p["w_conv"], p["b_conv"], p["w_down"], g_final,
                            seq=t, tm=tm, tf=512)
    keep = min(SWA_MAX_WINDOW, t)
    kcol = PROJ_W - 2 * SWA_W
    k_buf = proj3[:, t - keep:, kcol:kcol + SWA_W].reshape(b, keep, SWA_HEADS, SWA_HD)
    v_buf = proj3[:, t - keep:, kcol + SWA_W:].reshape(b, keep, SWA_HEADS, SWA_HD)
    last = jnp.concatenate([ua, uv], axis=-1).reshape(b, t // tm, 8, 2 * f)[:, -1, 8 - (CONV_W - 1):, :]
    return y.reshape(b, t, d), s_fin, k_buf, v_buf, last


def _sample_layer(x, s0, conv_s0, cache_k, cache_v, p, g_final):
    b, tn, d = x.shape
    m = b * tn
    tp = 8
    w = cache_k.shape[1]
    x2 = x.reshape(m, d)
    proj, loga = _inproj(x2, p["g_attn"], p["w_main"], p["w_z"], p["w_gate"], p["b_gate"], tm=m, tn=512)
    pad = ((0, 0), (0, tp - tn), (0, 0))
    projp = jnp.pad(proj.reshape(b, tn, PROJ_W), pad)
    logap = jnp.pad(loga.reshape(b, tn, GLA_QK_W), pad)
    o_a, s_fin = _gla(projp, logap, p["g_gla"], s0, chunk=tp, tc=tp)
    o_b, k_new, v_new = _swa_sample(projp, cache_k.reshape(b, w, SWA_W), cache_v.reshape(b, w, SWA_W),
                                    tn=tn, hg=4)
    h = _outproj(x2, o_a[:, :tn].reshape(m, GLA_V_W), [o_b[:, :tn].reshape(m, SWA_W)], None, p["w_out"],
                 tm=m, tn=1024)
    h_tm = h.reshape(b, tn, d).transpose(1, 0, 2).reshape(m, d)
    state_tm = conv_s0.transpose(1, 0, 2).reshape((CONV_W - 1) * b, -1)
    y_tm, na, nv = _ffn_sample(h_tm, state_tm, p["g_ffn"], p["w_up"], p["w_conv"], p["b_conv"], p["w_down"],
                               g_final, nb=b, tf=512)
    y = y_tm.reshape(tn, b, d).transpose(1, 0, 2)
    conv_new = jnp.concatenate([na, nv], axis=-1).reshape(CONV_W - 1, b, -1).transpose(1, 0, 2)
    shape = (b, w, SWA_HEADS, SWA_HD)
    return y, s_fin, k_new.reshape(shape), v_new.reshape(shape), conv_new


def kernel(x_prompt, x_sample, state_gla, cache_swa_k, cache_swa_v, state_ffn_conv, g_attn_norm, w_in, w_gate_up,
           b_gate, g_gla_norm, w_out, g_ffn_norm, w_up, w_conv, b_conv, w_down, g_final):
    depth = w_in.shape[0]
    assert depth == 1, "the final RMSNorm is fused into the last layer's ConvFFN kernel"
    assert cache_swa_k.shape[2] == SWA_MAX_WINDOW and x_sample.shape[1] <= 8
    p = _pack_weights(g_attn_norm[0], w_in[0], w_gate_up[0], b_gate[0], g_gla_norm[0], w_out[0], g_ffn_norm[0],
                      w_up[0], w_conv[0], b_conv[0], w_down[0])
    gf = g_final[None, :]
    yp, s1, k1, v1, c1 = _prompt_layer(x_prompt, p, gf)
    ys, s2, k2, v2, c2 = _sample_layer(x_sample, state_gla[0], state_ffn_conv[0], cache_swa_k[0], cache_swa_v[0],
                                       p, gf)
    return (yp, ys, s1[None], s2[None], k1[None], k2[None], v1[None], v2[None], c1[None], c2[None])
```

```python
import functools

import jax
import jax.numpy as jnp
from jax import lax
from jax.experimental import pallas as pl
from jax.experimental.pallas import tpu as pltpu

F32 = jnp.float32
BF16 = jnp.bfloat16

GLA_HEADS = 4
GLA_DK = 128
GLA_DV = 256
GLA_GATE_RANK = 16
GLA_TAU = 16.0
GLA_CHUNK = 64
SWA_HEADS = 8
SWA_HD = 128
SWA_PATTERNS = ((128, 1), (512, 4), (2048, 16))
SWA_BLOCK = 128
SWA_MAX_WINDOW = 2048
CONV_W = 3
EPS = 1e-6
NEG = -1e30

GLA_QK_W = GLA_HEADS * GLA_DK
GLA_V_W = GLA_HEADS * GLA_DV
SWA_W = SWA_HEADS * SWA_HD
PROJ_W = 2 * GLA_QK_W + 2 * GLA_V_W + 3 * SWA_W
LANE = 128
VMEM_LIMIT = 56 * 1024 * 1024

ALIBI_SLOPES = tuple(2.0 ** (-8.0 * (h + 1) / SWA_HEADS) for h in range(SWA_HEADS))


def _params(semantics):
    return pltpu.CompilerParams(dimension_semantics=semantics, vmem_limit_bytes=VMEM_LIMIT)


def _rms_rows(x, g):
    return x * lax.rsqrt(jnp.mean(x * x, axis=-1, keepdims=True) + EPS) * g


def _silu(x):
    return x / (1.0 + jnp.exp(-x))


def _log_sigmoid(x):
    return jnp.minimum(x, 0.0) - jnp.log1p(jnp.exp(-jnp.abs(x)))


def _dot(a, b):
    return jnp.dot(a, b, preferred_element_type=F32)


def _dot_nt(a, b):
    return lax.dot_general(a, b, (((1,), (1,)), ((), ())), preferred_element_type=F32)


def _dot_tn(a, b):
    return lax.dot_general(a, b, (((0,), (0,)), ((), ())), preferred_element_type=F32)


def _inproj_kernel(x_ref, g_ref, w_ref, wz_ref, wg_ref, bg_ref, proj_ref, loga_ref, kc_ref, vc_ref, xn_ref):
    j = pl.program_id(1)
    tm = x_ref.shape[0]

    @pl.when(j == 0)
    def _():
        xn_ref[...] = _rms_rows(x_ref[...], g_ref[...]).astype(BF16)
        za = _dot(xn_ref[...], wz_ref[...])
        gate = _dot(za.astype(BF16), wg_ref[...]) + bg_ref[...]
        loga_ref[...] = _log_sigmoid(gate) / GLA_TAU

    res = _dot(xn_ref[...], w_ref[...])
    proj_ref[...] = res.astype(BF16)

    def to_cache(dst_ref):
        for h in range(SWA_HEADS):
            dst_ref[pl.ds(h, tm, stride=SWA_HEADS), :] = res[:, h * SWA_HD:(h + 1) * SWA_HD]

    @pl.when(j == PROJ_W // SWA_W - 2)
    def _():
        to_cache(kc_ref)

    @pl.when(j == PROJ_W // SWA_W - 1)
    def _():
        to_cache(vc_ref)


def _inproj(x, g, w_main, w_z, w_gate, b_gate, *, tm, seq, keep):
    m, d = x.shape
    tn = SWA_W
    tps, kt = seq // tm, keep // tm
    assert seq % tm == 0 and keep % tm == 0 and m % seq == 0

    def cache_block(i, j):
        return ((i // tps) * kt + jnp.maximum(i % tps - (tps - kt), 0), 0)

    cache_rows = (m // seq) * keep * SWA_HEADS
    return pl.pallas_call(
        _inproj_kernel,
        grid=(m // tm, PROJ_W // tn),
        in_specs=[
            pl.BlockSpec((tm, d), lambda i, j: (i, 0)),
            pl.BlockSpec((1, d), lambda i, j: (0, 0)),
            pl.BlockSpec((d, tn), lambda i, j: (0, j)),
            pl.BlockSpec((d, LANE), lambda i, j: (0, 0)),
            pl.BlockSpec((LANE, GLA_QK_W), lambda i, j: (0, 0)),
            pl.BlockSpec((1, GLA_QK_W), lambda i, j: (0, 0)),
        ],
        out_specs=[
            pl.BlockSpec((tm, tn), lambda i, j: (i, j)),
            pl.BlockSpec((tm, GLA_QK_W), lambda i, j: (i, 0)),
            pl.BlockSpec((tm * SWA_HEADS, SWA_HD), cache_block, pipeline_mode=pl.Buffered(1)),
            pl.BlockSpec((tm * SWA_HEADS, SWA_HD), cache_block, pipeline_mode=pl.Buffered(1)),
        ],
        out_shape=[jax.ShapeDtypeStruct((m, PROJ_W), BF16),
                   jax.ShapeDtypeStruct((m, GLA_QK_W), F32),
                   jax.ShapeDtypeStruct((cache_rows, SWA_HD), F32),
                   jax.ShapeDtypeStruct((cache_rows, SWA_HD), F32)],
        scratch_shapes=[pltpu.VMEM((tm, d), BF16)],
        compiler_params=_params(("arbitrary", "arbitrary")),
        name="norm_inproj",
    )(x, g, w_main, w_z, w_gate, b_gate)


def _split3(x):
    hi = x.astype(BF16)
    r1 = x - hi.astype(F32)
    mid = r1.astype(BF16)
    lo = (r1 - mid.astype(F32)).astype(BF16)
    return hi, mid, lo


def _gla_kernel(*refs, chunk, nsub, has_s0):
    if has_s0:
        q_ref, k_ref, v_ref, ga_ref, la_ref, gn_ref, s0_ref, o_ref, sfin_ref, s_scr = refs
    else:
        q_ref, k_ref, v_ref, ga_ref, la_ref, gn_ref, o_ref, sfin_ref, s_scr = refs
    c = pl.program_id(1)

    @pl.when(c == 0)
    def _():
        if has_s0:
            s_scr[...] = s0_ref[0]
        else:
            s_scr[...] = jnp.zeros_like(s_scr)

    row = lax.broadcasted_iota(jnp.int32, (chunk, chunk), 0)
    col = lax.broadcasted_iota(jnp.int32, (chunk, chunk), 1)
    causal = row >= col
    tril = jnp.where(causal, 1.0, 0.0).astype(BF16)
    eye = (lax.broadcasted_iota(jnp.int32, (GLA_DK, GLA_DK), 0)
           == lax.broadcasted_iota(jnp.int32, (GLA_DK, GLA_DK), 1))

    def sub_chunk(i, carry):
        sl = pl.ds(pl.multiple_of(i * chunk, chunk), chunk)
        for h in range(GLA_HEADS):
            ks = slice(h * GLA_DK, (h + 1) * GLA_DK)
            vs = slice(h * GLA_DV, (h + 1) * GLA_DV)
            q = q_ref[0, sl, ks].astype(F32) * (GLA_DK ** -0.5)
            k = k_ref[0, sl, ks].astype(F32)
            v = v_ref[0, sl, vs]
            g_hi, g_mid, g_lo = _split3(la_ref[0, sl, ks])
            cum = _dot(tril, g_hi) + _dot(tril, g_mid) + _dot(tril, g_lo)
            total = cum[chunk - 1:chunk, :]
            q_dec = (q * jnp.exp(cum)).astype(BF16)
            k_dec = (k * jnp.exp(-cum)).astype(BF16)
            k_end = (k * jnp.exp(total - cum)).astype(BF16)
            attn = jnp.where(causal, _dot_nt(q_dec, k_dec), 0.0).astype(BF16)
            s = s_scr[h]
            o = _dot(attn, v) + _dot(q_dec, s.astype(BF16))
            decay = jnp.sum(jnp.where(eye, jnp.exp(total), 0.0), axis=1, keepdims=True)
            s_scr[h] = s * decay + _dot_tn(k_end, v)
            on = _rms_rows(o, gn_ref[:, vs])
            o_ref[0, sl, vs] = (on * _silu(ga_ref[0, sl, vs].astype(F32))).astype(o_ref.dtype)
        return carry

    if nsub == 1:
        sub_chunk(0, 0)
    else:
        lax.fori_loop(0, nsub, sub_chunk, 0)

    @pl.when(c == pl.num_programs(1) - 1)
    def _():
        sfin_ref[0] = s_scr[...]


def _gla(proj3, loga3, g_norm, s0, *, chunk, tc):
    b, t, _ = proj3.shape
    nsub = tc // chunk
    state = pl.BlockSpec((1, GLA_HEADS, GLA_DK, GLA_DV), lambda bi, c: (bi, 0, 0, 0))
    in_specs = [
        pl.BlockSpec((1, tc, GLA_QK_W), lambda bi, c: (bi, c, 0)),
        pl.BlockSpec((1, tc, GLA_QK_W), lambda bi, c: (bi, c, 1)),
        pl.BlockSpec((1, tc, GLA_V_W), lambda bi, c: (bi, c, 1)),
        pl.BlockSpec((1, tc, GLA_V_W), lambda bi, c: (bi, c, 2)),
        pl.BlockSpec((1, tc, GLA_QK_W), lambda bi, c: (bi, c, 0)),
        pl.BlockSpec((1, GLA_V_W), lambda bi, c: (0, 0)),
    ]
    args = [proj3, proj3, proj3, proj3, loga3, g_norm]
    if s0 is not None:
        in_specs.append(state)
        args.append(s0)
    return pl.pallas_call(
        functools.partial(_gla_kernel, chunk=chunk, nsub=nsub, has_s0=s0 is not None),
        grid=(b, t // tc),
        in_specs=in_specs,
        out_specs=[pl.BlockSpec((1, tc, GLA_V_W), lambda bi, c: (bi, c, 0)), state],
        out_shape=[jax.ShapeDtypeStruct((b, t, GLA_V_W), BF16),
                   jax.ShapeDtypeStruct((b, GLA_HEADS, GLA_DK, GLA_DV), F32)],
        scratch_shapes=[pltpu.VMEM((GLA_HEADS, GLA_DK, GLA_DV), F32)],
        compiler_params=_params(("parallel", "arbitrary")),
        name="gla",
    )(*args)


def _swa_prompt_kernel(q_ref, k_ref, v_ref, kh_ref, vh_ref, o_ref, lse_ref, bias_scr, *, dil, nq):
    n = SWA_BLOCK
    has_prev = pl.program_id(2) > 0
    qi = lax.broadcasted_iota(jnp.int32, (n, 2 * n), 0)
    kj = lax.broadcasted_iota(jnp.int32, (n, 2 * n), 1)
    steps = qi + n - kj
    visible = (steps >= 0) & (steps <= n)
    dist = jnp.where(visible, -float(dil) * steps.astype(F32), NEG)
    for h in range(SWA_HEADS):
        bias_scr[h] = ALIBI_SLOPES[h] * dist
    first_visible = visible & (has_prev | (kj >= n))
    scale = SWA_HD ** -0.5

    def tile(row0, keys, values, mask):
        rows = pl.ds(row0, n)
        for h in range(SWA_HEADS):
            hs = slice(h * SWA_HD, (h + 1) * SWA_HD)
            s = _dot_nt(q_ref[0, rows, hs], keys(hs)) * scale + bias_scr[h]
            if mask is not None:
                s = jnp.where(mask, s, NEG)
            m = jnp.max(s, axis=1, keepdims=True)
            p = jnp.exp(s - m)
            l = jnp.sum(p, axis=1, keepdims=True)
            acc = _dot(p.astype(BF16), values(hs))
            o_ref[0, rows, hs] = (acc * (1.0 / l)).astype(o_ref.dtype)
            lse_ref[0, 0, rows, h:h + 1] = m + jnp.log(l)

    tile(0,
         lambda hs: jnp.concatenate([kh_ref[0, :, hs], k_ref[0, 0:n, hs]], axis=0),
         lambda hs: jnp.concatenate([vh_ref[0, :, hs], v_ref[0, 0:n, hs]], axis=0),
         first_visible)

    def body(i, carry):
        row0 = pl.multiple_of(i * n, n)
        both = pl.ds(pl.multiple_of(row0 - n, n), 2 * n)
        tile(row0, lambda hs: k_ref[0, both, hs], lambda hs: v_ref[0, both, hs], None)
        return carry

    if nq > 1:
        lax.fori_loop(1, nq, body, 0)


def _swa_prompt_branch(proj3, dil, *, lc):
    b, t, _ = proj3.shape
    n = SWA_BLOCK
    length = t // dil
    lc = min(lc, length)
    view = proj3.reshape(b, length, dil * PROJ_W)
    nblk = PROJ_W // SWA_W
    qb, kb, vb = nblk - 3, nblk - 2, nblk - 1
    per = lc // n

    def main(cb):
        return pl.BlockSpec((1, lc, SWA_W), lambda bi, r, c: (bi, c, r * nblk + cb))

    def halo(cb):
        return pl.BlockSpec((1, n, SWA_W), lambda bi, r, c: (bi, jnp.maximum(c * per - 1, 0), r * nblk + cb))

    o, lse = pl.pallas_call(
        functools.partial(_swa_prompt_kernel, dil=dil, nq=per),
        grid=(b, dil, length // lc),
        in_specs=[main(qb), main(kb), main(vb), halo(kb), halo(vb)],
        out_specs=[
            pl.BlockSpec((1, lc, SWA_W), lambda bi, r, c: (bi, c, r)),
            pl.BlockSpec((1, 1, lc, SWA_HEADS), lambda bi, r, c: (bi, r, c, 0)),
        ],
        out_shape=[jax.ShapeDtypeStruct((b, length, dil * SWA_W), BF16),
                   jax.ShapeDtypeStruct((b, dil, length, SWA_HEADS), F32)],
        scratch_shapes=[pltpu.VMEM((SWA_HEADS, n, 2 * n), F32)],
        compiler_params=_params(("parallel", "parallel", "arbitrary")),
        name=f"swa_prompt_d{dil}",
    )(view, view, view, view, view)
    o = o.reshape(b * t, SWA_W)
    lse = lse.transpose(0, 2, 1, 3).reshape(b * t, SWA_HEADS)
    return o, lse


def _branch_multiplicity(d):
    mult = jnp.zeros(d.shape, F32)
    for window, dil in SWA_PATTERNS:
        hit = (d >= 0) & (d <= window) & ((d & (dil - 1)) == 0)
        mult = mult + jnp.where(hit, 1.0, 0.0)
    return mult


def _swa_sample_kernel(q_ref, kn_ref, vn_ref, knf_ref, vnf_ref, kc_ref, vc_ref, o_ref, ko_hbm, vo_hbm, sems,
                       *, w, tn):
    bi = pl.program_id(0)
    tp = q_ref.shape[1]
    old, new = (w - tn) * SWA_HEADS, tn * SWA_HEADS
    copies = []
    for idx, (cache, fresh, out) in enumerate(((kc_ref, knf_ref, ko_hbm), (vc_ref, vnf_ref, vo_hbm))):
        copies.append(pltpu.make_async_copy(cache.at[0, pl.ds(new, old)], out.at[bi, pl.ds(0, old)],
                                            sems.at[2 * idx]))
        copies.append(pltpu.make_async_copy(fresh.at[0], out.at[bi, pl.ds(old, new)], sems.at[2 * idx + 1]))
    for cp in copies:
        cp.start()

    d_c = (w + lax.broadcasted_iota(jnp.int32, (tp, w), 0)
           - lax.broadcasted_iota(jnp.int32, (tp, w), 1))
    mult_c = _branch_multiplicity(d_c)
    i_n = lax.broadcasted_iota(jnp.int32, (tp, tp), 0)
    j_n = lax.broadcasted_iota(jnp.int32, (tp, tp), 1)
    d_n = jnp.where(j_n < tn, i_n - j_n, -1)
    mult_n = _branch_multiplicity(d_n)
    dist_c = d_c.astype(F32)
    dist_n = d_n.astype(F32)
    scale = SWA_HD ** -0.5

    for h in range(SWA_HEADS):
        hs = slice(h * SWA_HD, (h + 1) * SWA_HD)
        head_rows = pl.ds(h, w, stride=SWA_HEADS)
        slope = ALIBI_SLOPES[h]
        q = q_ref[0, :, hs]
        s_c = _dot_nt(q, kc_ref[0, head_rows, :].astype(BF16)) * scale - slope * dist_c
        s_n = _dot_nt(q, kn_ref[0, :, hs]) * scale - slope * dist_n
        s_c = jnp.where(mult_c > 0.0, s_c, NEG)
        s_n = jnp.where(mult_n > 0.0, s_n, NEG)
        m = jnp.maximum(jnp.max(s_c, axis=1, keepdims=True), jnp.max(s_n, axis=1, keepdims=True))
        p_c = mult_c * jnp.exp(s_c - m)
        p_n = mult_n * jnp.exp(s_n - m)
        l = jnp.sum(p_c, axis=1, keepdims=True) + jnp.sum(p_n, axis=1, keepdims=True)
        acc = (_dot(p_c.astype(BF16), vc_ref[0, head_rows, :].astype(BF16))
               + _dot(p_n.astype(BF16), vn_ref[0, :, hs]))
        o_ref[0, :, hs] = (acc / l).astype(o_ref.dtype)

    for cp in copies:
        cp.wait()


def _swa_sample(projp, k_new, v_new, cache_k, cache_v, *, tn):
    b, tp, _ = projp.shape
    rows = cache_k.shape[1]
    w = rows // SWA_HEADS
    nblk = PROJ_W // SWA_W

    def new(cb):
        return pl.BlockSpec((1, tp, SWA_W), lambda bi: (bi, 0, cb))

    fresh = pl.BlockSpec((1, tn * SWA_HEADS, SWA_HD), lambda bi: (bi, 0, 0))
    cache = pl.BlockSpec((1, rows, SWA_HD), lambda bi: (bi, 0, 0))
    hbm = pl.BlockSpec(memory_space=pl.ANY)
    return pl.pallas_call(
        functools.partial(_swa_sample_kernel, w=w, tn=tn),
        grid=(b,),
        in_specs=[new(nblk - 3), new(nblk - 2), new(nblk - 1), fresh, fresh, cache, cache],
        out_specs=[pl.BlockSpec((1, tp, SWA_W), lambda bi: (bi, 0, 0)), hbm, hbm],
        out_shape=[jax.ShapeDtypeStruct((b, tp, SWA_W), BF16),
                   jax.ShapeDtypeStruct((b, rows, SWA_HD), F32),
                   jax.ShapeDtypeStruct((b, rows, SWA_HD), F32)],
        scratch_shapes=[pltpu.SemaphoreType.DMA((4,))],
        compiler_params=_params(("arbitrary",)),
        name="swa_sample_cache",
    )(projp, projp, projp, k_new, v_new, cache_k, cache_v)


def _outproj_kernel(*refs, nbranch):
    x_ref, oa_ref = refs[0], refs[1]
    ob_refs = refs[2:2 + nbranch]
    lse_refs = refs[2 + nbranch:2 + 2 * nbranch] if nbranch > 1 else ()
    w_ref, h_ref, lhs_ref = refs[-3], refs[-2], refs[-1]

    lhs_ref[:, 0:GLA_V_W] = oa_ref[...]
    if nbranch == 1:
        lhs_ref[:, GLA_V_W:] = ob_refs[0][...]
    else:
        lses = [r[...] for r in lse_refs]
        m = functools.reduce(jnp.maximum, lses)
        es = [jnp.exp(x - m) for x in lses]
        inv = 1.0 / functools.reduce(lambda a, b: a + b, es)
        ws = [e * inv for e in es]
        for h in range(SWA_HEADS):
            hs = slice(h * SWA_HD, (h + 1) * SWA_HD)
            ob = functools.reduce(
                lambda a, b: a + b,
                [ws[g][:, h:h + 1] * ob_refs[g][:, hs].astype(F32) for g in range(nbranch)])
            lhs_ref[:, GLA_V_W + h * SWA_HD:GLA_V_W + (h + 1) * SWA_HD] = ob.astype(BF16)

    h_ref[...] = x_ref[...] + _dot(lhs_ref[...], w_ref[...])


def _outproj(x, o_a, o_bs, lses, w_out, *, tm):
    m, d = x.shape
    nbranch = len(o_bs)
    mix = GLA_V_W + SWA_W
    in_specs = [pl.BlockSpec((tm, d), lambda i: (i, 0)),
                pl.BlockSpec((tm, GLA_V_W), lambda i: (i, 0))]
    in_specs += [pl.BlockSpec((tm, SWA_W), lambda i: (i, 0))] * nbranch
    if nbranch > 1:
        in_specs += [pl.BlockSpec((tm, SWA_HEADS), lambda i: (i, 0))] * nbranch
    in_specs.append(pl.BlockSpec((mix, d), lambda i: (0, 0)))
    args = [x, o_a, *o_bs, *(lses if nbranch > 1 else ()), w_out]
    return pl.pallas_call(
        functools.partial(_outproj_kernel, nbranch=nbranch),
        grid=(m // tm,),
        in_specs=in_specs,
        out_specs=pl.BlockSpec((tm, d), lambda i: (i, 0)),
        out_shape=jax.ShapeDtypeStruct((m, d), F32),
        scratch_shapes=[pltpu.VMEM((tm, mix), BF16)],
        compiler_params=_params(("parallel",)),
        name="combine_outproj",
    )(*args)


FFN_HALO = 16


def _ffn_prompt_kernel(h_ref, halo_ref, g_ref, wa_ref, wv_ref, ca_ref, cv_ref, ba_ref, bv_ref, wd_ref,
                       gf_ref, y_ref, ua_ref, uv_ref, xn_scr, ea_scr, ev_scr, acc_scr, *, tiles_per_seq):
    i, j = pl.program_id(0), pl.program_id(1)
    tm = h_ref.shape[0]

    @pl.when(j == 0)
    def _():
        first = (i % tiles_per_seq) == 0
        halo = _rms_rows(halo_ref[...], g_ref[...])
        xn_scr[0:FFN_HALO, :] = jnp.where(first, 0.0, halo).astype(BF16)
        xn_scr[FFN_HALO:, :] = _rms_rows(h_ref[...], g_ref[...]).astype(BF16)
        acc_scr[...] = jnp.zeros_like(acc_scr)

    xn = xn_scr[...]
    ea_scr[...] = _dot(xn, wa_ref[...])
    ev_scr[...] = _dot(xn, wv_ref[...])

    def conv(e_scr, c_ref, b_ref):
        out = b_ref[...]
        for tap in range(CONV_W):
            off = FFN_HALO - (CONV_W - 1) + tap
            out = out + c_ref[tap:tap + 1, :] * e_scr[off:off + tm, :]
        return out

    gate = (_silu(conv(ea_scr, ca_ref, ba_ref)) * conv(ev_scr, cv_ref, bv_ref)).astype(BF16)
    acc_scr[...] += _dot(gate, wd_ref[...])
    ua_ref[0] = ea_scr[FFN_HALO + tm - 8:FFN_HALO + tm, :]
    uv_ref[0] = ev_scr[FFN_HALO + tm - 8:FFN_HALO + tm, :]

    @pl.when(j == pl.num_programs(1) - 1)
    def _():
        y_ref[...] = _rms_rows(h_ref[...] + acc_scr[...], gf_ref[...])


def _ffn_prompt(h, g_ffn, w_up, w_conv, b_conv, w_down, g_final, *, seq, tm, tf):
    m, d = h.shape
    f = w_down.shape[0]
    nf = f // tf
    ntile = m // tm
    hb = tm // FFN_HALO
    y, ua, uv = pl.pallas_call(
        functools.partial(_ffn_prompt_kernel, tiles_per_seq=seq // tm),
        grid=(ntile, nf),
        in_specs=[
            pl.BlockSpec((tm, d), lambda i, j: (i, 0)),
            pl.BlockSpec((FFN_HALO, d), lambda i, j: (jnp.maximum(i * hb - 1, 0), 0)),
            pl.BlockSpec((1, d), lambda i, j: (0, 0)),
            pl.BlockSpec((d, tf), lambda i, j: (0, j)),
            pl.BlockSpec((d, tf), lambda i, j: (0, nf + j)),
            pl.BlockSpec((CONV_W, tf), lambda i, j: (0, j)),
            pl.BlockSpec((CONV_W, tf), lambda i, j: (0, nf + j)),
            pl.BlockSpec((1, tf), lambda i, j: (0, j)),
            pl.BlockSpec((1, tf), lambda i, j: (0, nf + j)),
            pl.BlockSpec((tf, d), lambda i, j: (j, 0)),
            pl.BlockSpec((1, d), lambda i, j: (0, 0)),
        ],
        out_specs=[
            pl.BlockSpec((tm, d), lambda i, j: (i, 0)),
            pl.BlockSpec((1, 8, tf), lambda i, j: (i, 0, j)),
            pl.BlockSpec((1, 8, tf), lambda i, j: (i, 0, j)),
        ],
        out_shape=[jax.ShapeDtypeStruct((m, d), F32),
                   jax.ShapeDtypeStruct((ntile, 8, f), F32),
                   jax.ShapeDtypeStruct((ntile, 8, f), F32)],
        scratch_shapes=[pltpu.VMEM((tm + FFN_HALO, d), BF16),
                        pltpu.VMEM((tm + FFN_HALO, tf), F32),
                        pltpu.VMEM((tm + FFN_HALO, tf), F32),
                        pltpu.VMEM((tm, d), F32)],
        compiler_params=_params(("parallel", "arbitrary")),
        name="convffn_prompt",
    )(h, h, g_ffn, w_up, w_up, w_conv, w_conv, b_conv, b_conv, w_down, g_final)
    return y, ua, uv


def _ffn_sample_kernel(h_ref, g_ref, sa_ref, sv_ref, wa_ref, wv_ref, ca_ref, cv_ref, ba_ref, bv_ref, wd_ref,
                       gf_ref, y_ref, na_ref, nv_ref, xn_scr, acc_scr, *, nb):
    j = pl.program_id(0)
    m = h_ref.shape[0]
    keep = (CONV_W - 1) * nb

    @pl.when(j == 0)
    def _():
        xn_scr[...] = _rms_rows(h_ref[...], g_ref[...]).astype(BF16)
        acc_scr[...] = jnp.zeros_like(acc_scr)

    xn = xn_scr[...]

    def conv(s_ref, w_ref, c_ref, b_ref, n_ref):
        u = _dot(xn, w_ref[...])
        ext = jnp.concatenate([s_ref[...], u], axis=0)
        n_ref[...] = ext[m:m + keep, :]
        out = b_ref[...]
        for tap in range(CONV_W):
            out = out + c_ref[tap:tap + 1, :] * ext[tap * nb:tap * nb + m, :]
        return out

    gate = (_silu(conv(sa_ref, wa_ref, ca_ref, ba_ref, na_ref))
            * conv(sv_ref, wv_ref, cv_ref, bv_ref, nv_ref)).astype(BF16)
    acc_scr[...] += _dot(gate, wd_ref[...])

    @pl.when(j == pl.num_programs(0) - 1)
    def _():
        y_ref[...] = _rms_rows(h_ref[...] + acc_scr[...], gf_ref[...])


def _ffn_sample(h_tm, state_tm, g_ffn, w_up, w_conv, b_conv, w_down, g_final, *, nb, tf):
    m, d = h_tm.shape
    f = w_down.shape[0]
    nf = f // tf
    keep = (CONV_W - 1) * nb
    return pl.pallas_call(
        functools.partial(_ffn_sample_kernel, nb=nb),
        grid=(nf,),
        in_specs=[
            pl.BlockSpec((m, d), lambda j: (0, 0)),
            pl.BlockSpec((1, d), lambda j: (0, 0)),
            pl.BlockSpec((keep, tf), lambda j: (0, j)),
            pl.BlockSpec((keep, tf), lambda j: (0, nf + j)),
            pl.BlockSpec((d, tf), lambda j: (0, j)),
            pl.BlockSpec((d, tf), lambda j: (0, nf + j)),
            pl.BlockSpec((CONV_W, tf), lambda j: (0, j)),
            pl.BlockSpec((CONV_W, tf), lambda j: (0, nf + j)),
            pl.BlockSpec((1, tf), lambda j: (0, j)),
            pl.BlockSpec((1, tf), lambda j: (0, nf + j)),
            pl.BlockSpec((tf, d), lambda j: (j, 0)),
            pl.BlockSpec((1, d), lambda j: (0, 0)),
        ],
        out_specs=[
            pl.BlockSpec((m, d), lambda j: (0, 0)),
            pl.BlockSpec((keep, tf), lambda j: (0, j)),
            pl.BlockSpec((keep, tf), lambda j: (0, j)),
        ],
        out_shape=[jax.ShapeDtypeStruct((m, d), F32),
                   jax.ShapeDtypeStruct((keep, f), F32),
                   jax.ShapeDtypeStruct((keep, f), F32)],
        scratch_shapes=[pltpu.VMEM((m, d), BF16), pltpu.VMEM((m, d), F32)],
        compiler_params=_params(("arbitrary",)),
        name="convffn_sample",
    )(h_tm, g_ffn, state_tm, state_tm, w_up, w_up, w_conv, w_conv, b_conv, b_conv, w_down, g_final)


def _pack_weights(g_attn_norm, w_in, w_gate_up, b_gate, g_gla_norm, w_out, g_ffn_norm, w_up, w_conv, b_conv,
                  w_down):
    z0 = 2 * GLA_QK_W + 2 * GLA_V_W
    z1 = z0 + GLA_GATE_RANK
    w_main = jnp.concatenate([w_in[:, :z0], w_in[:, z1:]], axis=1).astype(BF16)
    w_z = jnp.pad(w_in[:, z0:z1], ((0, 0), (0, LANE - GLA_GATE_RANK))).astype(BF16)
    w_gate = jnp.pad(w_gate_up, ((0, LANE - GLA_GATE_RANK), (0, 0))).astype(BF16)
    return dict(
        g_attn=g_attn_norm[None, :], w_main=w_main, w_z=w_z, w_gate=w_gate, b_gate=b_gate[None, :],
        g_gla=g_gla_norm[None, :], w_out=w_out.astype(BF16), g_ffn=g_ffn_norm[None, :],
        w_up=w_up.astype(BF16), w_conv=w_conv, b_conv=b_conv[None, :], w_down=w_down.astype(BF16))


def _prompt_layer(x, p, g_final):
    b, t, d = x.shape
    m = b * t
    f = p["w_down"].shape[0]
    keep = min(SWA_MAX_WINDOW, t)
    x2 = x.reshape(m, d)
    proj, loga, k_buf, v_buf = _inproj(x2, p["g_attn"], p["w_main"], p["w_z"], p["w_gate"], p["b_gate"],
                                       tm=1024, seq=t, keep=keep)
    proj3 = proj.reshape(b, t, PROJ_W)
    o_a, s_fin = _gla(proj3, loga.reshape(b, t, GLA_QK_W), p["g_gla"], None, chunk=GLA_CHUNK, tc=512)
    branches = [_swa_prompt_branch(proj3, dil, lc=1024) for _, dil in SWA_PATTERNS]
    h = _outproj(x2, o_a.reshape(m, GLA_V_W), [o for o, _ in branches], [l for _, l in branches], p["w_out"],
                 tm=512)
    tm = 512
    y, ua, uv = _ffn_prompt(h, p["g_ffn"], p["w_up"], p["w_conv"], p["b_conv"], p["w_down"], g_final,
                            seq=t, tm=tm, tf=512)
    shape = (b, keep, SWA_HEADS, SWA_HD)
    last = jnp.concatenate([ua, uv], axis=-1).reshape(b, t // tm, 8, 2 * f)[:, -1, 8 - (CONV_W - 1):, :]
    return y.reshape(b, t, d), s_fin, k_buf.reshape(shape), v_buf.reshape(shape), last


def _sample_layer(x, s0, conv_s0, cache_k, cache_v, p, g_final):
    b, tn, d = x.shape
    m = b * tn
    tp = 8
    w = cache_k.shape[1]
    x2 = x.reshape(m, d)
    proj, loga, k_new, v_new = _inproj(x2, p["g_attn"], p["w_main"], p["w_z"], p["w_gate"], p["b_gate"],
                                       tm=m, seq=m, keep=m)
    pad = ((0, 0), (0, tp - tn), (0, 0))
    projp = jnp.pad(proj.reshape(b, tn, PROJ_W), pad)
    logap = jnp.pad(loga.reshape(b, tn, GLA_QK_W), pad)
    o_a, s_fin = _gla(projp, logap, p["g_gla"], s0, chunk=tp, tc=tp)
    rows = (b, tn * SWA_HEADS, SWA_HD)
    o_b, k_all, v_all = _swa_sample(projp, k_new.reshape(rows), v_new.reshape(rows),
                                    cache_k.reshape(b, w * SWA_HEADS, SWA_HD),
                                    cache_v.reshape(b, w * SWA_HEADS, SWA_HD), tn=tn)
    h = _outproj(x2, o_a[:, :tn].reshape(m, GLA_V_W), [o_b[:, :tn].reshape(m, SWA_W)], None, p["w_out"], tm=m)
    h_tm = h.reshape(b, tn, d).transpose(1, 0, 2).reshape(m, d)
    state_tm = conv_s0.transpose(1, 0, 2).reshape((CONV_W - 1) * b, -1)
    y_tm, na, nv = _ffn_sample(h_tm, state_tm, p["g_ffn"], p["w_up"], p["w_conv"], p["b_conv"], p["w_down"],
                               g_final, nb=b, tf=512)
    y = y_tm.reshape(tn, b, d).transpose(1, 0, 2)
    conv_new = jnp.concatenate([na, nv], axis=-1).reshape(CONV_W - 1, b, -1).transpose(1, 0, 2)
    shape = (b, w, SWA_HEADS, SWA_HD)
    return y, s_fin, k_all.reshape(shape), v_all.reshape(shape), conv_new


def kernel(x_prompt, x_sample, state_gla, cache_swa_k, cache_swa_v, state_ffn_conv, g_attn_norm, w_in, w_gate_up,
           b_gate, g_gla_norm, w_out, g_ffn_norm, w_up, w_conv, b_conv, w_down, g_final):
    depth = w_in.shape[0]
    assert depth == 1, "the final RMSNorm is fused into the last layer's ConvFFN kernel"
    assert cache_swa_k.shape[2] == SWA_MAX_WINDOW and x_sample.shape[1] <= 8
    p = _pack_weights(g_attn_norm[0], w_in[0], w_gate_up[0], b_gate[0], g_gla_norm[0], w_out[0], g_ffn_norm[0],
                      w_up[0], w_conv[0], b_conv[0], w_down[0])
    gf = g_final[None, :]
    yp, s1, k1, v1, c1 = _prompt_layer(x_prompt, p, gf)
    ys, s2, k2, v2, c2 = _sample_layer(x_sample, state_gla[0], state_ffn_conv[0], cache_swa_k[0], cache_swa_v[0],
                                       p, gf)
    return (yp, ys, s1[None], s2[None], k1[None], k2[None], v1[None], v2[None], c1[None], c2[None])
```

```python
import functools

import jax
import jax.numpy as jnp
from jax import lax
from jax.experimental import pallas as pl
from jax.experimental.pallas import tpu as pltpu

F32 = jnp.float32
BF16 = jnp.bfloat16

GLA_HEADS = 4
GLA_DK = 128
GLA_DV = 256
GLA_GATE_RANK = 16
GLA_TAU = 16.0
GLA_CHUNK = 64
SWA_HEADS = 8
SWA_HD = 128
SWA_PATTERNS = ((128, 1), (512, 4), (2048, 16))
SWA_BLOCK = 128
SWA_MAX_WINDOW = 2048
CONV_W = 3
EPS = 1e-6
NEG = -1e30

GLA_QK_W = GLA_HEADS * GLA_DK
GLA_V_W = GLA_HEADS * GLA_DV
SWA_W = SWA_HEADS * SWA_HD
PROJ_W = 2 * GLA_QK_W + 2 * GLA_V_W + 3 * SWA_W
LANE = 128
VMEM_LIMIT = 56 * 1024 * 1024

ALIBI_SLOPES = tuple(2.0 ** (-8.0 * (h + 1) / SWA_HEADS) for h in range(SWA_HEADS))


def _params(semantics):
    return pltpu.CompilerParams(dimension_semantics=semantics, vmem_limit_bytes=VMEM_LIMIT)


def _rms_rows(x, g):
    return x * lax.rsqrt(jnp.mean(x * x, axis=-1, keepdims=True) + EPS) * g


def _silu(x):
    return x / (1.0 + jnp.exp(-x))


def _log_sigmoid(x):
    return jnp.minimum(x, 0.0) - jnp.log1p(jnp.exp(-jnp.abs(x)))


def _dot(a, b):
    return jnp.dot(a, b, preferred_element_type=F32)


def _dot_nt(a, b):
    return lax.dot_general(a, b, (((1,), (1,)), ((), ())), preferred_element_type=F32)


def _dot_tn(a, b):
    return lax.dot_general(a, b, (((0,), (0,)), ((), ())), preferred_element_type=F32)


GLA_COLS = 2 * GLA_QK_W + 2 * GLA_V_W
GLA_STEPS = GLA_COLS // SWA_W


def _inproj_kernel(x_hbm, g_ref, w_ref, wz_ref, wg_ref, bg_ref, proj_ref, loga_ref, qkv_ref, kc_ref, vc_ref,
                   x_buf, xn_ref, sem):
    i, j = pl.program_id(0), pl.program_id(1)
    tm = x_buf.shape[0]

    def x_copy(tile):
        return pltpu.make_async_copy(x_hbm.at[pl.ds(tile * tm, tm)], x_buf, sem.at[0])

    @pl.when(j == 0)
    def _():
        @pl.when(i == 0)
        def _():
            x_copy(0).start()

        x_copy(i).wait()
        xn_ref[...] = _rms_rows(x_buf[...], g_ref[...]).astype(BF16)

        @pl.when(i + 1 < pl.num_programs(0))
        def _():
            x_copy(i + 1).start()

        za = _dot(xn_ref[...], wz_ref[...])
        gate = _dot(za.astype(BF16), wg_ref[...]) + bg_ref[...]
        loga_ref[...] = _log_sigmoid(gate) / GLA_TAU

    res = _dot(xn_ref[...], w_ref[...])

    @pl.when(j < GLA_STEPS)
    def _():
        proj_ref[...] = res.astype(BF16)

    @pl.when(j >= GLA_STEPS)
    def _():
        for h in range(SWA_HEADS):
            qkv_ref[0, h] = res[:, h * SWA_HD:(h + 1) * SWA_HD]

    def to_cache(dst_ref):
        for h in range(SWA_HEADS):
            dst_ref[pl.ds(h, tm, stride=SWA_HEADS), :] = res[:, h * SWA_HD:(h + 1) * SWA_HD]

    @pl.when(j == GLA_STEPS + 1)
    def _():
        to_cache(kc_ref)

    @pl.when(j == GLA_STEPS + 2)
    def _():
        to_cache(vc_ref)


def _inproj(x, g, w_main, w_z, w_gate, b_gate, *, tm, seq, keep):
    m, d = x.shape
    tn = SWA_W
    tps, kt = seq // tm, keep // tm
    assert seq % tm == 0 and keep % tm == 0 and m % seq == 0

    def cache_block(i, j):
        return ((i // tps) * kt + jnp.maximum(i % tps - (tps - kt), 0), 0)

    cache_rows = (m // seq) * keep * SWA_HEADS
    return pl.pallas_call(
        _inproj_kernel,
        grid=(m // tm, PROJ_W // tn),
        in_specs=[
            pl.BlockSpec(memory_space=pl.ANY),
            pl.BlockSpec((1, d), lambda i, j: (0, 0)),
            pl.BlockSpec((d, tn), lambda i, j: (0, j)),
            pl.BlockSpec((d, LANE), lambda i, j: (0, 0)),
            pl.BlockSpec((LANE, GLA_QK_W), lambda i, j: (0, 0)),
            pl.BlockSpec((1, GLA_QK_W), lambda i, j: (0, 0)),
        ],
        out_specs=[
            pl.BlockSpec((tm, tn), lambda i, j: (i, jnp.minimum(j, GLA_STEPS - 1))),
            pl.BlockSpec((tm, GLA_QK_W), lambda i, j: (i, 0)),
            pl.BlockSpec((1, SWA_HEADS, tm, SWA_HD), lambda i, j: (jnp.maximum(j - GLA_STEPS, 0), 0, i, 0)),
            pl.BlockSpec((tm * SWA_HEADS, SWA_HD), cache_block, pipeline_mode=pl.Buffered(1)),
            pl.BlockSpec((tm * SWA_HEADS, SWA_HD), cache_block, pipeline_mode=pl.Buffered(1)),
        ],
        out_shape=[jax.ShapeDtypeStruct((m, GLA_COLS), BF16),
                   jax.ShapeDtypeStruct((m, GLA_QK_W), F32),
                   jax.ShapeDtypeStruct((3, SWA_HEADS, m, SWA_HD), F32),
                   jax.ShapeDtypeStruct((cache_rows, SWA_HD), F32),
                   jax.ShapeDtypeStruct((cache_rows, SWA_HD), F32)],
        scratch_shapes=[pltpu.VMEM((tm, d), F32), pltpu.VMEM((tm, d), BF16), pltpu.SemaphoreType.DMA((1,))],
        compiler_params=_params(("arbitrary", "arbitrary")),
        name="norm_inproj",
    )(x, g, w_main, w_z, w_gate, b_gate)


def _split3(x):
    hi = x.astype(BF16)
    r1 = x - hi.astype(F32)
    mid = r1.astype(BF16)
    lo = (r1 - mid.astype(F32)).astype(BF16)
    return hi, mid, lo


def _gla_kernel(*refs, chunk, nsub, has_s0):
    if has_s0:
        q_ref, k_ref, v_ref, ga_ref, la_ref, gn_ref, s0_ref, o_ref, sfin_ref, s_scr = refs
    else:
        q_ref, k_ref, v_ref, ga_ref, la_ref, gn_ref, o_ref, sfin_ref, s_scr = refs
    c = pl.program_id(1)

    @pl.when(c == 0)
    def _():
        if has_s0:
            s_scr[...] = s0_ref[0]
        else:
            s_scr[...] = jnp.zeros_like(s_scr)

    row = lax.broadcasted_iota(jnp.int32, (chunk, chunk), 0)
    col = lax.broadcasted_iota(jnp.int32, (chunk, chunk), 1)
    causal = row >= col
    tril = jnp.where(causal, 1.0, 0.0).astype(BF16)
    eye = (lax.broadcasted_iota(jnp.int32, (GLA_DK, GLA_DK), 0)
           == lax.broadcasted_iota(jnp.int32, (GLA_DK, GLA_DK), 1))

    def sub_chunk(i, carry):
        sl = pl.ds(pl.multiple_of(i * chunk, chunk), chunk)
        for h in range(GLA_HEADS):
            ks = slice(h * GLA_DK, (h + 1) * GLA_DK)
            vs = slice(h * GLA_DV, (h + 1) * GLA_DV)
            q = q_ref[0, sl, ks].astype(F32) * (GLA_DK ** -0.5)
            k = k_ref[0, sl, ks].astype(F32)
            v = v_ref[0, sl, vs]
            g_hi, g_mid, g_lo = _split3(la_ref[0, sl, ks])
            cum = _dot(tril, g_hi) + _dot(tril, g_mid) + _dot(tril, g_lo)
            total = cum[chunk - 1:chunk, :]
            q_dec = (q * jnp.exp(cum)).astype(BF16)
            k_dec = (k * jnp.exp(-cum)).astype(BF16)
            k_end = (k * jnp.exp(total - cum)).astype(BF16)
            attn = jnp.where(causal, _dot_nt(q_dec, k_dec), 0.0).astype(BF16)
            s = s_scr[h]
            o = _dot(attn, v) + _dot(q_dec, s.astype(BF16))
            decay = jnp.sum(jnp.where(eye, jnp.exp(total), 0.0), axis=1, keepdims=True)
            s_scr[h] = s * decay + _dot_tn(k_end, v)
            on = _rms_rows(o, gn_ref[:, vs])
            o_ref[0, sl, vs] = (on * _silu(ga_ref[0, sl, vs].astype(F32))).astype(o_ref.dtype)
        return carry

    if nsub == 1:
        sub_chunk(0, 0)
    else:
        lax.fori_loop(0, nsub, sub_chunk, 0)

    @pl.when(c == pl.num_programs(1) - 1)
    def _():
        sfin_ref[0] = s_scr[...]


def _gla(proj3, loga3, g_norm, s0, *, chunk, tc):
    b, t, _ = proj3.shape
    nsub = tc // chunk
    state = pl.BlockSpec((1, GLA_HEADS, GLA_DK, GLA_DV), lambda bi, c: (bi, 0, 0, 0))
    in_specs = [
        pl.BlockSpec((1, tc, GLA_QK_W), lambda bi, c: (bi, c, 0)),
        pl.BlockSpec((1, tc, GLA_QK_W), lambda bi, c: (bi, c, 1)),
        pl.BlockSpec((1, tc, GLA_V_W), lambda bi, c: (bi, c, 1)),
        pl.BlockSpec((1, tc, GLA_V_W), lambda bi, c: (bi, c, 2)),
        pl.BlockSpec((1, tc, GLA_QK_W), lambda bi, c: (bi, c, 0)),
        pl.BlockSpec((1, GLA_V_W), lambda bi, c: (0, 0)),
    ]
    args = [proj3, proj3, proj3, proj3, loga3, g_norm]
    if s0 is not None:
        in_specs.append(state)
        args.append(s0)
    return pl.pallas_call(
        functools.partial(_gla_kernel, chunk=chunk, nsub=nsub, has_s0=s0 is not None),
        grid=(b, t // tc),
        in_specs=in_specs,
        out_specs=[pl.BlockSpec((1, tc, GLA_V_W), lambda bi, c: (bi, c, 0)), state],
        out_shape=[jax.ShapeDtypeStruct((b, t, GLA_V_W), BF16),
                   jax.ShapeDtypeStruct((b, GLA_HEADS, GLA_DK, GLA_DV), F32)],
        scratch_shapes=[pltpu.VMEM((GLA_HEADS, GLA_DK, GLA_DV), F32)],
        compiler_params=_params(("parallel", "arbitrary")),
        name="gla",
    )(*args)


SWA_SUPER = 2048
SWA_TILES = 8
QUAD = 4


def _swa_prompt_kernel(q_ref, k_ref, v_ref, o_ref, x4, q1, q4, q16, k1, k4, k16, v1, v4, v16, ob, lb, bias):
    n = SWA_BLOCK
    s_tok = q_ref.shape[0]
    l4, l16 = s_tok // QUAD, s_tok // (QUAD * QUAD)
    head, has_prev = pl.program_id(1), pl.program_id(2) > 0
    scale = SWA_HD ** -0.5

    slope = ALIBI_SLOPES[0]
    for h in range(1, SWA_HEADS):
        slope = jnp.where(head == h, ALIBI_SLOPES[h], slope)
    qi = lax.broadcasted_iota(jnp.int32, (n, 2 * n), 0)
    kj = lax.broadcasted_iota(jnp.int32, (n, 2 * n), 1)
    steps = qi + n - kj
    visible = (steps >= 0) & (steps <= n)
    first_visible = visible & (has_prev | (kj >= n))
    for g, (_, dil) in enumerate(SWA_PATTERNS):
        alibi = (-float(dil) * slope) * steps.astype(F32)
        bias[2 * g] = jnp.where(visible, alibi, NEG)
        bias[2 * g + 1] = jnp.where(first_visible, alibi, NEG)

    for buf, length in ((k1, s_tok), (v1, s_tok), (k4, l4), (v4, l4), (k16, l16), (v16, l16)):
        lead = buf.shape[:-2]
        idx = (slice(None),) * len(lead)

        @pl.when(has_prev)
        def _():
            buf[idx + (slice(0, n),)] = buf[idx + (slice(length, length + n),)]

        @pl.when(jnp.logical_not(has_prev))
        def _():
            buf[idx + (slice(0, n),)] = jnp.zeros(lead + (n, SWA_HD), BF16)

    for t, src in enumerate((q_ref, k_ref, v_ref)):
        for r in range(QUAD):
            x4[t, r] = src[pl.ds(r, l4, stride=QUAD), :]
    q1[...] = q_ref[...].astype(BF16)
    k1[n:, :] = k_ref[...].astype(BF16)
    v1[n:, :] = v_ref[...].astype(BF16)
    for r in range(QUAD):
        q4[r] = x4[0, r].astype(BF16)
        k4[r, n:, :] = x4[1, r].astype(BF16)
        v4[r, n:, :] = x4[2, r].astype(BF16)
    for r in range(QUAD * QUAD):
        sub = pl.ds(r // QUAD, l16, stride=QUAD)
        q16[r] = x4[0, r % QUAD, sub, :].astype(BF16)
        k16[r, n:, :] = x4[1, r % QUAD, sub, :].astype(BF16)
        v16[r, n:, :] = x4[2, r % QUAD, sub, :].astype(BF16)

    def tile(q, keys, values, table):
        s = _dot_nt(q, keys) * scale + table
        m = jnp.max(s, axis=1, keepdims=True)
        p = jnp.exp(s - m)
        l = jnp.sum(p, axis=1, keepdims=True)
        acc = _dot(p.astype(BF16), values)
        return acc * (1.0 / l), jnp.broadcast_to(m + jnp.log(l), (n, SWA_HD))

    def d1_body(c, carry):
        for a in range(SWA_TILES):
            i = c * SWA_TILES + a
            row0 = pl.multiple_of(i * n, n)
            o, lse = tile(q1[pl.ds(row0, n), :], k1[pl.ds(row0, 2 * n), :], v1[pl.ds(row0, 2 * n), :],
                          bias[jnp.where(i == 0, 1, 0)])
            ob[0, pl.ds(row0, n), :] = o
            lb[0, pl.ds(row0, n), :] = lse
        return carry

    lax.fori_loop(0, s_tok // n // SWA_TILES, d1_body, 0)

    per4 = l4 // n

    def d4_body(c, carry):
        for a in range(SWA_TILES // per4):
            r = c * (SWA_TILES // per4) + a
            for i in range(per4):
                o, lse = tile(q4[r, i * n:(i + 1) * n, :], k4[r, i * n:(i + 2) * n, :], v4[r, i * n:(i + 2) * n, :],
                              bias[2 + (1 if i == 0 else 0)])
                row0 = pl.multiple_of(r * l4 + i * n, n)
                ob[1, pl.ds(row0, n), :] = o
                lb[1, pl.ds(row0, n), :] = lse
        return carry

    lax.fori_loop(0, QUAD * per4 // SWA_TILES, d4_body, 0)

    per16 = l16 // n

    def d16_body(c, carry):
        for a in range(SWA_TILES // per16):
            r4 = a % QUAD
            j = c * (SWA_TILES // per16 // QUAD) + a // QUAD
            r = r4 + QUAD * j
            for i in range(per16):
                o, lse = tile(q16[r, i * n:(i + 1) * n, :], k16[r, i * n:(i + 2) * n, :],
                              v16[r, i * n:(i + 2) * n, :], bias[4 + (1 if i == 0 else 0)])
                rows = pl.ds(r4 * l4 + QUAD * i * n + j, n, stride=QUAD)
                ob[2, rows, :] = o
                lb[2, rows, :] = lse
        return carry

    lax.fori_loop(0, QUAD * QUAD * per16 // SWA_TILES, d16_body, 0)

    def combine_body(c, carry):
        for r in range(QUAD):
            tok = pl.ds(r + QUAD * c * n, n, stride=QUAD)
            cls = pl.ds(pl.multiple_of(r * l4 + c * n, n), n)
            lses = [lb[0, tok, :], lb[1, cls, :], lb[2, cls, :]]
            outs = [ob[0, tok, :], ob[1, cls, :], ob[2, cls, :]]
            m = jnp.maximum(jnp.maximum(lses[0], lses[1]), lses[2])
            es = [jnp.exp(x - m) for x in lses]
            num = es[0] * outs[0] + es[1] * outs[1] + es[2] * outs[2]
            ob[0, tok, :] = num * (1.0 / (es[0] + es[1] + es[2]))
        return carry

    lax.fori_loop(0, l4 // n, combine_body, 0)
    o_ref[...] = ob[0].astype(o_ref.dtype)


def _swa_prompt(qkv, b, t):
    n, s_tok = SWA_BLOCK, SWA_SUPER
    assert t % s_tok == 0 and [d for _, d in SWA_PATTERNS] == [1, QUAD, QUAD * QUAD]
    assert all(w // d == n for w, d in SWA_PATTERNS)
    l4, l16 = s_tok // QUAD, s_tok // (QUAD * QUAD)
    assert (s_tok // n) % SWA_TILES == 0 and SWA_TILES % (l4 // n) == 0 and SWA_TILES % (QUAD * l16 // n) == 0
    view = qkv.reshape(3, SWA_HEADS, b, t, SWA_HD)

    def part(idx):
        return pl.BlockSpec((None, None, None, s_tok, SWA_HD), lambda bi, h, sb: (idx, h, bi, sb, 0))

    def rows(*lead):
        return pltpu.VMEM(lead + (SWA_HD,), BF16)

    o = pl.pallas_call(
        _swa_prompt_kernel,
        grid=(b, SWA_HEADS, t // s_tok),
        in_specs=[part(0), part(1), part(2)],
        out_specs=pl.BlockSpec((None, s_tok, SWA_HD), lambda bi, h, sb: (bi, sb, h)),
        out_shape=jax.ShapeDtypeStruct((b, t, SWA_W), BF16),
        scratch_shapes=[
            pltpu.VMEM((3, QUAD, l4, SWA_HD), F32),
            rows(s_tok), rows(QUAD, l4), rows(QUAD * QUAD, l16),
            rows(n + s_tok), rows(QUAD, n + l4), rows(QUAD * QUAD, n + l16),
            rows(n + s_tok), rows(QUAD, n + l4), rows(QUAD * QUAD, n + l16),
            pltpu.VMEM((3, s_tok, SWA_HD), F32), pltpu.VMEM((3, s_tok, SWA_HD), F32),
            pltpu.VMEM((2 * len(SWA_PATTERNS), n, 2 * n), F32),
        ],
        compiler_params=_params(("parallel", "parallel", "arbitrary")),
        name="swa_prompt",
    )(view, view, view)
    return o.reshape(b * t, SWA_W)


def _branch_multiplicity(d):
    mult = jnp.zeros(d.shape, F32)
    for window, dil in SWA_PATTERNS:
        hit = (d >= 0) & (d <= window) & ((d & (dil - 1)) == 0)
        mult = mult + jnp.where(hit, 1.0, 0.0)
    return mult


def _swa_sample_kernel(q_ref, kn_ref, vn_ref, knf_ref, vnf_ref, kc_ref, vc_ref, o_ref, ko_hbm, vo_hbm, sems,
                       *, w, tn):
    bi = pl.program_id(0)
    tp = q_ref.shape[1]
    old, new = (w - tn) * SWA_HEADS, tn * SWA_HEADS
    copies = []
    for idx, (cache, fresh, out) in enumerate(((kc_ref, knf_ref, ko_hbm), (vc_ref, vnf_ref, vo_hbm))):
        copies.append(pltpu.make_async_copy(cache.at[0, pl.ds(new, old)], out.at[bi, pl.ds(0, old)],
                                            sems.at[2 * idx]))
        copies.append(pltpu.make_async_copy(fresh.at[0], out.at[bi, pl.ds(old, new)], sems.at[2 * idx + 1]))
    for cp in copies:
        cp.start()

    d_c = (w + lax.broadcasted_iota(jnp.int32, (tp, w), 0)
           - lax.broadcasted_iota(jnp.int32, (tp, w), 1))
    mult_c = _branch_multiplicity(d_c)
    i_n = lax.broadcasted_iota(jnp.int32, (tp, tp), 0)
    j_n = lax.broadcasted_iota(jnp.int32, (tp, tp), 1)
    d_n = jnp.where(j_n < tn, i_n - j_n, -1)
    mult_n = _branch_multiplicity(d_n)
    dist_c = d_c.astype(F32)
    dist_n = d_n.astype(F32)
    scale = SWA_HD ** -0.5

    for h in range(SWA_HEADS):
        hs = slice(h * SWA_HD, (h + 1) * SWA_HD)
        head_rows = pl.ds(h, w, stride=SWA_HEADS)
        slope = ALIBI_SLOPES[h]
        q = q_ref[0, :, hs]
        s_c = _dot_nt(q, kc_ref[0, head_rows, :].astype(BF16)) * scale - slope * dist_c
        s_n = _dot_nt(q, kn_ref[0, :, hs]) * scale - slope * dist_n
        s_c = jnp.where(mult_c > 0.0, s_c, NEG)
        s_n = jnp.where(mult_n > 0.0, s_n, NEG)
        m = jnp.maximum(jnp.max(s_c, axis=1, keepdims=True), jnp.max(s_n, axis=1, keepdims=True))
        p_c = mult_c * jnp.exp(s_c - m)
        p_n = mult_n * jnp.exp(s_n - m)
        l = jnp.sum(p_c, axis=1, keepdims=True) + jnp.sum(p_n, axis=1, keepdims=True)
        acc = (_dot(p_c.astype(BF16), vc_ref[0, head_rows, :].astype(BF16))
               + _dot(p_n.astype(BF16), vn_ref[0, :, hs]))
        o_ref[0, :, hs] = (acc / l).astype(o_ref.dtype)

    for cp in copies:
        cp.wait()


def _swa_sample(projp, k_new, v_new, cache_k, cache_v, *, tn):
    b, tp, _ = projp.shape
    rows = cache_k.shape[1]
    w = rows // SWA_HEADS
    nblk = 3

    def new(cb):
        return pl.BlockSpec((1, tp, SWA_W), lambda bi: (bi, 0, cb))

    fresh = pl.BlockSpec((1, tn * SWA_HEADS, SWA_HD), lambda bi: (bi, 0, 0))
    cache = pl.BlockSpec((1, rows, SWA_HD), lambda bi: (bi, 0, 0))
    hbm = pl.BlockSpec(memory_space=pl.ANY)
    return pl.pallas_call(
        functools.partial(_swa_sample_kernel, w=w, tn=tn),
        grid=(b,),
        in_specs=[new(nblk - 3), new(nblk - 2), new(nblk - 1), fresh, fresh, cache, cache],
        out_specs=[pl.BlockSpec((1, tp, SWA_W), lambda bi: (bi, 0, 0)), hbm, hbm],
        out_shape=[jax.ShapeDtypeStruct((b, tp, SWA_W), BF16),
                   jax.ShapeDtypeStruct((b, rows, SWA_HD), F32),
                   jax.ShapeDtypeStruct((b, rows, SWA_HD), F32)],
        scratch_shapes=[pltpu.SemaphoreType.DMA((4,))],
        compiler_params=_params(("arbitrary",)),
        name="swa_sample_cache",
    )(projp, projp, projp, k_new, v_new, cache_k, cache_v)


def _outproj_kernel(x_ref, oa_ref, ob_ref, wa_ref, wb_ref, h_ref):
    h_ref[...] = x_ref[...] + _dot(oa_ref[...], wa_ref[...]) + _dot(ob_ref[...], wb_ref[...])


def _outproj(x, o_a, o_b, w_out, *, tm):
    m, d = x.shape
    return pl.pallas_call(
        _outproj_kernel,
        grid=(m // tm,),
        in_specs=[pl.BlockSpec((tm, d), lambda i: (i, 0)),
                  pl.BlockSpec((tm, GLA_V_W), lambda i: (i, 0)),
                  pl.BlockSpec((tm, SWA_W), lambda i: (i, 0)),
                  pl.BlockSpec((GLA_V_W, d), lambda i: (0, 0)),
                  pl.BlockSpec((SWA_W, d), lambda i: (GLA_V_W // SWA_W, 0))],
        out_specs=pl.BlockSpec((tm, d), lambda i: (i, 0)),
        out_shape=jax.ShapeDtypeStruct((m, d), F32),
        compiler_params=_params(("parallel",)),
        name="outproj",
    )(x, o_a, o_b, w_out, w_out)


FFN_HALO = 16


def _ffn_prompt_kernel(h_ref, halo_ref, g_ref, wa_ref, wv_ref, ca_ref, cv_ref, ba_ref, bv_ref, wd_ref,
                       gf_ref, y_ref, ua_ref, uv_ref, xn_scr, ea_scr, ev_scr, acc_scr, *, tiles_per_seq):
    i, j = pl.program_id(0), pl.program_id(1)
    tm = h_ref.shape[0]

    @pl.when(j == 0)
    def _():
        first = (i % tiles_per_seq) == 0
        halo = _rms_rows(halo_ref[...], g_ref[...])
        xn_scr[0:FFN_HALO, :] = jnp.where(first, 0.0, halo).astype(BF16)
        xn_scr[FFN_HALO:, :] = _rms_rows(h_ref[...], g_ref[...]).astype(BF16)
        acc_scr[...] = jnp.zeros_like(acc_scr)

    xn = xn_scr[...]
    ea_scr[...] = _dot(xn, wa_ref[...])
    ev_scr[...] = _dot(xn, wv_ref[...])

    def conv(e_scr, c_ref, b_ref):
        out = b_ref[...]
        for tap in range(CONV_W):
            off = FFN_HALO - (CONV_W - 1) + tap
            out = out + c_ref[tap:tap + 1, :] * e_scr[off:off + tm, :]
        return out

    gate = (_silu(conv(ea_scr, ca_ref, ba_ref)) * conv(ev_scr, cv_ref, bv_ref)).astype(BF16)
    acc_scr[...] += _dot(gate, wd_ref[...])
    ua_ref[0] = ea_scr[FFN_HALO + tm - 8:FFN_HALO + tm, :]
    uv_ref[0] = ev_scr[FFN_HALO + tm - 8:FFN_HALO + tm, :]

    @pl.when(j == pl.num_programs(1) - 1)
    def _():
        y_ref[...] = _rms_rows(h_ref[...] + acc_scr[...], gf_ref[...])


def _ffn_prompt(h, g_ffn, w_up, w_conv, b_conv, w_down, g_final, *, seq, tm, tf):
    m, d = h.shape
    f = w_down.shape[0]
    nf = f // tf
    ntile = m // tm
    hb = tm // FFN_HALO
    y, ua, uv = pl.pallas_call(
        functools.partial(_ffn_prompt_kernel, tiles_per_seq=seq // tm),
        grid=(ntile, nf),
        in_specs=[
            pl.BlockSpec((tm, d), lambda i, j: (i, 0)),
            pl.BlockSpec((FFN_HALO, d), lambda i, j: (jnp.maximum(i * hb - 1, 0), 0)),
            pl.BlockSpec((1, d), lambda i, j: (0, 0)),
            pl.BlockSpec((d, tf), lambda i, j: (0, j)),
            pl.BlockSpec((d, tf), lambda i, j: (0, nf + j)),
            pl.BlockSpec((CONV_W, tf), lambda i, j: (0, j)),
            pl.BlockSpec((CONV_W, tf), lambda i, j: (0, nf + j)),
            pl.BlockSpec((1, tf), lambda i, j: (0, j)),
            pl.BlockSpec((1, tf), lambda i, j: (0, nf + j)),
            pl.BlockSpec((tf, d), lambda i, j: (j, 0)),
            pl.BlockSpec((1, d), lambda i, j: (0, 0)),
        ],
        out_specs=[
            pl.BlockSpec((tm, d), lambda i, j: (i, 0)),
            pl.BlockSpec((1, 8, tf), lambda i, j: (i, 0, j)),
            pl.BlockSpec((1, 8, tf), lambda i, j: (i, 0, j)),
        ],
        out_shape=[jax.ShapeDtypeStruct((m, d), F32),
                   jax.ShapeDtypeStruct((ntile, 8, f), F32),
                   jax.ShapeDtypeStruct((ntile, 8, f), F32)],
        scratch_shapes=[pltpu.VMEM((tm + FFN_HALO, d), BF16),
                        pltpu.VMEM((tm + FFN_HALO, tf), F32),
                        pltpu.VMEM((tm + FFN_HALO, tf), F32),
                        pltpu.VMEM((tm, d), F32)],
        compiler_params=_params(("parallel", "arbitrary")),
        name="convffn_prompt",
    )(h, h, g_ffn, w_up, w_up, w_conv, w_conv, b_conv, b_conv, w_down, g_final)
    return y, ua, uv


def _ffn_sample_kernel(h_ref, g_ref, sa_ref, sv_ref, wa_ref, wv_ref, ca_ref, cv_ref, ba_ref, bv_ref, wd_ref,
                       gf_ref, y_ref, na_ref, nv_ref, xn_scr, acc_scr, *, nb):
    j = pl.program_id(0)
    m = h_ref.shape[0]
    keep = (CONV_W - 1) * nb

    @pl.when(j == 0)
    def _():
        xn_scr[...] = _rms_rows(h_ref[...], g_ref[...]).astype(BF16)
        acc_scr[...] = jnp.zeros_like(acc_scr)

    xn = xn_scr[...]

    def conv(s_ref, w_ref, c_ref, b_ref, n_ref):
        u = _dot(xn, w_ref[...])
        ext = jnp.concatenate([s_ref[...], u], axis=0)
        n_ref[...] = ext[m:m + keep, :]
        out = b_ref[...]
        for tap in range(CONV_W):
            out = out + c_ref[tap:tap + 1, :] * ext[tap * nb:tap * nb + m, :]
        return out

    gate = (_silu(conv(sa_ref, wa_ref, ca_ref, ba_ref, na_ref))
            * conv(sv_ref, wv_ref, cv_ref, bv_ref, nv_ref)).astype(BF16)
    acc_scr[...] += _dot(gate, wd_ref[...])

    @pl.when(j == pl.num_programs(0) - 1)
    def _():
        y_ref[...] = _rms_rows(h_ref[...] + acc_scr[...], gf_ref[...])


def _ffn_sample(h_tm, state_tm, g_ffn, w_up, w_conv, b_conv, w_down, g_final, *, nb, tf):
    m, d = h_tm.shape
    f = w_down.shape[0]
    nf = f // tf
    keep = (CONV_W - 1) * nb
    return pl.pallas_call(
        functools.partial(_ffn_sample_kernel, nb=nb),
        grid=(nf,),
        in_specs=[
            pl.BlockSpec((m, d), lambda j: (0, 0)),
            pl.BlockSpec((1, d), lambda j: (0, 0)),
            pl.BlockSpec((keep, tf), lambda j: (0, j)),
            pl.BlockSpec((keep, tf), lambda j: (0, nf + j)),
            pl.BlockSpec((d, tf), lambda j: (0, j)),
            pl.BlockSpec((d, tf), lambda j: (0, nf + j)),
            pl.BlockSpec((CONV_W, tf), lambda j: (0, j)),
            pl.BlockSpec((CONV_W, tf), lambda j: (0, nf + j)),
            pl.BlockSpec((1, tf), lambda j: (0, j)),
            pl.BlockSpec((1, tf), lambda j: (0, nf + j)),
            pl.BlockSpec((tf, d), lambda j: (j, 0)),
            pl.BlockSpec((1, d), lambda j: (0, 0)),
        ],
        out_specs=[
            pl.BlockSpec((m, d), lambda j: (0, 0)),
            pl.BlockSpec((keep, tf), lambda j: (0, j)),
            pl.BlockSpec((keep, tf), lambda j: (0, j)),
        ],
        out_shape=[jax.ShapeDtypeStruct((m, d), F32),
                   jax.ShapeDtypeStruct((keep, f), F32),
                   jax.ShapeDtypeStruct((keep, f), F32)],
        scratch_shapes=[pltpu.VMEM((m, d), BF16), pltpu.VMEM((m, d), F32)],
        compiler_params=_params(("arbitrary",)),
        name="convffn_sample",
    )(h_tm, g_ffn, state_tm, state_tm, w_up, w_up, w_conv, w_conv, b_conv, b_conv, w_down, g_final)


def _pack_weights(g_attn_norm, w_in, w_gate_up, b_gate, g_gla_norm, w_out, g_ffn_norm, w_up, w_conv, b_conv,
                  w_down):
    z0 = 2 * GLA_QK_W + 2 * GLA_V_W
    z1 = z0 + GLA_GATE_RANK
    w_main = jnp.concatenate([w_in[:, :z0], w_in[:, z1:]], axis=1).astype(BF16)
    w_z = jnp.pad(w_in[:, z0:z1], ((0, 0), (0, LANE - GLA_GATE_RANK))).astype(BF16)
    w_gate = jnp.pad(w_gate_up, ((0, LANE - GLA_GATE_RANK), (0, 0))).astype(BF16)
    return dict(
        g_attn=g_attn_norm[None, :], w_main=w_main, w_z=w_z, w_gate=w_gate, b_gate=b_gate[None, :],
        g_gla=g_gla_norm[None, :], w_out=w_out.astype(BF16), g_ffn=g_ffn_norm[None, :],
        w_up=w_up.astype(BF16), w_conv=w_conv, b_conv=b_conv[None, :], w_down=w_down.astype(BF16))


def _prompt_layer(x, p, g_final):
    b, t, d = x.shape
    m = b * t
    f = p["w_down"].shape[0]
    keep = min(SWA_MAX_WINDOW, t)
    x2 = x.reshape(m, d)
    proj, loga, qkv, k_buf, v_buf = _inproj(x2, p["g_attn"], p["w_main"], p["w_z"], p["w_gate"], p["b_gate"],
                                            tm=1024, seq=t, keep=keep)
    o_a, s_fin = _gla(proj.reshape(b, t, GLA_COLS), loga.reshape(b, t, GLA_QK_W), p["g_gla"], None,
                      chunk=GLA_CHUNK, tc=512)
    o_b = _swa_prompt(qkv, b, t)
    h = _outproj(x2, o_a.reshape(m, GLA_V_W), o_b, p["w_out"], tm=512)
    tm = 512
    y, ua, uv = _ffn_prompt(h, p["g_ffn"], p["w_up"], p["w_conv"], p["b_conv"], p["w_down"], g_final,
                            seq=t, tm=tm, tf=512)
    shape = (b, keep, SWA_HEADS, SWA_HD)
    last = jnp.concatenate([ua, uv], axis=-1).reshape(b, t // tm, 8, 2 * f)[:, -1, 8 - (CONV_W - 1):, :]
    return y.reshape(b, t, d), s_fin, k_buf.reshape(shape), v_buf.reshape(shape), last


def _sample_layer(x, s0, conv_s0, cache_k, cache_v, p, g_final):
    b, tn, d = x.shape
    m = b * tn
    tp = 8
    w = cache_k.shape[1]
    x2 = x.reshape(m, d)
    proj, loga, qkv, k_new, v_new = _inproj(x2, p["g_attn"], p["w_main"], p["w_z"], p["w_gate"], p["b_gate"],
                                            tm=m, seq=m, keep=m)
    pad = ((0, 0), (0, tp - tn), (0, 0))
    projp = jnp.pad(proj.reshape(b, tn, GLA_COLS), pad)
    logap = jnp.pad(loga.reshape(b, tn, GLA_QK_W), pad)
    o_a, s_fin = _gla(projp, logap, p["g_gla"], s0, chunk=tp, tc=tp)
    qkvp = jnp.pad(qkv.transpose(2, 0, 1, 3).reshape(b, tn, 3 * SWA_W).astype(BF16), pad)
    rows = (b, tn * SWA_HEADS, SWA_HD)
    o_b, k_all, v_all = _swa_sample(qkvp, k_new.reshape(rows), v_new.reshape(rows),
                                    cache_k.reshape(b, w * SWA_HEADS, SWA_HD),
                                    cache_v.reshape(b, w * SWA_HEADS, SWA_HD), tn=tn)
    h = _outproj(x2, o_a[:, :tn].reshape(m, GLA_V_W), o_b[:, :tn].reshape(m, SWA_W), p["w_out"], tm=m)
    h_tm = h.reshape(b, tn, d).transpose(1, 0, 2).reshape(m, d)
    state_tm = conv_s0.transpose(1, 0, 2).reshape((CONV_W - 1) * b, -1)
    y_tm, na, nv = _ffn_sample(h_tm, state_tm, p["g_ffn"], p["w_up"], p["w_conv"], p["b_conv"], p["w_down"],
                               g_final, nb=b, tf=512)
    y = y_tm.reshape(tn, b, d).transpose(1, 0, 2)
    conv_new = jnp.concatenate([na, nv], axis=-1).reshape(CONV_W - 1, b, -1).transpose(1, 0, 2)
    shape = (b, w, SWA_HEADS, SWA_HD)
    return y, s_fin, k_all.reshape(shape), v_all.reshape(shape), conv_new


def kernel(x_prompt, x_sample, state_gla, cache_swa_k, cache_swa_v, state_ffn_conv, g_attn_norm, w_in, w_gate_up,
           b_gate, g_gla_norm, w_out, g_ffn_norm, w_up, w_conv, b_conv, w_down, g_final):
    depth = w_in.shape[0]
    assert depth == 1, "the final RMSNorm is fused into the last layer's ConvFFN kernel"
    assert cache_swa_k.shape[2] == SWA_MAX_WINDOW and x_sample.shape[1] <= 8
    p = _pack_weights(g_attn_norm[0], w_in[0], w_gate_up[0], b_gate[0], g_gla_norm[0], w_out[0], g_ffn_norm[0],
                      w_up[0], w_conv[0], b_conv[0], w_down[0])
    gf = g_final[None, :]
    yp, s1, k1, v1, c1 = _prompt_layer(x_prompt, p, gf)
    ys, s2, k2, v2, c2 = _sample_layer(x_sample, state_gla[0], state_ffn_conv[0], cache_swa_k[0], cache_swa_v[0],
                                       p, gf)
    return (yp, ys, s1[None], s2[None], k1[None], k2[None], v1[None], v2[None], c1[None], c2[None])
```

```python
import functools

import jax
import jax.numpy as jnp
from jax import lax
from jax.experimental import pallas as pl
from jax.experimental.pallas import tpu as pltpu

F32 = jnp.float32
BF16 = jnp.bfloat16

GLA_HEADS = 4
GLA_DK = 128
GLA_DV = 256
GLA_GATE_RANK = 16
GLA_TAU = 16.0
GLA_CHUNK = 64
SWA_HEADS = 8
SWA_HD = 128
SWA_PATTERNS = ((128, 1), (512, 4), (2048, 16))
SWA_BLOCK = 128
SWA_MAX_WINDOW = 2048
CONV_W = 3
EPS = 1e-6
NEG = -1e30

GLA_QK_W = GLA_HEADS * GLA_DK
GLA_V_W = GLA_HEADS * GLA_DV
SWA_W = SWA_HEADS * SWA_HD
PROJ_W = 2 * GLA_QK_W + 2 * GLA_V_W + 3 * SWA_W
LANE = 128
VMEM_LIMIT = 56 * 1024 * 1024

ALIBI_SLOPES = tuple(2.0 ** (-8.0 * (h + 1) / SWA_HEADS) for h in range(SWA_HEADS))


def _params(semantics):
    return pltpu.CompilerParams(dimension_semantics=semantics, vmem_limit_bytes=VMEM_LIMIT)


def _rms_rows(x, g):
    return x * lax.rsqrt(jnp.mean(x * x, axis=-1, keepdims=True) + EPS) * g


def _silu(x):
    return x / (1.0 + jnp.exp(-x))


def _log_sigmoid(x):
    return jnp.minimum(x, 0.0) - jnp.log1p(jnp.exp(-jnp.abs(x)))


def _dot(a, b):
    return jnp.dot(a, b, preferred_element_type=F32)


def _dot_nt(a, b):
    return lax.dot_general(a, b, (((1,), (1,)), ((), ())), preferred_element_type=F32)


def _dot_tn(a, b):
    return lax.dot_general(a, b, (((0,), (0,)), ((), ())), preferred_element_type=F32)


GLA_COLS = 2 * GLA_QK_W + 2 * GLA_V_W
GLA_STEPS = GLA_COLS // SWA_W


def _inproj_kernel(x_hbm, g_ref, w_ref, wz_ref, wg_ref, bg_ref, proj_ref, loga_ref, qkv_ref, kc_ref, vc_ref,
                   x_buf, xn_ref, sem, *, tiles_per_seq, kept_tiles):
    i, j = pl.program_id(0), pl.program_id(1)
    tm = x_buf.shape[0]

    def x_copy(tile):
        return pltpu.make_async_copy(x_hbm.at[pl.ds(tile * tm, tm)], x_buf, sem.at[0])

    @pl.when(j == 0)
    def _():
        @pl.when(i == 0)
        def _():
            x_copy(0).start()

        x_copy(i).wait()
        xn_ref[...] = _rms_rows(x_buf[...], g_ref[...]).astype(BF16)

        @pl.when(i + 1 < pl.num_programs(0))
        def _():
            x_copy(i + 1).start()

        za = _dot(xn_ref[...], wz_ref[...])
        gate = _dot(za.astype(BF16), wg_ref[...]) + bg_ref[...]
        loga_ref[...] = _log_sigmoid(gate) / GLA_TAU

    @pl.when(j < GLA_STEPS)
    def _():
        proj_ref[...] = _dot(xn_ref[...], w_ref[...]).astype(BF16)

    def swa_step(cache_ref):
        res = _dot(xn_ref[...], w_ref[...])
        for h in range(SWA_HEADS):
            qkv_ref[0, h] = res[:, h * SWA_HD:(h + 1) * SWA_HD]
        if cache_ref is not None:
            @pl.when(i % tiles_per_seq >= tiles_per_seq - kept_tiles)
            def _():
                for h in range(SWA_HEADS):
                    cache_ref[pl.ds(h, tm, stride=SWA_HEADS), :] = res[:, h * SWA_HD:(h + 1) * SWA_HD]

    for step, cache_ref in enumerate((None, kc_ref, vc_ref)):
        pl.when(j == GLA_STEPS + step)(functools.partial(swa_step, cache_ref))


def _inproj(x, g, w_main, w_z, w_gate, b_gate, *, tm, seq, keep):
    m, d = x.shape
    tn = SWA_W
    tps, kt = seq // tm, keep // tm
    assert seq % tm == 0 and keep % tm == 0 and m % seq == 0

    def cache_block(i, j):
        return ((i // tps) * kt + jnp.maximum(i % tps - (tps - kt), 0), 0)

    cache_rows = (m // seq) * keep * SWA_HEADS
    return pl.pallas_call(
        functools.partial(_inproj_kernel, tiles_per_seq=tps, kept_tiles=kt),
        grid=(m // tm, PROJ_W // tn),
        in_specs=[
            pl.BlockSpec(memory_space=pl.ANY),
            pl.BlockSpec((1, d), lambda i, j: (0, 0)),
            pl.BlockSpec((d, tn), lambda i, j: (0, j)),
            pl.BlockSpec((d, LANE), lambda i, j: (0, 0)),
            pl.BlockSpec((LANE, GLA_QK_W), lambda i, j: (0, 0)),
            pl.BlockSpec((1, GLA_QK_W), lambda i, j: (0, 0)),
        ],
        out_specs=[
            pl.BlockSpec((tm, tn), lambda i, j: (i, jnp.minimum(j, GLA_STEPS - 1))),
            pl.BlockSpec((tm, GLA_QK_W), lambda i, j: (i, 0)),
            pl.BlockSpec((1, SWA_HEADS, tm, SWA_HD), lambda i, j: (jnp.maximum(j - GLA_STEPS, 0), 0, i, 0)),
            pl.BlockSpec((tm * SWA_HEADS, SWA_HD), cache_block, pipeline_mode=pl.Buffered(1)),
            pl.BlockSpec((tm * SWA_HEADS, SWA_HD), cache_block, pipeline_mode=pl.Buffered(1)),
        ],
        out_shape=[jax.ShapeDtypeStruct((m, GLA_COLS), BF16),
                   jax.ShapeDtypeStruct((m, GLA_QK_W), F32),
                   jax.ShapeDtypeStruct((3, SWA_HEADS, m, SWA_HD), F32),
                   jax.ShapeDtypeStruct((cache_rows, SWA_HD), F32),
                   jax.ShapeDtypeStruct((cache_rows, SWA_HD), F32)],
        scratch_shapes=[pltpu.VMEM((tm, d), F32), pltpu.VMEM((tm, d), BF16), pltpu.SemaphoreType.DMA((1,))],
        compiler_params=_params(("arbitrary", "arbitrary")),
        name="norm_inproj",
    )(x, g, w_main, w_z, w_gate, b_gate)


def _split3(x):
    hi = x.astype(BF16)
    r1 = x - hi.astype(F32)
    mid = r1.astype(BF16)
    lo = (r1 - mid.astype(F32)).astype(BF16)
    return hi, mid, lo


def _gla_kernel(*refs, chunk, nsub, has_s0):
    scratch = refs[-5:]
    s_scr, qd_scr, oi_scr, u_scr, dec_scr = scratch
    if has_s0:
        q_ref, k_ref, v_ref, ga_ref, la_ref, gn_ref, s0_ref, o_ref, sfin_ref = refs[:-5]
    else:
        q_ref, k_ref, v_ref, ga_ref, la_ref, gn_ref, o_ref, sfin_ref = refs[:-5]
    c = pl.program_id(1)

    @pl.when(c == 0)
    def _():
        if has_s0:
            s_scr[...] = s0_ref[0]
        else:
            s_scr[...] = jnp.zeros_like(s_scr)

    row = lax.broadcasted_iota(jnp.int32, (chunk, chunk), 0)
    col = lax.broadcasted_iota(jnp.int32, (chunk, chunk), 1)
    causal = row >= col
    tril = jnp.where(causal, 1.0, 0.0).astype(BF16)
    eye = (lax.broadcasted_iota(jnp.int32, (GLA_DK, GLA_DK), 0)
           == lax.broadcasted_iota(jnp.int32, (GLA_DK, GLA_DK), 1))

    group = min(nsub, 4)
    heads = range(GLA_HEADS)
    ks = [slice(h * GLA_DK, (h + 1) * GLA_DK) for h in heads]
    vs = [slice(h * GLA_DV, (h + 1) * GLA_DV) for h in heads]

    def local(c2, carry):
        subs = [c2 * group + a for a in range(group)]
        sls = [pl.ds(pl.multiple_of(i * chunk, chunk), chunk) for i in subs]
        pieces = [_split3(la_ref[0, sl, :]) for sl in sls]
        cums = [_dot(tril, hi) + _dot(tril, mid) + _dot(tril, lo) for hi, mid, lo in pieces]
        q_dec, k_dec, k_end, grow = [], [], [], []
        for sl, cum in zip(sls, cums):
            total = cum[chunk - 1:chunk, :]
            k = k_ref[0, sl, :].astype(F32)
            q_dec.append((q_ref[0, sl, :].astype(F32) * (GLA_DK ** -0.5) * jnp.exp(cum)).astype(BF16))
            k_dec.append((k * jnp.exp(-cum)).astype(BF16))
            k_end.append((k * jnp.exp(total - cum)).astype(BF16))
            grow.append(jnp.exp(total))
            qd_scr[sl, :] = q_dec[-1]
        units = [(a, h) for a in range(group) for h in heads]
        vals = {(a, h): v_ref[0, sls[a], vs[h]] for a, h in units}
        raw = {(a, h): _dot_nt(q_dec[a][:, ks[h]], k_dec[a][:, ks[h]]) for a, h in units}
        for a, h in units:
            u_scr[subs[a], h] = _dot_tn(k_end[a][:, ks[h]], vals[a, h])
        attn = {(a, h): jnp.where(causal, raw[a, h], 0.0).astype(BF16) for a, h in units}
        for a, h in units:
            oi_scr[sls[a], vs[h]] = _dot(attn[a, h], vals[a, h])
        for a, h in units:
            decay = jnp.sum(jnp.where(eye, grow[a][:, ks[h]], 0.0), axis=1, keepdims=True)
            dec_scr[subs[a], h] = jnp.broadcast_to(decay, (GLA_DK, GLA_DK))
        return carry

    if nsub == group:
        local(0, 0)
    else:
        lax.fori_loop(0, nsub // group, local, 0)

    for i in range(nsub):
        sl = slice(i * chunk, (i + 1) * chunk)
        states = [s_scr[h] for h in heads]
        inter = [_dot(qd_scr[sl, ks[h]], states[h].astype(BF16)) for h in heads]
        for h in heads:
            decay = dec_scr[i, h]
            s_scr[h] = states[h] * jnp.concatenate([decay] * (GLA_DV // GLA_DK), axis=1) + u_scr[i, h]
        for h in heads:
            on = _rms_rows(oi_scr[sl, vs[h]] + inter[h], gn_ref[:, vs[h]])
            o_ref[0, sl, vs[h]] = (on * _silu(ga_ref[0, sl, vs[h]].astype(F32))).astype(o_ref.dtype)

    @pl.when(c == pl.num_programs(1) - 1)
    def _():
        sfin_ref[0] = s_scr[...]


def _gla(proj3, loga3, g_norm, s0, *, chunk, tc):
    b, t, _ = proj3.shape
    nsub = tc // chunk
    state = pl.BlockSpec((1, GLA_HEADS, GLA_DK, GLA_DV), lambda bi, c: (bi, 0, 0, 0))
    in_specs = [
        pl.BlockSpec((1, tc, GLA_QK_W), lambda bi, c: (bi, c, 0)),
        pl.BlockSpec((1, tc, GLA_QK_W), lambda bi, c: (bi, c, 1)),
        pl.BlockSpec((1, tc, GLA_V_W), lambda bi, c: (bi, c, 1)),
        pl.BlockSpec((1, tc, GLA_V_W), lambda bi, c: (bi, c, 2)),
        pl.BlockSpec((1, tc, GLA_QK_W), lambda bi, c: (bi, c, 0)),
        pl.BlockSpec((1, GLA_V_W), lambda bi, c: (0, 0)),
    ]
    args = [proj3, proj3, proj3, proj3, loga3, g_norm]
    if s0 is not None:
        in_specs.append(state)
        args.append(s0)
    return pl.pallas_call(
        functools.partial(_gla_kernel, chunk=chunk, nsub=nsub, has_s0=s0 is not None),
        grid=(b, t // tc),
        in_specs=in_specs,
        out_specs=[pl.BlockSpec((1, tc, GLA_V_W), lambda bi, c: (bi, c, 0)), state],
        out_shape=[jax.ShapeDtypeStruct((b, t, GLA_V_W), BF16),
                   jax.ShapeDtypeStruct((b, GLA_HEADS, GLA_DK, GLA_DV), F32)],
        scratch_shapes=[pltpu.VMEM((GLA_HEADS, GLA_DK, GLA_DV), F32),
                        pltpu.VMEM((tc, GLA_QK_W), BF16),
                        pltpu.VMEM((tc, GLA_V_W), F32),
                        pltpu.VMEM((nsub, GLA_HEADS, GLA_DK, GLA_DV), F32),
                        pltpu.VMEM((nsub, GLA_HEADS, GLA_DK, GLA_DK), F32)],
        compiler_params=_params(("parallel", "arbitrary")),
        name="gla",
    )(*args)


SWA_SUPER = 2048
SWA_TILES = 8
QUAD = 4


def _swa_prompt_kernel(q_ref, k_ref, v_ref, o_ref, x4, q1, q4, q16, k1, k4, k16, v1, v4, v16, ob, lb, bias):
    n = SWA_BLOCK
    s_tok = q_ref.shape[0]
    l4, l16 = s_tok // QUAD, s_tok // (QUAD * QUAD)
    head, has_prev = pl.program_id(1), pl.program_id(2) > 0
    scale = SWA_HD ** -0.5

    slope = ALIBI_SLOPES[0]
    for h in range(1, SWA_HEADS):
        slope = jnp.where(head == h, ALIBI_SLOPES[h], slope)
    qi = lax.broadcasted_iota(jnp.int32, (n, 2 * n), 0)
    kj = lax.broadcasted_iota(jnp.int32, (n, 2 * n), 1)
    steps = qi + n - kj
    visible = (steps >= 0) & (steps <= n)
    first_visible = visible & (has_prev | (kj >= n))
    for g, (_, dil) in enumerate(SWA_PATTERNS):
        alibi = (-float(dil) * slope) * steps.astype(F32)
        bias[2 * g] = jnp.where(visible, alibi, NEG)
        bias[2 * g + 1] = jnp.where(first_visible, alibi, NEG)

    for buf, length in ((k1, s_tok), (v1, s_tok), (k4, l4), (v4, l4), (k16, l16), (v16, l16)):
        lead = buf.shape[:-2]
        idx = (slice(None),) * len(lead)

        @pl.when(has_prev)
        def _():
            buf[idx + (slice(0, n),)] = buf[idx + (slice(length, length + n),)]

        @pl.when(jnp.logical_not(has_prev))
        def _():
            buf[idx + (slice(0, n),)] = jnp.zeros(lead + (n, SWA_HD), BF16)

    for t, src in enumerate((q_ref, k_ref, v_ref)):
        for r in range(QUAD):
            x4[t, r] = src[pl.ds(r, l4, stride=QUAD), :]
    q1[...] = q_ref[...].astype(BF16)
    k1[n:, :] = k_ref[...].astype(BF16)
    v1[n:, :] = v_ref[...].astype(BF16)
    for r in range(QUAD):
        q4[r] = x4[0, r].astype(BF16)
        k4[r, n:, :] = x4[1, r].astype(BF16)
        v4[r, n:, :] = x4[2, r].astype(BF16)
    for r in range(QUAD * QUAD):
        sub = pl.ds(r // QUAD, l16, stride=QUAD)
        q16[r] = x4[0, r % QUAD, sub, :].astype(BF16)
        k16[r, n:, :] = x4[1, r % QUAD, sub, :].astype(BF16)
        v16[r, n:, :] = x4[2, r % QUAD, sub, :].astype(BF16)

    def tile(q, keys, values, table):
        s = _dot_nt(q, keys) * scale + table
        m = jnp.max(s, axis=1, keepdims=True)
        p = jnp.exp(s - m)
        l = jnp.sum(p, axis=1, keepdims=True)
        acc = _dot(p.astype(BF16), values)
        return acc * (1.0 / l), jnp.broadcast_to(m + jnp.log(l), (n, SWA_HD))

    def d1_body(c, carry):
        for a in range(SWA_TILES):
            i = c * SWA_TILES + a
            row0 = pl.multiple_of(i * n, n)
            o, lse = tile(q1[pl.ds(row0, n), :], k1[pl.ds(row0, 2 * n), :], v1[pl.ds(row0, 2 * n), :],
                          bias[jnp.where(i == 0, 1, 0)])
            ob[0, pl.ds(row0, n), :] = o
            lb[0, pl.ds(row0, n), :] = lse
        return carry

    lax.fori_loop(0, s_tok // n // SWA_TILES, d1_body, 0)

    per4 = l4 // n

    def d4_body(c, carry):
        for a in range(SWA_TILES // per4):
            r = c * (SWA_TILES // per4) + a
            for i in range(per4):
                o, lse = tile(q4[r, i * n:(i + 1) * n, :], k4[r, i * n:(i + 2) * n, :], v4[r, i * n:(i + 2) * n, :],
                              bias[2 + (1 if i == 0 else 0)])
                row0 = pl.multiple_of(r * l4 + i * n, n)
                ob[1, pl.ds(row0, n), :] = o
                lb[1, pl.ds(row0, n), :] = lse
        return carry

    lax.fori_loop(0, QUAD * per4 // SWA_TILES, d4_body, 0)

    per16 = l16 // n

    def d16_body(c, carry):
        for a in range(SWA_TILES // per16):
            r4 = a % QUAD
            j = c * (SWA_TILES // per16 // QUAD) + a // QUAD
            r = r4 + QUAD * j
            for i in range(per16):
                o, lse = tile(q16[r, i * n:(i + 1) * n, :], k16[r, i * n:(i + 2) * n, :],
                              v16[r, i * n:(i + 2) * n, :], bias[4 + (1 if i == 0 else 0)])
                rows = pl.ds(r4 * l4 + QUAD * i * n + j, n, stride=QUAD)
                ob[2, rows, :] = o
                lb[2, rows, :] = lse
        return carry

    lax.fori_loop(0, QUAD * QUAD * per16 // SWA_TILES, d16_body, 0)

    def combine_body(c, carry):
        for r in range(QUAD):
            tok = pl.ds(r + QUAD * c * n, n, stride=QUAD)
            cls = pl.ds(pl.multiple_of(r * l4 + c * n, n), n)
            lses = [lb[0, tok, :], lb[1, cls, :], lb[2, cls, :]]
            outs = [ob[0, tok, :], ob[1, cls, :], ob[2, cls, :]]
            m = jnp.maximum(jnp.maximum(lses[0], lses[1]), lses[2])
            es = [jnp.exp(x - m) for x in lses]
            num = es[0] * outs[0] + es[1] * outs[1] + es[2] * outs[2]
            ob[0, tok, :] = num * (1.0 / (es[0] + es[1] + es[2]))
        return carry

    lax.fori_loop(0, l4 // n, combine_body, 0)
    o_ref[...] = ob[0].astype(o_ref.dtype)


def _swa_prompt(qkv, b, t):
    n, s_tok = SWA_BLOCK, SWA_SUPER
    assert t % s_tok == 0 and [d for _, d in SWA_PATTERNS] == [1, QUAD, QUAD * QUAD]
    assert all(w // d == n for w, d in SWA_PATTERNS)
    l4, l16 = s_tok // QUAD, s_tok // (QUAD * QUAD)
    assert (s_tok // n) % SWA_TILES == 0 and SWA_TILES % (l4 // n) == 0 and SWA_TILES % (QUAD * l16 // n) == 0
    view = qkv.reshape(3, SWA_HEADS, b, t, SWA_HD)

    def part(idx):
        return pl.BlockSpec((None, None, None, s_tok, SWA_HD), lambda bi, h, sb: (idx, h, bi, sb, 0))

    def rows(*lead):
        return pltpu.VMEM(lead + (SWA_HD,), BF16)

    o = pl.pallas_call(
        _swa_prompt_kernel,
        grid=(b, SWA_HEADS, t // s_tok),
        in_specs=[part(0), part(1), part(2)],
        out_specs=pl.BlockSpec((None, s_tok, SWA_HD), lambda bi, h, sb: (bi, sb, h)),
        out_shape=jax.ShapeDtypeStruct((b, t, SWA_W), BF16),
        scratch_shapes=[
            pltpu.VMEM((3, QUAD, l4, SWA_HD), F32),
            rows(s_tok), rows(QUAD, l4), rows(QUAD * QUAD, l16),
            rows(n + s_tok), rows(QUAD, n + l4), rows(QUAD * QUAD, n + l16),
            rows(n + s_tok), rows(QUAD, n + l4), rows(QUAD * QUAD, n + l16),
            pltpu.VMEM((3, s_tok, SWA_HD), F32), pltpu.VMEM((3, s_tok, SWA_HD), F32),
            pltpu.VMEM((2 * len(SWA_PATTERNS), n, 2 * n), F32),
        ],
        compiler_params=_params(("parallel", "parallel", "arbitrary")),
        name="swa_prompt",
    )(view, view, view)
    return o.reshape(b * t, SWA_W)


def _branch_multiplicity(d):
    mult = jnp.zeros(d.shape, F32)
    for window, dil in SWA_PATTERNS:
        hit = (d >= 0) & (d <= window) & ((d & (dil - 1)) == 0)
        mult = mult + jnp.where(hit, 1.0, 0.0)
    return mult


def _swa_sample_kernel(q_ref, kn_ref, vn_ref, kc_ref, vc_ref, o_ref, *, w, tn):
    tp = q_ref.shape[1]

    d_c = (w + lax.broadcasted_iota(jnp.int32, (tp, w), 0)
           - lax.broadcasted_iota(jnp.int32, (tp, w), 1))
    mult_c = _branch_multiplicity(d_c)
    i_n = lax.broadcasted_iota(jnp.int32, (tp, tp), 0)
    j_n = lax.broadcasted_iota(jnp.int32, (tp, tp), 1)
    d_n = jnp.where(j_n < tn, i_n - j_n, -1)
    mult_n = _branch_multiplicity(d_n)
    dist_c = d_c.astype(F32)
    dist_n = d_n.astype(F32)
    scale = SWA_HD ** -0.5

    for h in range(SWA_HEADS):
        hs = slice(h * SWA_HD, (h + 1) * SWA_HD)
        head_rows = pl.ds(h, w, stride=SWA_HEADS)
        slope = ALIBI_SLOPES[h]
        q = q_ref[0, :, hs]
        s_c = _dot_nt(q, kc_ref[0, head_rows, :].astype(BF16)) * scale - slope * dist_c
        s_n = _dot_nt(q, kn_ref[0, :, hs]) * scale - slope * dist_n
        s_c = jnp.where(mult_c > 0.0, s_c, NEG)
        s_n = jnp.where(mult_n > 0.0, s_n, NEG)
        m = jnp.maximum(jnp.max(s_c, axis=1, keepdims=True), jnp.max(s_n, axis=1, keepdims=True))
        p_c = mult_c * jnp.exp(s_c - m)
        p_n = mult_n * jnp.exp(s_n - m)
        l = jnp.sum(p_c, axis=1, keepdims=True) + jnp.sum(p_n, axis=1, keepdims=True)
        acc = (_dot(p_c.astype(BF16), vc_ref[0, head_rows, :].astype(BF16))
               + _dot(p_n.astype(BF16), vn_ref[0, :, hs]))
        o_ref[0, :, hs] = (acc / l).astype(o_ref.dtype)


def _swa_sample(projp, cache_k, cache_v, *, tn):
    b, tp, _ = projp.shape
    rows = cache_k.shape[1]
    w = rows // SWA_HEADS

    def new(cb):
        return pl.BlockSpec((1, tp, SWA_W), lambda bi: (bi, 0, cb))

    cache = pl.BlockSpec((1, rows, SWA_HD), lambda bi: (bi, 0, 0))
    return pl.pallas_call(
        functools.partial(_swa_sample_kernel, w=w, tn=tn),
        grid=(b,),
        in_specs=[new(0), new(1), new(2), cache, cache],
        out_specs=pl.BlockSpec((1, tp, SWA_W), lambda bi: (bi, 0, 0)),
        out_shape=jax.ShapeDtypeStruct((b, tp, SWA_W), BF16),
        compiler_params=_params(("parallel",)),
        name="swa_sample",
    )(projp, projp, projp, cache_k, cache_v)


def _outproj_kernel(x_ref, oa_ref, ob_ref, wa_ref, wb_ref, h_ref):
    h_ref[...] = x_ref[...] + _dot(oa_ref[...], wa_ref[...]) + _dot(ob_ref[...], wb_ref[...])


def _outproj(x, o_a, o_b, w_out, *, tm):
    m, d = x.shape
    return pl.pallas_call(
        _outproj_kernel,
        grid=(m // tm,),
        in_specs=[pl.BlockSpec((tm, d), lambda i: (i, 0)),
                  pl.BlockSpec((tm, GLA_V_W), lambda i: (i, 0)),
                  pl.BlockSpec((tm, SWA_W), lambda i: (i, 0)),
                  pl.BlockSpec((GLA_V_W, d), lambda i: (0, 0)),
                  pl.BlockSpec((SWA_W, d), lambda i: (GLA_V_W // SWA_W, 0))],
        out_specs=pl.BlockSpec((tm, d), lambda i: (i, 0)),
        out_shape=jax.ShapeDtypeStruct((m, d), F32),
        compiler_params=_params(("parallel",)),
        name="outproj",
    )(x, o_a, o_b, w_out, w_out)


FFN_HALO = 16


def _ffn_prompt_kernel(h_ref, halo_ref, g_ref, wa_ref, wv_ref, ca_ref, cv_ref, ba_ref, bv_ref, wd_ref,
                       gf_ref, kc_hbm, vc_hbm, kn_hbm, vn_hbm, y_ref, ua_ref, uv_ref, ko_hbm, vo_hbm,
                       xn_scr, ea_scr, ev_scr, acc_scr, sems, *, tiles_per_seq, per_tile):
    i, j = pl.program_id(0), pl.program_id(1)
    tm = h_ref.shape[0]

    nseq, rows = kc_hbm.shape[0], kc_hbm.shape[1]
    new = kn_hbm.shape[1]

    def cache_copies(a):
        seq = i * per_tile + a
        return seq < nseq, [
            pltpu.make_async_copy(kc_hbm.at[seq, pl.ds(new, rows - new)], ko_hbm.at[seq, pl.ds(0, rows - new)],
                                  sems.at[a, 0]),
            pltpu.make_async_copy(kn_hbm.at[seq], ko_hbm.at[seq, pl.ds(rows - new, new)], sems.at[a, 1]),
            pltpu.make_async_copy(vc_hbm.at[seq, pl.ds(new, rows - new)], vo_hbm.at[seq, pl.ds(0, rows - new)],
                                  sems.at[a, 2]),
            pltpu.make_async_copy(vn_hbm.at[seq], vo_hbm.at[seq, pl.ds(rows - new, new)], sems.at[a, 3]),
        ]

    for a in range(per_tile):
        valid, copies = cache_copies(a)

        @pl.when(valid & (j == 0))
        def _():
            for cp in copies:
                cp.start()

    @pl.when(j == 0)
    def _():
        first = (i % tiles_per_seq) == 0
        halo = _rms_rows(halo_ref[...], g_ref[...])
        xn_scr[0:FFN_HALO, :] = jnp.where(first, 0.0, halo).astype(BF16)
        xn_scr[FFN_HALO:, :] = _rms_rows(h_ref[...], g_ref[...]).astype(BF16)
        acc_scr[...] = jnp.zeros_like(acc_scr)

    xn = xn_scr[...]
    ea_scr[...] = _dot(xn, wa_ref[...])
    ev_scr[...] = _dot(xn, wv_ref[...])

    def conv(e_scr, c_ref, b_ref):
        out = b_ref[...]
        for tap in range(CONV_W):
            off = FFN_HALO - (CONV_W - 1) + tap
            out = out + c_ref[tap:tap + 1, :] * e_scr[off:off + tm, :]
        return out

    gate = (_silu(conv(ea_scr, ca_ref, ba_ref)) * conv(ev_scr, cv_ref, bv_ref)).astype(BF16)
    acc_scr[...] += _dot(gate, wd_ref[...])
    ua_ref[0] = ea_scr[FFN_HALO + tm - 8:FFN_HALO + tm, :]
    uv_ref[0] = ev_scr[FFN_HALO + tm - 8:FFN_HALO + tm, :]

    @pl.when(j == pl.num_programs(1) - 1)
    def _():
        y_ref[...] = _rms_rows(h_ref[...] + acc_scr[...], gf_ref[...])

    for a in range(per_tile):
        valid, copies = cache_copies(a)

        @pl.when(valid & (j == pl.num_programs(1) - 1))
        def _():
            for cp in copies:
                cp.wait()


def _ffn_prompt(h, g_ffn, w_up, w_conv, b_conv, w_down, g_final, cache_k, cache_v, k_new, v_new, *, seq, tm, tf):
    m, d = h.shape
    f = w_down.shape[0]
    nf = f // tf
    ntile = m // tm
    hb = tm // FFN_HALO
    per_tile = -(-cache_k.shape[0] // ntile)
    hbm = pl.BlockSpec(memory_space=pl.ANY)
    y, ua, uv, k_all, v_all = pl.pallas_call(
        functools.partial(_ffn_prompt_kernel, tiles_per_seq=seq // tm, per_tile=per_tile),
        grid=(ntile, nf),
        in_specs=[
            pl.BlockSpec((tm, d), lambda i, j: (i, 0)),
            pl.BlockSpec((FFN_HALO, d), lambda i, j: (jnp.maximum(i * hb - 1, 0), 0)),
            pl.BlockSpec((1, d), lambda i, j: (0, 0)),
            pl.BlockSpec((d, tf), lambda i, j: (0, j)),
            pl.BlockSpec((d, tf), lambda i, j: (0, nf + j)),
            pl.BlockSpec((CONV_W, tf), lambda i, j: (0, j)),
            pl.BlockSpec((CONV_W, tf), lambda i, j: (0, nf + j)),
            pl.BlockSpec((1, tf), lambda i, j: (0, j)),
            pl.BlockSpec((1, tf), lambda i, j: (0, nf + j)),
            pl.BlockSpec((tf, d), lambda i, j: (j, 0)),
            pl.BlockSpec((1, d), lambda i, j: (0, 0)),
            hbm, hbm, hbm, hbm,
        ],
        out_specs=[
            pl.BlockSpec((tm, d), lambda i, j: (i, 0)),
            pl.BlockSpec((1, 8, tf), lambda i, j: (i, 0, j)),
            pl.BlockSpec((1, 8, tf), lambda i, j: (i, 0, j)),
            hbm, hbm,
        ],
        out_shape=[jax.ShapeDtypeStruct((m, d), F32),
                   jax.ShapeDtypeStruct((ntile, 8, f), F32),
                   jax.ShapeDtypeStruct((ntile, 8, f), F32),
                   jax.ShapeDtypeStruct(cache_k.shape, cache_k.dtype),
                   jax.ShapeDtypeStruct(cache_v.shape, cache_v.dtype)],
        scratch_shapes=[pltpu.VMEM((tm + FFN_HALO, d), BF16),
                        pltpu.VMEM((tm + FFN_HALO, tf), F32),
                        pltpu.VMEM((tm + FFN_HALO, tf), F32),
                        pltpu.VMEM((tm, d), F32),
                        pltpu.SemaphoreType.DMA((per_tile, 4))],
        compiler_params=_params(("arbitrary", "arbitrary")),
        name="convffn_prompt",
    )(h, h, g_ffn, w_up, w_up, w_conv, w_conv, b_conv, b_conv, w_down, g_final, cache_k, cache_v, k_new, v_new)
    return y, ua, uv, k_all, v_all


def _ffn_sample_kernel(h_ref, g_ref, sa_ref, sv_ref, wa_ref, wv_ref, ca_ref, cv_ref, ba_ref, bv_ref, wd_ref,
                       gf_ref, y_ref, na_ref, nv_ref, xn_scr, acc_scr, *, nb):
    j = pl.program_id(0)
    m = h_ref.shape[0]
    keep = (CONV_W - 1) * nb

    @pl.when(j == 0)
    def _():
        xn_scr[...] = _rms_rows(h_ref[...], g_ref[...]).astype(BF16)
        acc_scr[...] = jnp.zeros_like(acc_scr)

    xn = xn_scr[...]

    def conv(s_ref, w_ref, c_ref, b_ref, n_ref):
        u = _dot(xn, w_ref[...])
        ext = jnp.concatenate([s_ref[...], u], axis=0)
        n_ref[...] = ext[m:m + keep, :]
        out = b_ref[...]
        for tap in range(CONV_W):
            out = out + c_ref[tap:tap + 1, :] * ext[tap * nb:tap * nb + m, :]
        return out

    gate = (_silu(conv(sa_ref, wa_ref, ca_ref, ba_ref, na_ref))
            * conv(sv_ref, wv_ref, cv_ref, bv_ref, nv_ref)).astype(BF16)
    acc_scr[...] += _dot(gate, wd_ref[...])

    @pl.when(j == pl.num_programs(0) - 1)
    def _():
        y_ref[...] = _rms_rows(h_ref[...] + acc_scr[...], gf_ref[...])


def _ffn_sample(h_tm, state_tm, g_ffn, w_up, w_conv, b_conv, w_down, g_final, *, nb, tf):
    m, d = h_tm.shape
    f = w_down.shape[0]
    nf = f // tf
    keep = (CONV_W - 1) * nb
    return pl.pallas_call(
        functools.partial(_ffn_sample_kernel, nb=nb),
        grid=(nf,),
        in_specs=[
            pl.BlockSpec((m, d), lambda j: (0, 0)),
            pl.BlockSpec((1, d), lambda j: (0, 0)),
            pl.BlockSpec((keep, tf), lambda j: (0, j)),
            pl.BlockSpec((keep, tf), lambda j: (0, nf + j)),
            pl.BlockSpec((d, tf), lambda j: (0, j)),
            pl.BlockSpec((d, tf), lambda j: (0, nf + j)),
            pl.BlockSpec((CONV_W, tf), lambda j: (0, j)),
            pl.BlockSpec((CONV_W, tf), lambda j: (0, nf + j)),
            pl.BlockSpec((1, tf), lambda j: (0, j)),
            pl.BlockSpec((1, tf), lambda j: (0, nf + j)),
            pl.BlockSpec((tf, d), lambda j: (j, 0)),
            pl.BlockSpec((1, d), lambda j: (0, 0)),
        ],
        out_specs=[
            pl.BlockSpec((m, d), lambda j: (0, 0)),
            pl.BlockSpec((keep, tf), lambda j: (0, j)),
            pl.BlockSpec((keep, tf), lambda j: (0, j)),
        ],
        out_shape=[jax.ShapeDtypeStruct((m, d), F32),
                   jax.ShapeDtypeStruct((keep, f), F32),
                   jax.ShapeDtypeStruct((keep, f), F32)],
        scratch_shapes=[pltpu.VMEM((m, d), BF16), pltpu.VMEM((m, d), F32)],
        compiler_params=_params(("arbitrary",)),
        name="convffn_sample",
    )(h_tm, g_ffn, state_tm, state_tm, w_up, w_up, w_conv, w_conv, b_conv, b_conv, w_down, g_final)


def _pack_weights(g_attn_norm, w_in, w_gate_up, b_gate, g_gla_norm, w_out, g_ffn_norm, w_up, w_conv, b_conv,
                  w_down):
    z0 = 2 * GLA_QK_W + 2 * GLA_V_W
    z1 = z0 + GLA_GATE_RANK
    w_main = jnp.concatenate([w_in[:, :z0], w_in[:, z1:]], axis=1).astype(BF16)
    w_z = jnp.pad(w_in[:, z0:z1], ((0, 0), (0, LANE - GLA_GATE_RANK))).astype(BF16)
    w_gate = jnp.pad(w_gate_up, ((0, LANE - GLA_GATE_RANK), (0, 0))).astype(BF16)
    return dict(
        g_attn=g_attn_norm[None, :], w_main=w_main, w_z=w_z, w_gate=w_gate, b_gate=b_gate[None, :],
        g_gla=g_gla_norm[None, :], w_out=w_out.astype(BF16), g_ffn=g_ffn_norm[None, :],
        w_up=w_up.astype(BF16), w_conv=w_conv, b_conv=b_conv[None, :], w_down=w_down.astype(BF16))


def _prompt_layer(x, p, g_final, sample_cache):
    b, t, d = x.shape
    m = b * t
    f = p["w_down"].shape[0]
    keep = min(SWA_MAX_WINDOW, t)
    x2 = x.reshape(m, d)
    proj, loga, qkv, k_buf, v_buf = _inproj(x2, p["g_attn"], p["w_main"], p["w_z"], p["w_gate"], p["b_gate"],
                                            tm=1024, seq=t, keep=keep)
    o_a, s_fin = _gla(proj.reshape(b, t, GLA_COLS), loga.reshape(b, t, GLA_QK_W), p["g_gla"], None,
                      chunk=GLA_CHUNK, tc=512)
    o_b = _swa_prompt(qkv, b, t)
    h = _outproj(x2, o_a.reshape(m, GLA_V_W), o_b, p["w_out"], tm=512)
    tm = 512
    y, ua, uv, k_all, v_all = _ffn_prompt(h, p["g_ffn"], p["w_up"], p["w_conv"], p["b_conv"], p["w_down"], g_final,
                                          *sample_cache, seq=t, tm=tm, tf=512)
    shape = (b, keep, SWA_HEADS, SWA_HD)
    last = jnp.concatenate([ua, uv], axis=-1).reshape(b, t // tm, 8, 2 * f)[:, -1, 8 - (CONV_W - 1):, :]
    return y.reshape(b, t, d), s_fin, k_buf.reshape(shape), v_buf.reshape(shape), last, k_all, v_all


def _sample_layer(x, s0, conv_s0, cache_k, cache_v, p, g_final):
    b, tn, d = x.shape
    m = b * tn
    tp = 8
    w = cache_k.shape[1]
    x2 = x.reshape(m, d)
    proj, loga, qkv, k_new, v_new = _inproj(x2, p["g_attn"], p["w_main"], p["w_z"], p["w_gate"], p["b_gate"],
                                            tm=m, seq=m, keep=m)
    pad = ((0, 0), (0, tp - tn), (0, 0))
    projp = jnp.pad(proj.reshape(b, tn, GLA_COLS), pad)
    logap = jnp.pad(loga.reshape(b, tn, GLA_QK_W), pad)
    o_a, s_fin = _gla(projp, logap, p["g_gla"], s0, chunk=tp, tc=tp)
    qkvp = jnp.pad(qkv.transpose(2, 0, 1, 3).reshape(b, tn, 3 * SWA_W).astype(BF16), pad)
    rows = (b, tn * SWA_HEADS, SWA_HD)
    cache = (cache_k.reshape(b, w * SWA_HEADS, SWA_HD), cache_v.reshape(b, w * SWA_HEADS, SWA_HD),
             k_new.reshape(rows), v_new.reshape(rows))
    o_b = _swa_sample(qkvp, cache[0], cache[1], tn=tn)
    h = _outproj(x2, o_a[:, :tn].reshape(m, GLA_V_W), o_b[:, :tn].reshape(m, SWA_W), p["w_out"], tm=m)
    h_tm = h.reshape(b, tn, d).transpose(1, 0, 2).reshape(m, d)
    state_tm = conv_s0.transpose(1, 0, 2).reshape((CONV_W - 1) * b, -1)
    y_tm, na, nv = _ffn_sample(h_tm, state_tm, p["g_ffn"], p["w_up"], p["w_conv"], p["b_conv"], p["w_down"],
                               g_final, nb=b, tf=512)
    y = y_tm.reshape(tn, b, d).transpose(1, 0, 2)
    conv_new = jnp.concatenate([na, nv], axis=-1).reshape(CONV_W - 1, b, -1).transpose(1, 0, 2)
    return y, s_fin, conv_new, cache


def kernel(x_prompt, x_sample, state_gla, cache_swa_k, cache_swa_v, state_ffn_conv, g_attn_norm, w_in, w_gate_up,
           b_gate, g_gla_norm, w_out, g_ffn_norm, w_up, w_conv, b_conv, w_down, g_final):
    depth = w_in.shape[0]
    assert depth == 1, "the final RMSNorm is fused into the last layer's ConvFFN kernel"
    assert cache_swa_k.shape[2] == SWA_MAX_WINDOW and x_sample.shape[1] <= 8
    p = _pack_weights(g_attn_norm[0], w_in[0], w_gate_up[0], b_gate[0], g_gla_norm[0], w_out[0], g_ffn_norm[0],
                      w_up[0], w_conv[0], b_conv[0], w_down[0])
    gf = g_final[None, :]
    ys, s2, c2, sample_cache = _sample_layer(x_sample, state_gla[0], state_ffn_conv[0], cache_swa_k[0],
                                             cache_swa_v[0], p, gf)
    yp, s1, k1, v1, c1, k2, v2 = _prompt_layer(x_prompt, p, gf, sample_cache)
    k2, v2 = k2.reshape(cache_swa_k.shape[1:]), v2.reshape(cache_swa_v.shape[1:])
    return (yp, ys, s1[None], s2[None], k1[None], k2[None], v1[None], v2[None], c1[None], c2[None])
```

```python
import functools

import jax
import jax.numpy as jnp
from jax import lax
from jax.experimental import pallas as pl
from jax.experimental.pallas import tpu as pltpu

F32 = jnp.float32
BF16 = jnp.bfloat16

GLA_HEADS = 4
GLA_DK = 128
GLA_DV = 256
GLA_GATE_RANK = 16
GLA_TAU = 16.0
GLA_CHUNK = 64
SWA_HEADS = 8
SWA_HD = 128
SWA_PATTERNS = ((128, 1), (512, 4), (2048, 16))
SWA_BLOCK = 128
SWA_MAX_WINDOW = 2048
CONV_W = 3
EPS = 1e-6
NEG = -1e30

GLA_QK_W = GLA_HEADS * GLA_DK
GLA_V_W = GLA_HEADS * GLA_DV
SWA_W = SWA_HEADS * SWA_HD
PROJ_W = 2 * GLA_QK_W + 2 * GLA_V_W + 3 * SWA_W
LANE = 128
VMEM_LIMIT = 56 * 1024 * 1024

ALIBI_SLOPES = tuple(2.0 ** (-8.0 * (h + 1) / SWA_HEADS) for h in range(SWA_HEADS))


def _params(semantics):
    return pltpu.CompilerParams(dimension_semantics=semantics, vmem_limit_bytes=VMEM_LIMIT)


def _rms_rows(x, g):
    return x * lax.rsqrt(jnp.mean(x * x, axis=-1, keepdims=True) + EPS) * g


def _silu(x):
    return x / (1.0 + jnp.exp(-x))


def _log_sigmoid(x):
    return jnp.minimum(x, 0.0) - jnp.log1p(jnp.exp(-jnp.abs(x)))


def _dot(a, b):
    return jnp.dot(a, b, preferred_element_type=F32)


def _dot_nt(a, b):
    return lax.dot_general(a, b, (((1,), (1,)), ((), ())), preferred_element_type=F32)


def _dot_tn(a, b):
    return lax.dot_general(a, b, (((0,), (0,)), ((), ())), preferred_element_type=F32)


GLA_COLS = 2 * GLA_QK_W + 2 * GLA_V_W
GLA_STEPS = GLA_COLS // SWA_W


def _inproj_kernel(x_hbm, g_ref, w_ref, wz_ref, wg_ref, bg_ref, proj_ref, loga_ref, qkv_ref, kc_ref, vc_ref,
                   x_buf, xn_ref, sem, *, tiles_per_seq, kept_tiles):
    i, j = pl.program_id(0), pl.program_id(1)
    tm = x_buf.shape[0]

    def x_copy(tile):
        return pltpu.make_async_copy(x_hbm.at[pl.ds(tile * tm, tm)], x_buf, sem.at[0])

    @pl.when(j == 0)
    def _():
        @pl.when(i == 0)
        def _():
            x_copy(0).start()

        x_copy(i).wait()
        xn_ref[...] = _rms_rows(x_buf[...], g_ref[...]).astype(BF16)

        @pl.when(i + 1 < pl.num_programs(0))
        def _():
            x_copy(i + 1).start()

        za = _dot(xn_ref[...], wz_ref[...])
        gate = _dot(za.astype(BF16), wg_ref[...]) + bg_ref[...]
        loga_ref[...] = _log_sigmoid(gate) / GLA_TAU

    @pl.when(j < GLA_STEPS)
    def _():
        proj_ref[...] = _dot(xn_ref[...], w_ref[...]).astype(BF16)

    def swa_step(cache_ref):
        res = _dot(xn_ref[...], w_ref[...])
        for h in range(SWA_HEADS):
            qkv_ref[0, h] = res[:, h * SWA_HD:(h + 1) * SWA_HD]
        if cache_ref is not None:
            @pl.when(i % tiles_per_seq >= tiles_per_seq - kept_tiles)
            def _():
                for h in range(SWA_HEADS):
                    cache_ref[pl.ds(h, tm, stride=SWA_HEADS), :] = res[:, h * SWA_HD:(h + 1) * SWA_HD]

    for step, cache_ref in enumerate((None, kc_ref, vc_ref)):
        pl.when(j == GLA_STEPS + step)(functools.partial(swa_step, cache_ref))


def _inproj(x, g, w_main, w_z, w_gate, b_gate, *, tm, seq, keep):
    m, d = x.shape
    tn = SWA_W
    tps, kt = seq // tm, keep // tm
    assert seq % tm == 0 and keep % tm == 0 and m % seq == 0

    def cache_block(i, j):
        return ((i // tps) * kt + jnp.maximum(i % tps - (tps - kt), 0), 0)

    cache_rows = (m // seq) * keep * SWA_HEADS
    return pl.pallas_call(
        functools.partial(_inproj_kernel, tiles_per_seq=tps, kept_tiles=kt),
        grid=(m // tm, PROJ_W // tn),
        in_specs=[
            pl.BlockSpec(memory_space=pl.ANY),
            pl.BlockSpec((1, d), lambda i, j: (0, 0)),
            pl.BlockSpec((d, tn), lambda i, j: (0, j)),
            pl.BlockSpec((d, LANE), lambda i, j: (0, 0)),
            pl.BlockSpec((LANE, GLA_QK_W), lambda i, j: (0, 0)),
            pl.BlockSpec((1, GLA_QK_W), lambda i, j: (0, 0)),
        ],
        out_specs=[
            pl.BlockSpec((tm, tn), lambda i, j: (i, jnp.minimum(j, GLA_STEPS - 1))),
            pl.BlockSpec((tm, GLA_QK_W), lambda i, j: (i, 0)),
            pl.BlockSpec((1, SWA_HEADS, tm, SWA_HD), lambda i, j: (jnp.maximum(j - GLA_STEPS, 0), 0, i, 0)),
            pl.BlockSpec((tm * SWA_HEADS, SWA_HD), cache_block, pipeline_mode=pl.Buffered(1)),
            pl.BlockSpec((tm * SWA_HEADS, SWA_HD), cache_block, pipeline_mode=pl.Buffered(1)),
        ],
        out_shape=[jax.ShapeDtypeStruct((m, GLA_COLS), BF16),
                   jax.ShapeDtypeStruct((m, GLA_QK_W), F32),
                   jax.ShapeDtypeStruct((3, SWA_HEADS, m, SWA_HD), F32),
                   jax.ShapeDtypeStruct((cache_rows, SWA_HD), F32),
                   jax.ShapeDtypeStruct((cache_rows, SWA_HD), F32)],
        scratch_shapes=[pltpu.VMEM((tm, d), F32), pltpu.VMEM((tm, d), BF16), pltpu.SemaphoreType.DMA((1,))],
        compiler_params=_params(("arbitrary", "arbitrary")),
        name="norm_inproj",
    )(x, g, w_main, w_z, w_gate, b_gate)


def _split3(x):
    hi = x.astype(BF16)
    r1 = x - hi.astype(F32)
    mid = r1.astype(BF16)
    lo = (r1 - mid.astype(F32)).astype(BF16)
    return hi, mid, lo


def _gla_kernel(*refs, chunk, nsub, has_s0):
    scratch = refs[-5:]
    s_scr, qd_scr, oi_scr, u_scr, dec_scr = scratch
    if has_s0:
        q_ref, k_ref, v_ref, ga_ref, la_ref, gn_ref, s0_ref, o_ref, sfin_ref = refs[:-5]
    else:
        q_ref, k_ref, v_ref, ga_ref, la_ref, gn_ref, o_ref, sfin_ref = refs[:-5]
    c = pl.program_id(1)

    @pl.when(c == 0)
    def _():
        if has_s0:
            s_scr[...] = s0_ref[0]
        else:
            s_scr[...] = jnp.zeros_like(s_scr)

    row = lax.broadcasted_iota(jnp.int32, (chunk, chunk), 0)
    col = lax.broadcasted_iota(jnp.int32, (chunk, chunk), 1)
    causal = row >= col
    tril = jnp.where(causal, 1.0, 0.0).astype(BF16)
    eye = (lax.broadcasted_iota(jnp.int32, (GLA_DK, GLA_DK), 0)
           == lax.broadcasted_iota(jnp.int32, (GLA_DK, GLA_DK), 1))

    group = min(nsub, 4)
    heads = range(GLA_HEADS)
    ks = [slice(h * GLA_DK, (h + 1) * GLA_DK) for h in heads]
    vs = [slice(h * GLA_DV, (h + 1) * GLA_DV) for h in heads]

    def local(c2, carry):
        subs = [c2 * group + a for a in range(group)]
        sls = [pl.ds(pl.multiple_of(i * chunk, chunk), chunk) for i in subs]
        pieces = [_split3(la_ref[0, sl, :]) for sl in sls]
        cums = [_dot(tril, hi) + _dot(tril, mid) + _dot(tril, lo) for hi, mid, lo in pieces]
        q_dec, k_dec, k_end, grow = [], [], [], []
        for sl, cum in zip(sls, cums):
            total = cum[chunk - 1:chunk, :]
            k = k_ref[0, sl, :].astype(F32)
            q_dec.append((q_ref[0, sl, :].astype(F32) * (GLA_DK ** -0.5) * jnp.exp(cum)).astype(BF16))
            k_dec.append((k * jnp.exp(-cum)).astype(BF16))
            k_end.append((k * jnp.exp(total - cum)).astype(BF16))
            grow.append(jnp.exp(total))
            qd_scr[sl, :] = q_dec[-1]
        units = [(a, h) for a in range(group) for h in heads]
        vals = {(a, h): v_ref[0, sls[a], vs[h]] for a, h in units}
        raw = {(a, h): _dot_nt(q_dec[a][:, ks[h]], k_dec[a][:, ks[h]]) for a, h in units}
        for a, h in units:
            u_scr[subs[a], h] = _dot_tn(k_end[a][:, ks[h]], vals[a, h])
        attn = {(a, h): jnp.where(causal, raw[a, h], 0.0).astype(BF16) for a, h in units}
        for a, h in units:
            oi_scr[sls[a], vs[h]] = _dot(attn[a, h], vals[a, h])
        for a, h in units:
            decay = jnp.sum(jnp.where(eye, grow[a][:, ks[h]], 0.0), axis=1, keepdims=True)
            dec_scr[subs[a], h] = jnp.broadcast_to(decay, (GLA_DK, GLA_DK))
        return carry

    if nsub == group:
        local(0, 0)
    else:
        lax.fori_loop(0, nsub // group, local, 0)

    for i in range(nsub):
        sl = slice(i * chunk, (i + 1) * chunk)
        states = [s_scr[h] for h in heads]
        inter = [_dot(qd_scr[sl, ks[h]], states[h].astype(BF16)) for h in heads]
        for h in heads:
            decay = dec_scr[i, h]
            s_scr[h] = states[h] * jnp.concatenate([decay] * (GLA_DV // GLA_DK), axis=1) + u_scr[i, h]
        for h in heads:
            on = _rms_rows(oi_scr[sl, vs[h]] + inter[h], gn_ref[:, vs[h]])
            o_ref[0, sl, vs[h]] = (on * _silu(ga_ref[0, sl, vs[h]].astype(F32))).astype(o_ref.dtype)

    @pl.when(c == pl.num_programs(1) - 1)
    def _():
        sfin_ref[0] = s_scr[...]


def _gla(proj3, loga3, g_norm, s0, *, chunk, tc):
    b, t, _ = proj3.shape
    nsub = tc // chunk
    state = pl.BlockSpec((1, GLA_HEADS, GLA_DK, GLA_DV), lambda bi, c: (bi, 0, 0, 0))
    in_specs = [
        pl.BlockSpec((1, tc, GLA_QK_W), lambda bi, c: (bi, c, 0)),
        pl.BlockSpec((1, tc, GLA_QK_W), lambda bi, c: (bi, c, 1)),
        pl.BlockSpec((1, tc, GLA_V_W), lambda bi, c: (bi, c, 1)),
        pl.BlockSpec((1, tc, GLA_V_W), lambda bi, c: (bi, c, 2)),
        pl.BlockSpec((1, tc, GLA_QK_W), lambda bi, c: (bi, c, 0)),
        pl.BlockSpec((1, GLA_V_W), lambda bi, c: (0, 0)),
    ]
    args = [proj3, proj3, proj3, proj3, loga3, g_norm]
    if s0 is not None:
        in_specs.append(state)
        args.append(s0)
    return pl.pallas_call(
        functools.partial(_gla_kernel, chunk=chunk, nsub=nsub, has_s0=s0 is not None),
        grid=(b, t // tc),
        in_specs=in_specs,
        out_specs=[pl.BlockSpec((1, tc, GLA_V_W), lambda bi, c: (bi, c, 0)), state],
        out_shape=[jax.ShapeDtypeStruct((b, t, GLA_V_W), BF16),
                   jax.ShapeDtypeStruct((b, GLA_HEADS, GLA_DK, GLA_DV), F32)],
        scratch_shapes=[pltpu.VMEM((GLA_HEADS, GLA_DK, GLA_DV), F32),
                        pltpu.VMEM((tc, GLA_QK_W), BF16),
                        pltpu.VMEM((tc, GLA_V_W), F32),
                        pltpu.VMEM((nsub, GLA_HEADS, GLA_DK, GLA_DV), F32),
                        pltpu.VMEM((nsub, GLA_HEADS, GLA_DK, GLA_DK), F32)],
        compiler_params=_params(("parallel", "arbitrary")),
        name="gla",
    )(*args)


SWA_SUPER = 2048
SWA_TILES = 8
QUAD = 4


def _swa_prompt_kernel(q_ref, k_ref, v_ref, o_ref, x4, q1, q4, q16, k1, k4, k16, v1, v4, v16, ob, lb, bias):
    n = SWA_BLOCK
    s_tok = q_ref.shape[0]
    l4, l16 = s_tok // QUAD, s_tok // (QUAD * QUAD)
    head, has_prev = pl.program_id(1), pl.program_id(2) > 0
    scale = SWA_HD ** -0.5

    slope = ALIBI_SLOPES[0]
    for h in range(1, SWA_HEADS):
        slope = jnp.where(head == h, ALIBI_SLOPES[h], slope)
    qi = lax.broadcasted_iota(jnp.int32, (n, 2 * n), 0)
    kj = lax.broadcasted_iota(jnp.int32, (n, 2 * n), 1)
    steps = qi + n - kj
    visible = (steps >= 0) & (steps <= n)
    first_visible = visible & (has_prev | (kj >= n))
    for g, (_, dil) in enumerate(SWA_PATTERNS):
        alibi = (-float(dil) * slope) * steps.astype(F32)
        bias[2 * g] = jnp.where(visible, alibi, NEG)
        bias[2 * g + 1] = jnp.where(first_visible, alibi, NEG)

    for buf, length in ((k1, s_tok), (v1, s_tok), (k4, l4), (v4, l4), (k16, l16), (v16, l16)):
        lead = buf.shape[:-2]
        idx = (slice(None),) * len(lead)

        @pl.when(has_prev)
        def _():
            buf[idx + (slice(0, n),)] = buf[idx + (slice(length, length + n),)]

        @pl.when(jnp.logical_not(has_prev))
        def _():
            buf[idx + (slice(0, n),)] = jnp.zeros(lead + (n, SWA_HD), BF16)

    for t, src in enumerate((q_ref, k_ref, v_ref)):
        for r in range(QUAD):
            x4[t, r] = src[pl.ds(r, l4, stride=QUAD), :]
    q1[...] = q_ref[...].astype(BF16)
    k1[n:, :] = k_ref[...].astype(BF16)
    v1[n:, :] = v_ref[...].astype(BF16)
    for r in range(QUAD):
        q4[r] = x4[0, r].astype(BF16)
        k4[r, n:, :] = x4[1, r].astype(BF16)
        v4[r, n:, :] = x4[2, r].astype(BF16)
    for r in range(QUAD * QUAD):
        sub = pl.ds(r // QUAD, l16, stride=QUAD)
        q16[r] = x4[0, r % QUAD, sub, :].astype(BF16)
        k16[r, n:, :] = x4[1, r % QUAD, sub, :].astype(BF16)
        v16[r, n:, :] = x4[2, r % QUAD, sub, :].astype(BF16)

    def tile(q, keys, values, table):
        s = _dot_nt(q, keys) * scale + table
        m = jnp.max(s, axis=1, keepdims=True)
        p = jnp.exp(s - m)
        l = jnp.sum(p, axis=1, keepdims=True)
        acc = _dot(p.astype(BF16), values)
        return acc * (1.0 / l), jnp.broadcast_to(m + jnp.log(l), (n, SWA_HD))

    def d1_body(c, carry):
        for a in range(SWA_TILES):
            i = c * SWA_TILES + a
            row0 = pl.multiple_of(i * n, n)
            o, lse = tile(q1[pl.ds(row0, n), :], k1[pl.ds(row0, 2 * n), :], v1[pl.ds(row0, 2 * n), :],
                          bias[jnp.where(i == 0, 1, 0)])
            ob[0, pl.ds(row0, n), :] = o
            lb[0, pl.ds(row0, n), :] = lse
        return carry

    lax.fori_loop(0, s_tok // n // SWA_TILES, d1_body, 0)

    per4 = l4 // n

    def d4_body(c, carry):
        for a in range(SWA_TILES // per4):
            r = c * (SWA_TILES // per4) + a
            for i in range(per4):
                o, lse = tile(q4[r, i * n:(i + 1) * n, :], k4[r, i * n:(i + 2) * n, :], v4[r, i * n:(i + 2) * n, :],
                              bias[2 + (1 if i == 0 else 0)])
                row0 = pl.multiple_of(r * l4 + i * n, n)
                ob[1, pl.ds(row0, n), :] = o
                lb[1, pl.ds(row0, n), :] = lse
        return carry

    lax.fori_loop(0, QUAD * per4 // SWA_TILES, d4_body, 0)

    per16 = l16 // n

    def d16_body(c, carry):
        for a in range(SWA_TILES // per16):
            r4 = a % QUAD
            j = c * (SWA_TILES // per16 // QUAD) + a // QUAD
            r = r4 + QUAD * j
            for i in range(per16):
                o, lse = tile(q16[r, i * n:(i + 1) * n, :], k16[r, i * n:(i + 2) * n, :],
                              v16[r, i * n:(i + 2) * n, :], bias[4 + (1 if i == 0 else 0)])
                rows = pl.ds(r4 * l4 + QUAD * i * n + j, n, stride=QUAD)
                ob[2, rows, :] = o
                lb[2, rows, :] = lse
        return carry

    lax.fori_loop(0, QUAD * QUAD * per16 // SWA_TILES, d16_body, 0)

    def combine_body(c, carry):
        for r in range(QUAD):
            tok = pl.ds(r + QUAD * c * n, n, stride=QUAD)
            cls = pl.ds(pl.multiple_of(r * l4 + c * n, n), n)
            lses = [lb[0, tok, :], lb[1, cls, :], lb[2, cls, :]]
            outs = [ob[0, tok, :], ob[1, cls, :], ob[2, cls, :]]
            m = jnp.maximum(jnp.maximum(lses[0], lses[1]), lses[2])
            es = [jnp.exp(x - m) for x in lses]
            num = es[0] * outs[0] + es[1] * outs[1] + es[2] * outs[2]
            ob[0, tok, :] = num * (1.0 / (es[0] + es[1] + es[2]))
        return carry

    lax.fori_loop(0, l4 // n, combine_body, 0)
    o_ref[...] = ob[0].astype(o_ref.dtype)


def _swa_prompt(qkv, b, t):
    n, s_tok = SWA_BLOCK, SWA_SUPER
    assert t % s_tok == 0 and [d for _, d in SWA_PATTERNS] == [1, QUAD, QUAD * QUAD]
    assert all(w // d == n for w, d in SWA_PATTERNS)
    l4, l16 = s_tok // QUAD, s_tok // (QUAD * QUAD)
    assert (s_tok // n) % SWA_TILES == 0 and SWA_TILES % (l4 // n) == 0 and SWA_TILES % (QUAD * l16 // n) == 0
    view = qkv.reshape(3, SWA_HEADS, b, t, SWA_HD)

    def part(idx):
        return pl.BlockSpec((None, None, None, s_tok, SWA_HD), lambda bi, h, sb: (idx, h, bi, sb, 0))

    def rows(*lead):
        return pltpu.VMEM(lead + (SWA_HD,), BF16)

    o = pl.pallas_call(
        _swa_prompt_kernel,
        grid=(b, SWA_HEADS, t // s_tok),
        in_specs=[part(0), part(1), part(2)],
        out_specs=pl.BlockSpec((None, s_tok, SWA_HD), lambda bi, h, sb: (bi, sb, h)),
        out_shape=jax.ShapeDtypeStruct((b, t, SWA_W), BF16),
        scratch_shapes=[
            pltpu.VMEM((3, QUAD, l4, SWA_HD), F32),
            rows(s_tok), rows(QUAD, l4), rows(QUAD * QUAD, l16),
            rows(n + s_tok), rows(QUAD, n + l4), rows(QUAD * QUAD, n + l16),
            rows(n + s_tok), rows(QUAD, n + l4), rows(QUAD * QUAD, n + l16),
            pltpu.VMEM((3, s_tok, SWA_HD), F32), pltpu.VMEM((3, s_tok, SWA_HD), F32),
            pltpu.VMEM((2 * len(SWA_PATTERNS), n, 2 * n), F32),
        ],
        compiler_params=_params(("parallel", "parallel", "arbitrary")),
        name="swa_prompt",
    )(view, view, view)
    return o.reshape(b * t, SWA_W)


def _branch_multiplicity(d):
    mult = jnp.zeros(d.shape, F32)
    for window, dil in SWA_PATTERNS:
        hit = (d >= 0) & (d <= window) & ((d & (dil - 1)) == 0)
        mult = mult + jnp.where(hit, 1.0, 0.0)
    return mult


def _swa_sample_kernel(q_ref, kn_ref, vn_ref, kc_ref, vc_ref, o_ref, *, w, tn):
    tp = q_ref.shape[1]

    d_c = (w + lax.broadcasted_iota(jnp.int32, (tp, w), 0)
           - lax.broadcasted_iota(jnp.int32, (tp, w), 1))
    mult_c = _branch_multiplicity(d_c)
    i_n = lax.broadcasted_iota(jnp.int32, (tp, tp), 0)
    j_n = lax.broadcasted_iota(jnp.int32, (tp, tp), 1)
    d_n = jnp.where(j_n < tn, i_n - j_n, -1)
    mult_n = _branch_multiplicity(d_n)
    dist_c = d_c.astype(F32)
    dist_n = d_n.astype(F32)
    scale = SWA_HD ** -0.5

    for h in range(SWA_HEADS):
        hs = slice(h * SWA_HD, (h + 1) * SWA_HD)
        head_rows = pl.ds(h, w, stride=SWA_HEADS)
        slope = ALIBI_SLOPES[h]
        q = q_ref[0, :, hs]
        s_c = _dot_nt(q, kc_ref[0, head_rows, :].astype(BF16)) * scale - slope * dist_c
        s_n = _dot_nt(q, kn_ref[0, :, hs]) * scale - slope * dist_n
        s_c = jnp.where(mult_c > 0.0, s_c, NEG)
        s_n = jnp.where(mult_n > 0.0, s_n, NEG)
        m = jnp.maximum(jnp.max(s_c, axis=1, keepdims=True), jnp.max(s_n, axis=1, keepdims=True))
        p_c = mult_c * jnp.exp(s_c - m)
        p_n = mult_n * jnp.exp(s_n - m)
        l = jnp.sum(p_c, axis=1, keepdims=True) + jnp.sum(p_n, axis=1, keepdims=True)
        acc = (_dot(p_c.astype(BF16), vc_ref[0, head_rows, :].astype(BF16))
               + _dot(p_n.astype(BF16), vn_ref[0, :, hs]))
        o_ref[0, :, hs] = (acc / l).astype(o_ref.dtype)


def _swa_sample(projp, cache_k, cache_v, *, tn):
    b, tp, _ = projp.shape
    rows = cache_k.shape[1]
    w = rows // SWA_HEADS

    def new(cb):
        return pl.BlockSpec((1, tp, SWA_W), lambda bi: (bi, 0, cb))

    cache = pl.BlockSpec((1, rows, SWA_HD), lambda bi: (bi, 0, 0))
    return pl.pallas_call(
        functools.partial(_swa_sample_kernel, w=w, tn=tn),
        grid=(b,),
        in_specs=[new(0), new(1), new(2), cache, cache],
        out_specs=pl.BlockSpec((1, tp, SWA_W), lambda bi: (bi, 0, 0)),
        out_shape=jax.ShapeDtypeStruct((b, tp, SWA_W), BF16),
        compiler_params=_params(("parallel",)),
        name="swa_sample",
    )(projp, projp, projp, cache_k, cache_v)


def _outproj_kernel(x_ref, oa_ref, ob_ref, wa_ref, wb_ref, h_ref):
    h_ref[...] = x_ref[...] + _dot(oa_ref[...], wa_ref[...]) + _dot(ob_ref[...], wb_ref[...])


def _outproj(x, o_a, o_b, w_out, *, tm):
    m, d = x.shape
    return pl.pallas_call(
        _outproj_kernel,
        grid=(m // tm,),
        in_specs=[pl.BlockSpec((tm, d), lambda i: (i, 0)),
                  pl.BlockSpec((tm, GLA_V_W), lambda i: (i, 0)),
                  pl.BlockSpec((tm, SWA_W), lambda i: (i, 0)),
                  pl.BlockSpec((GLA_V_W, d), lambda i: (0, 0)),
                  pl.BlockSpec((SWA_W, d), lambda i: (GLA_V_W // SWA_W, 0))],
        out_specs=pl.BlockSpec((tm, d), lambda i: (i, 0)),
        out_shape=jax.ShapeDtypeStruct((m, d), F32),
        compiler_params=_params(("parallel",)),
        name="outproj",
    )(x, o_a, o_b, w_out, w_out)


FFN_HALO = 16


def _ffn_prompt_kernel(h_ref, halo_ref, g_ref, wa_ref, wv_ref, ca_ref, cv_ref, ba_ref, bv_ref, wd_ref,
                       gf_ref, kc_hbm, vc_hbm, kn_ref, vn_ref, y_ref, ua_ref, uv_ref, ko_hbm, vo_hbm,
                       xn_scr, ea_scr, ev_scr, ks_scr, vs_scr, sems, *, tiles_per_seq):
    i, j = pl.program_id(0), pl.program_id(1)
    tm = h_ref.shape[0]
    last_j = pl.num_programs(1) - 1

    nseq, rows = kc_hbm.shape[0], kc_hbm.shape[1]
    new = kn_ref.shape[1]
    old = rows - new
    moving = i < nseq
    loads = [pltpu.make_async_copy(kc_hbm.at[i, pl.ds(new, old)], ks_scr, sems.at[0]),
             pltpu.make_async_copy(vc_hbm.at[i, pl.ds(new, old)], vs_scr, sems.at[1])]
    stores = [pltpu.make_async_copy(ks_scr, ko_hbm.at[i, pl.ds(0, old)], sems.at[2]),
              pltpu.make_async_copy(vs_scr, vo_hbm.at[i, pl.ds(0, old)], sems.at[3]),
              pltpu.make_async_copy(kn_ref.at[0], ko_hbm.at[i, pl.ds(old, new)], sems.at[4]),
              pltpu.make_async_copy(vn_ref.at[0], vo_hbm.at[i, pl.ds(old, new)], sems.at[5])]

    @pl.when(moving & (j == 0))
    def _():
        for cp in loads:
            cp.start()

    @pl.when(moving & (j == last_j // 2))
    def _():
        for cp in loads:
            cp.wait()
        for cp in stores:
            cp.start()

    @pl.when(j == 0)
    def _():
        first = (i % tiles_per_seq) == 0
        halo = _rms_rows(halo_ref[...], g_ref[...])
        xn_scr[0:FFN_HALO, :] = jnp.where(first, 0.0, halo).astype(BF16)
        xn_scr[FFN_HALO:, :] = _rms_rows(h_ref[...], g_ref[...]).astype(BF16)
        y_ref[...] = h_ref[...]

    xn = xn_scr[...]
    ea_scr[...] = _dot(xn, wa_ref[...])
    ev_scr[...] = _dot(xn, wv_ref[...])

    def conv(e_scr, c_ref, b_ref):
        out = b_ref[...]
        for tap in range(CONV_W):
            off = FFN_HALO - (CONV_W - 1) + tap
            out = out + c_ref[tap:tap + 1, :] * e_scr[off:off + tm, :]
        return out

    gate = (_silu(conv(ea_scr, ca_ref, ba_ref)) * conv(ev_scr, cv_ref, bv_ref)).astype(BF16)
    y_ref[...] += _dot(gate, wd_ref[...])
    ua_ref[0] = ea_scr[FFN_HALO + tm - 8:FFN_HALO + tm, :]
    uv_ref[0] = ev_scr[FFN_HALO + tm - 8:FFN_HALO + tm, :]

    @pl.when(j == last_j)
    def _():
        y_ref[...] = _rms_rows(y_ref[...], gf_ref[...])

    @pl.when(moving & (j == last_j))
    def _():
        for cp in stores:
            cp.wait()


def _ffn_prompt(h, g_ffn, w_up, w_conv, b_conv, w_down, g_final, cache_k, cache_v, k_new, v_new, *, seq, tm, tf):
    m, d = h.shape
    f = w_down.shape[0]
    nf = f // tf
    ntile = m // tm
    hb = tm // FFN_HALO
    nseq, rows, _ = cache_k.shape
    new = k_new.shape[1]
    assert nseq <= ntile and nf >= 2, "one cache per row tile, loaded and stored within its column steps"
    hbm = pl.BlockSpec(memory_space=pl.ANY)
    fresh = pl.BlockSpec((1, new, SWA_HD), lambda i, j: (jnp.minimum(i, nseq - 1), 0, 0))
    y, ua, uv, k_all, v_all = pl.pallas_call(
        functools.partial(_ffn_prompt_kernel, tiles_per_seq=seq // tm),
        grid=(ntile, nf),
        in_specs=[
            pl.BlockSpec((tm, d), lambda i, j: (i, 0)),
            pl.BlockSpec((FFN_HALO, d), lambda i, j: (jnp.maximum(i * hb - 1, 0), 0)),
            pl.BlockSpec((1, d), lambda i, j: (0, 0)),
            pl.BlockSpec((d, tf), lambda i, j: (0, j)),
            pl.BlockSpec((d, tf), lambda i, j: (0, nf + j)),
            pl.BlockSpec((CONV_W, tf), lambda i, j: (0, j)),
            pl.BlockSpec((CONV_W, tf), lambda i, j: (0, nf + j)),
            pl.BlockSpec((1, tf), lambda i, j: (0, j)),
            pl.BlockSpec((1, tf), lambda i, j: (0, nf + j)),
            pl.BlockSpec((tf, d), lambda i, j: (j, 0)),
            pl.BlockSpec((1, d), lambda i, j: (0, 0)),
            hbm, hbm, fresh, fresh,
        ],
        out_specs=[
            pl.BlockSpec((tm, d), lambda i, j: (i, 0)),
            pl.BlockSpec((1, 8, tf), lambda i, j: (i, 0, j)),
            pl.BlockSpec((1, 8, tf), lambda i, j: (i, 0, j)),
            hbm, hbm,
        ],
        out_shape=[jax.ShapeDtypeStruct((m, d), F32),
                   jax.ShapeDtypeStruct((ntile, 8, f), F32),
                   jax.ShapeDtypeStruct((ntile, 8, f), F32),
                   jax.ShapeDtypeStruct(cache_k.shape, cache_k.dtype),
                   jax.ShapeDtypeStruct(cache_v.shape, cache_v.dtype)],
        scratch_shapes=[pltpu.VMEM((tm + FFN_HALO, d), BF16),
                        pltpu.VMEM((tm + FFN_HALO, tf), F32),
                        pltpu.VMEM((tm + FFN_HALO, tf), F32),
                        pltpu.VMEM((rows - new, SWA_HD), F32),
                        pltpu.VMEM((rows - new, SWA_HD), F32),
                        pltpu.SemaphoreType.DMA((6,))],
        compiler_params=_params(("arbitrary", "arbitrary")),
        name="convffn_prompt",
    )(h, h, g_ffn, w_up, w_up, w_conv, w_conv, b_conv, b_conv, w_down, g_final, cache_k, cache_v, k_new, v_new)
    return y, ua, uv, k_all, v_all


def _ffn_sample_kernel(h_ref, g_ref, sa_ref, sv_ref, wa_ref, wv_ref, ca_ref, cv_ref, ba_ref, bv_ref, wd_ref,
                       gf_ref, y_ref, na_ref, nv_ref, xn_scr, acc_scr, *, nb):
    j = pl.program_id(0)
    m = h_ref.shape[0]
    keep = (CONV_W - 1) * nb

    @pl.when(j == 0)
    def _():
        xn_scr[...] = _rms_rows(h_ref[...], g_ref[...]).astype(BF16)
        acc_scr[...] = jnp.zeros_like(acc_scr)

    xn = xn_scr[...]

    def conv(s_ref, w_ref, c_ref, b_ref, n_ref):
        u = _dot(xn, w_ref[...])
        ext = jnp.concatenate([s_ref[...], u], axis=0)
        n_ref[...] = ext[m:m + keep, :]
        out = b_ref[...]
        for tap in range(CONV_W):
            out = out + c_ref[tap:tap + 1, :] * ext[tap * nb:tap * nb + m, :]
        return out

    gate = (_silu(conv(sa_ref, wa_ref, ca_ref, ba_ref, na_ref))
            * conv(sv_ref, wv_ref, cv_ref, bv_ref, nv_ref)).astype(BF16)
    acc_scr[...] += _dot(gate, wd_ref[...])

    @pl.when(j == pl.num_programs(0) - 1)
    def _():
        y_ref[...] = _rms_rows(h_ref[...] + acc_scr[...], gf_ref[...])


def _ffn_sample(h_tm, state_tm, g_ffn, w_up, w_conv, b_conv, w_down, g_final, *, nb, tf):
    m, d = h_tm.shape
    f = w_down.shape[0]
    nf = f // tf
    keep = (CONV_W - 1) * nb
    return pl.pallas_call(
        functools.partial(_ffn_sample_kernel, nb=nb),
        grid=(nf,),
        in_specs=[
            pl.BlockSpec((m, d), lambda j: (0, 0)),
            pl.BlockSpec((1, d), lambda j: (0, 0)),
            pl.BlockSpec((keep, tf), lambda j: (0, j)),
            pl.BlockSpec((keep, tf), lambda j: (0, nf + j)),
            pl.BlockSpec((d, tf), lambda j: (0, j)),
            pl.BlockSpec((d, tf), lambda j: (0, nf + j)),
            pl.BlockSpec((CONV_W, tf), lambda j: (0, j)),
            pl.BlockSpec((CONV_W, tf), lambda j: (0, nf + j)),
            pl.BlockSpec((1, tf), lambda j: (0, j)),
            pl.BlockSpec((1, tf), lambda j: (0, nf + j)),
            pl.BlockSpec((tf, d), lambda j: (j, 0)),
            pl.BlockSpec((1, d), lambda j: (0, 0)),
        ],
        out_specs=[
            pl.BlockSpec((m, d), lambda j: (0, 0)),
            pl.BlockSpec((keep, tf), lambda j: (0, j)),
            pl.BlockSpec((keep, tf), lambda j: (0, j)),
        ],
        out_shape=[jax.ShapeDtypeStruct((m, d), F32),
                   jax.ShapeDtypeStruct((keep, f), F32),
                   jax.ShapeDtypeStruct((keep, f), F32)],
        scratch_shapes=[pltpu.VMEM((m, d), BF16), pltpu.VMEM((m, d), F32)],
        compiler_params=_params(("arbitrary",)),
        name="convffn_sample",
    )(h_tm, g_ffn, state_tm, state_tm, w_up, w_up, w_conv, w_conv, b_conv, b_conv, w_down, g_final)


def _pack_weights(g_attn_norm, w_in, w_gate_up, b_gate, g_gla_norm, w_out, g_ffn_norm, w_up, w_conv, b_conv,
                  w_down):
    z0 = 2 * GLA_QK_W + 2 * GLA_V_W
    z1 = z0 + GLA_GATE_RANK
    w_main = jnp.concatenate([w_in[:, :z0], w_in[:, z1:]], axis=1).astype(BF16)
    w_z = jnp.pad(w_in[:, z0:z1], ((0, 0), (0, LANE - GLA_GATE_RANK))).astype(BF16)
    w_gate = jnp.pad(w_gate_up, ((0, LANE - GLA_GATE_RANK), (0, 0))).astype(BF16)
    return dict(
        g_attn=g_attn_norm[None, :], w_main=w_main, w_z=w_z, w_gate=w_gate, b_gate=b_gate[None, :],
        g_gla=g_gla_norm[None, :], w_out=w_out.astype(BF16), g_ffn=g_ffn_norm[None, :],
        w_up=w_up.astype(BF16), w_conv=w_conv, b_conv=b_conv[None, :], w_down=w_down.astype(BF16))


def _prompt_layer(x, p, g_final, sample_cache):
    b, t, d = x.shape
    m = b * t
    f = p["w_down"].shape[0]
    keep = min(SWA_MAX_WINDOW, t)
    x2 = x.reshape(m, d)
    proj, loga, qkv, k_buf, v_buf = _inproj(x2, p["g_attn"], p["w_main"], p["w_z"], p["w_gate"], p["b_gate"],
                                            tm=1024, seq=t, keep=keep)
    o_a, s_fin = _gla(proj.reshape(b, t, GLA_COLS), loga.reshape(b, t, GLA_QK_W), p["g_gla"], None,
                      chunk=GLA_CHUNK, tc=512)
    o_b = _swa_prompt(qkv, b, t)
    h = _outproj(x2, o_a.reshape(m, GLA_V_W), o_b, p["w_out"], tm=512)
    tm = 512
    y, ua, uv, k_all, v_all = _ffn_prompt(h, p["g_ffn"], p["w_up"], p["w_conv"], p["b_conv"], p["w_down"], g_final,
                                          *sample_cache, seq=t, tm=tm, tf=512)
    shape = (b, keep, SWA_HEADS, SWA_HD)
    last = jnp.concatenate([ua, uv], axis=-1).reshape(b, t // tm, 8, 2 * f)[:, -1, 8 - (CONV_W - 1):, :]
    return y.reshape(b, t, d), s_fin, k_buf.reshape(shape), v_buf.reshape(shape), last, k_all, v_all


def _sample_layer(x, s0, conv_s0, cache_k, cache_v, p, g_final):
    b, tn, d = x.shape
    m = b * tn
    tp = 8
    w = cache_k.shape[1]
    x2 = x.reshape(m, d)
    proj, loga, qkv, k_new, v_new = _inproj(x2, p["g_attn"], p["w_main"], p["w_z"], p["w_gate"], p["b_gate"],
                                            tm=m, seq=m, keep=m)
    pad = ((0, 0), (0, tp - tn), (0, 0))
    projp = jnp.pad(proj.reshape(b, tn, GLA_COLS), pad)
    logap = jnp.pad(loga.reshape(b, tn, GLA_QK_W), pad)
    o_a, s_fin = _gla(projp, logap, p["g_gla"], s0, chunk=tp, tc=tp)
    qkvp = jnp.pad(qkv.transpose(2, 0, 1, 3).reshape(b, tn, 3 * SWA_W).astype(BF16), pad)
    rows = (b, tn * SWA_HEADS, SWA_HD)
    cache = (cache_k.reshape(b, w * SWA_HEADS, SWA_HD), cache_v.reshape(b, w * SWA_HEADS, SWA_HD),
             k_new.reshape(rows), v_new.reshape(rows))
    o_b = _swa_sample(qkvp, cache[0], cache[1], tn=tn)
    h = _outproj(x2, o_a[:, :tn].reshape(m, GLA_V_W), o_b[:, :tn].reshape(m, SWA_W), p["w_out"], tm=m)
    h_tm = h.reshape(b, tn, d).transpose(1, 0, 2).reshape(m, d)
    state_tm = conv_s0.transpose(1, 0, 2).reshape((CONV_W - 1) * b, -1)
    y_tm, na, nv = _ffn_sample(h_tm, state_tm, p["g_ffn"], p["w_up"], p["w_conv"], p["b_conv"], p["w_down"],
                               g_final, nb=b, tf=512)
    y = y_tm.reshape(tn, b, d).transpose(1, 0, 2)
    conv_new = jnp.concatenate([na, nv], axis=-1).reshape(CONV_W - 1, b, -1).transpose(1, 0, 2)
    return y, s_fin, conv_new, cache


def kernel(x_prompt, x_sample, state_gla, cache_swa_k, cache_swa_v, state_ffn_conv, g_attn_norm, w_in, w_gate_up,
           b_gate, g_gla_norm, w_out, g_ffn_norm, w_up, w_conv, b_conv, w_down, g_final):
    depth = w_in.shape[0]
    assert depth == 1, "the final RMSNorm is fused into the last layer's ConvFFN kernel"
    assert cache_swa_k.shape[2] == SWA_MAX_WINDOW and x_sample.shape[1] <= 8
    p = _pack_weights(g_attn_norm[0], w_in[0], w_gate_up[0], b_gate[0], g_gla_norm[0], w_out[0], g_ffn_norm[0],
                      w_up[0], w_conv[0], b_conv[0], w_down[0])
    gf = g_final[None, :]
    ys, s2, c2, sample_cache = _sample_layer(x_sample, state_gla[0], state_ffn_conv[0], cache_swa_k[0],
                                             cache_swa_v[0], p, gf)
    yp, s1, k1, v1, c1, k2, v2 = _prompt_layer(x_prompt, p, gf, sample_cache)
    k2, v2 = k2.reshape(cache_swa_k.shape[1:]), v2.reshape(cache_swa_v.shape[1:])
    return (yp, ys, s1[None], s2[None], k1[None], k2[None], v1[None], v2[None], c1[None], c2[None])
```

```python
import functools

import jax
import jax.numpy as jnp
from jax import lax
from jax.experimental import pallas as pl
from jax.experimental.pallas import tpu as pltpu

F32 = jnp.float32
BF16 = jnp.bfloat16

GLA_HEADS = 4
GLA_DK = 128
GLA_DV = 256
GLA_GATE_RANK = 16
GLA_TAU = 16.0
GLA_CHUNK = 64
SWA_HEADS = 8
SWA_HD = 128
SWA_PATTERNS = ((128, 1), (512, 4), (2048, 16))
SWA_BLOCK = 128
SWA_MAX_WINDOW = 2048
CONV_W = 3
EPS = 1e-6
NEG = -1e30

GLA_QK_W = GLA_HEADS * GLA_DK
GLA_V_W = GLA_HEADS * GLA_DV
SWA_W = SWA_HEADS * SWA_HD
PROJ_W = 2 * GLA_QK_W + 2 * GLA_V_W + 3 * SWA_W
LANE = 128
VMEM_LIMIT = 56 * 1024 * 1024

ALIBI_SLOPES = tuple(2.0 ** (-8.0 * (h + 1) / SWA_HEADS) for h in range(SWA_HEADS))


def _params(semantics):
    return pltpu.CompilerParams(dimension_semantics=semantics, vmem_limit_bytes=VMEM_LIMIT)


def _rms_rows(x, g):
    return x * lax.rsqrt(jnp.mean(x * x, axis=-1, keepdims=True) + EPS) * g


def _silu(x):
    return x / (1.0 + jnp.exp(-x))


def _log_sigmoid(x):
    return jnp.minimum(x, 0.0) - jnp.log1p(jnp.exp(-jnp.abs(x)))


def _dot(a, b):
    return jnp.dot(a, b, preferred_element_type=F32)


def _dot_nt(a, b):
    return lax.dot_general(a, b, (((1,), (1,)), ((), ())), preferred_element_type=F32)


def _dot_tn(a, b):
    return lax.dot_general(a, b, (((0,), (0,)), ((), ())), preferred_element_type=F32)


GLA_COLS = 2 * GLA_QK_W + 2 * GLA_V_W
GLA_STEPS = GLA_COLS // SWA_W


def _inproj_kernel(x_hbm, g_ref, w_ref, wz_ref, wg_ref, bg_ref, proj_ref, loga_ref, qkv_ref, kc_ref, vc_ref,
                   x_buf, xn_ref, sem, *, tiles_per_seq, kept_tiles):
    i, j = pl.program_id(0), pl.program_id(1)
    tm = x_buf.shape[0]

    def x_copy(tile):
        return pltpu.make_async_copy(x_hbm.at[pl.ds(tile * tm, tm)], x_buf, sem.at[0])

    @pl.when(j == 0)
    def _():
        @pl.when(i == 0)
        def _():
            x_copy(0).start()

        x_copy(i).wait()
        xn_ref[...] = _rms_rows(x_buf[...], g_ref[...]).astype(BF16)

        @pl.when(i + 1 < pl.num_programs(0))
        def _():
            x_copy(i + 1).start()

        za = _dot(xn_ref[...], wz_ref[...])
        gate = _dot(za.astype(BF16), wg_ref[...]) + bg_ref[...]
        loga_ref[...] = _log_sigmoid(gate) / GLA_TAU

    @pl.when(j < GLA_STEPS)
    def _():
        proj_ref[...] = _dot(xn_ref[...], w_ref[...]).astype(BF16)

    def swa_step(cache_ref):
        res = _dot(xn_ref[...], w_ref[...])
        for h in range(SWA_HEADS):
            qkv_ref[0, h] = res[:, h * SWA_HD:(h + 1) * SWA_HD]
        if cache_ref is not None:
            @pl.when(i % tiles_per_seq >= tiles_per_seq - kept_tiles)
            def _():
                for h in range(SWA_HEADS):
                    cache_ref[pl.ds(h, tm, stride=SWA_HEADS), :] = res[:, h * SWA_HD:(h + 1) * SWA_HD]

    for step, cache_ref in enumerate((None, kc_ref, vc_ref)):
        pl.when(j == GLA_STEPS + step)(functools.partial(swa_step, cache_ref))


def _inproj(x, g, w_main, w_z, w_gate, b_gate, *, tm, seq, keep):
    m, d = x.shape
    tn = SWA_W
    tps, kt = seq // tm, keep // tm
    assert seq % tm == 0 and keep % tm == 0 and m % seq == 0

    def cache_block(i, j):
        return ((i // tps) * kt + jnp.maximum(i % tps - (tps - kt), 0), 0)

    cache_rows = (m // seq) * keep * SWA_HEADS
    return pl.pallas_call(
        functools.partial(_inproj_kernel, tiles_per_seq=tps, kept_tiles=kt),
        grid=(m // tm, PROJ_W // tn),
        in_specs=[
            pl.BlockSpec(memory_space=pl.ANY),
            pl.BlockSpec((1, d), lambda i, j: (0, 0)),
            pl.BlockSpec((d, tn), lambda i, j: (0, j)),
            pl.BlockSpec((d, LANE), lambda i, j: (0, 0)),
            pl.BlockSpec((LANE, GLA_QK_W), lambda i, j: (0, 0)),
            pl.BlockSpec((1, GLA_QK_W), lambda i, j: (0, 0)),
        ],
        out_specs=[
            pl.BlockSpec((tm, tn), lambda i, j: (i, jnp.minimum(j, GLA_STEPS - 1))),
            pl.BlockSpec((tm, GLA_QK_W), lambda i, j: (i, 0)),
            pl.BlockSpec((1, SWA_HEADS, tm, SWA_HD), lambda i, j: (jnp.maximum(j - GLA_STEPS, 0), 0, i, 0)),
            pl.BlockSpec((tm * SWA_HEADS, SWA_HD), cache_block, pipeline_mode=pl.Buffered(1)),
            pl.BlockSpec((tm * SWA_HEADS, SWA_HD), cache_block, pipeline_mode=pl.Buffered(1)),
        ],
        out_shape=[jax.ShapeDtypeStruct((m, GLA_COLS), BF16),
                   jax.ShapeDtypeStruct((m, GLA_QK_W), F32),
                   jax.ShapeDtypeStruct((3, SWA_HEADS, m, SWA_HD), F32),
                   jax.ShapeDtypeStruct((cache_rows, SWA_HD), F32),
                   jax.ShapeDtypeStruct((cache_rows, SWA_HD), F32)],
        scratch_shapes=[pltpu.VMEM((tm, d), F32), pltpu.VMEM((tm, d), BF16), pltpu.SemaphoreType.DMA((1,))],
        compiler_params=_params(("arbitrary", "arbitrary")),
        name="norm_inproj",
    )(x, g, w_main, w_z, w_gate, b_gate)


def _split3(x):
    hi = x.astype(BF16)
    r1 = x - hi.astype(F32)
    mid = r1.astype(BF16)
    lo = (r1 - mid.astype(F32)).astype(BF16)
    return hi, mid, lo


def _gla_kernel(*refs, chunk, nsub, has_s0):
    scratch = refs[-5:]
    s_scr, qd_scr, oi_scr, u_scr, dec_scr = scratch
    if has_s0:
        q_ref, k_ref, v_ref, ga_ref, la_ref, gn_ref, s0_ref, o_ref, sfin_ref = refs[:-5]
    else:
        q_ref, k_ref, v_ref, ga_ref, la_ref, gn_ref, o_ref, sfin_ref = refs[:-5]
    c = pl.program_id(1)

    @pl.when(c == 0)
    def _():
        if has_s0:
            s_scr[...] = s0_ref[0]
        else:
            s_scr[...] = jnp.zeros_like(s_scr)

    row = lax.broadcasted_iota(jnp.int32, (chunk, chunk), 0)
    col = lax.broadcasted_iota(jnp.int32, (chunk, chunk), 1)
    causal = row >= col
    tril = jnp.where(causal, 1.0, 0.0).astype(BF16)
    eye = (lax.broadcasted_iota(jnp.int32, (GLA_DK, GLA_DK), 0)
           == lax.broadcasted_iota(jnp.int32, (GLA_DK, GLA_DK), 1))

    group = min(nsub, 4)
    heads = range(GLA_HEADS)
    ks = [slice(h * GLA_DK, (h + 1) * GLA_DK) for h in heads]
    vs = [slice(h * GLA_DV, (h + 1) * GLA_DV) for h in heads]

    def local(c2, carry):
        subs = [c2 * group + a for a in range(group)]
        sls = [pl.ds(pl.multiple_of(i * chunk, chunk), chunk) for i in subs]
        pieces = [_split3(la_ref[0, sl, :]) for sl in sls]
        cums = [_dot(tril, hi) + _dot(tril, mid) + _dot(tril, lo) for hi, mid, lo in pieces]
        q_dec, k_dec, k_end, grow = [], [], [], []
        for sl, cum in zip(sls, cums):
            total = cum[chunk - 1:chunk, :]
            k = k_ref[0, sl, :].astype(F32)
            q_dec.append((q_ref[0, sl, :].astype(F32) * (GLA_DK ** -0.5) * jnp.exp(cum)).astype(BF16))
            k_dec.append((k * jnp.exp(-cum)).astype(BF16))
            k_end.append((k * jnp.exp(total - cum)).astype(BF16))
            grow.append(jnp.exp(total))
            qd_scr[sl, :] = q_dec[-1]
        units = [(a, h) for a in range(group) for h in heads]
        vals = {(a, h): v_ref[0, sls[a], vs[h]] for a, h in units}
        raw = {(a, h): _dot_nt(q_dec[a][:, ks[h]], k_dec[a][:, ks[h]]) for a, h in units}
        for a, h in units:
            u_scr[subs[a], h] = _dot_tn(k_end[a][:, ks[h]], vals[a, h])
        attn = {(a, h): jnp.where(causal, raw[a, h], 0.0).astype(BF16) for a, h in units}
        for a, h in units:
            oi_scr[sls[a], vs[h]] = _dot(attn[a, h], vals[a, h])
        for a, h in units:
            decay = jnp.sum(jnp.where(eye, grow[a][:, ks[h]], 0.0), axis=1, keepdims=True)
            dec_scr[subs[a], h] = jnp.broadcast_to(decay, (GLA_DK, GLA_DK))
        return carry

    if nsub == group:
        local(0, 0)
    else:
        lax.fori_loop(0, nsub // group, local, 0)

    for i in range(nsub):
        sl = slice(i * chunk, (i + 1) * chunk)
        states = [s_scr[h] for h in heads]
        inter = [_dot(qd_scr[sl, ks[h]], states[h].astype(BF16)) for h in heads]
        for h in heads:
            decay = dec_scr[i, h]
            s_scr[h] = states[h] * jnp.concatenate([decay] * (GLA_DV // GLA_DK), axis=1) + u_scr[i, h]
        for h in heads:
            on = _rms_rows(oi_scr[sl, vs[h]] + inter[h], gn_ref[:, vs[h]])
            o_ref[0, sl, vs[h]] = (on * _silu(ga_ref[0, sl, vs[h]].astype(F32))).astype(o_ref.dtype)

    @pl.when(c == pl.num_programs(1) - 1)
    def _():
        sfin_ref[0] = s_scr[...]


def _gla(proj3, loga3, g_norm, s0, *, chunk, tc):
    b, t, _ = proj3.shape
    nsub = tc // chunk
    state = pl.BlockSpec((1, GLA_HEADS, GLA_DK, GLA_DV), lambda bi, c: (bi, 0, 0, 0))
    in_specs = [
        pl.BlockSpec((1, tc, GLA_QK_W), lambda bi, c: (bi, c, 0)),
        pl.BlockSpec((1, tc, GLA_QK_W), lambda bi, c: (bi, c, 1)),
        pl.BlockSpec((1, tc, GLA_V_W), lambda bi, c: (bi, c, 1)),
        pl.BlockSpec((1, tc, GLA_V_W), lambda bi, c: (bi, c, 2)),
        pl.BlockSpec((1, tc, GLA_QK_W), lambda bi, c: (bi, c, 0)),
        pl.BlockSpec((1, GLA_V_W), lambda bi, c: (0, 0)),
    ]
    args = [proj3, proj3, proj3, proj3, loga3, g_norm]
    if s0 is not None:
        in_specs.append(state)
        args.append(s0)
    return pl.pallas_call(
        functools.partial(_gla_kernel, chunk=chunk, nsub=nsub, has_s0=s0 is not None),
        grid=(b, t // tc),
        in_specs=in_specs,
        out_specs=[pl.BlockSpec((1, tc, GLA_V_W), lambda bi, c: (bi, c, 0)), state],
        out_shape=[jax.ShapeDtypeStruct((b, t, GLA_V_W), BF16),
                   jax.ShapeDtypeStruct((b, GLA_HEADS, GLA_DK, GLA_DV), F32)],
        scratch_shapes=[pltpu.VMEM((GLA_HEADS, GLA_DK, GLA_DV), F32),
                        pltpu.VMEM((tc, GLA_QK_W), BF16),
                        pltpu.VMEM((tc, GLA_V_W), F32),
                        pltpu.VMEM((nsub, GLA_HEADS, GLA_DK, GLA_DV), F32),
                        pltpu.VMEM((nsub, GLA_HEADS, GLA_DK, GLA_DK), F32)],
        compiler_params=_params(("parallel", "arbitrary")),
        name="gla",
    )(*args)


SWA_SUPER = 2048
SWA_TILES = 16
QUAD = 4


def _swa_prompt_kernel(q_ref, k_ref, v_ref, o_ref, x4, q1, q4, q16, k1, k4, k16, v1, v4, v16, ob, lb, bias):
    n = SWA_BLOCK
    s_tok = q_ref.shape[0]
    l4, l16 = s_tok // QUAD, s_tok // (QUAD * QUAD)
    head, has_prev = pl.program_id(1), pl.program_id(2) > 0
    scale = SWA_HD ** -0.5

    slope = ALIBI_SLOPES[0]
    for h in range(1, SWA_HEADS):
        slope = jnp.where(head == h, ALIBI_SLOPES[h], slope)
    qi = lax.broadcasted_iota(jnp.int32, (n, 2 * n), 0)
    kj = lax.broadcasted_iota(jnp.int32, (n, 2 * n), 1)
    steps = qi + n - kj
    visible = (steps >= 0) & (steps <= n)
    first_visible = visible & (has_prev | (kj >= n))
    for g, (_, dil) in enumerate(SWA_PATTERNS):
        alibi = (-float(dil) * slope) * steps.astype(F32)
        bias[2 * g] = jnp.where(visible, alibi, NEG)
        bias[2 * g + 1] = jnp.where(first_visible, alibi, NEG)

    for buf, length in ((k1, s_tok), (v1, s_tok), (k4, l4), (v4, l4), (k16, l16), (v16, l16)):
        lead = buf.shape[:-2]
        idx = (slice(None),) * len(lead)

        @pl.when(has_prev)
        def _():
            buf[idx + (slice(0, n),)] = buf[idx + (slice(length, length + n),)]

        @pl.when(jnp.logical_not(has_prev))
        def _():
            buf[idx + (slice(0, n),)] = jnp.zeros(lead + (n, SWA_HD), BF16)

    for t, src in enumerate((q_ref, k_ref, v_ref)):
        for r in range(QUAD):
            x4[t, r] = src[pl.ds(r, l4, stride=QUAD), :]
    q1[...] = q_ref[...].astype(BF16)
    k1[n:, :] = k_ref[...].astype(BF16)
    v1[n:, :] = v_ref[...].astype(BF16)
    for r in range(QUAD):
        q4[r] = x4[0, r].astype(BF16)
        k4[r, n:, :] = x4[1, r].astype(BF16)
        v4[r, n:, :] = x4[2, r].astype(BF16)
    for r in range(QUAD * QUAD):
        sub = pl.ds(r // QUAD, l16, stride=QUAD)
        q16[r] = x4[0, r % QUAD, sub, :].astype(BF16)
        k16[r, n:, :] = x4[1, r % QUAD, sub, :].astype(BF16)
        v16[r, n:, :] = x4[2, r % QUAD, sub, :].astype(BF16)

    def tile(q, keys, values, table):
        s = _dot_nt(q, keys) * scale + table
        m = jnp.max(s, axis=1, keepdims=True)
        p = jnp.exp(s - m)
        l = jnp.sum(p, axis=1, keepdims=True)
        acc = _dot(p.astype(BF16), values)
        return acc * (1.0 / l), jnp.broadcast_to(m + jnp.log(l), (n, SWA_HD))

    def d1_body(c, carry):
        for a in range(SWA_TILES):
            i = c * SWA_TILES + a
            row0 = pl.multiple_of(i * n, n)
            o, lse = tile(q1[pl.ds(row0, n), :], k1[pl.ds(row0, 2 * n), :], v1[pl.ds(row0, 2 * n), :],
                          bias[jnp.where(i == 0, 1, 0)])
            ob[0, pl.ds(row0, n), :] = o
            lb[0, pl.ds(row0, n), :] = lse
        return carry

    lax.fori_loop(0, s_tok // n // SWA_TILES, d1_body, 0)

    per4 = l4 // n

    def d4_body(c, carry):
        for a in range(SWA_TILES // per4):
            r = c * (SWA_TILES // per4) + a
            for i in range(per4):
                o, lse = tile(q4[r, i * n:(i + 1) * n, :], k4[r, i * n:(i + 2) * n, :], v4[r, i * n:(i + 2) * n, :],
                              bias[2 + (1 if i == 0 else 0)])
                row0 = pl.multiple_of(r * l4 + i * n, n)
                ob[1, pl.ds(row0, n), :] = o
                lb[1, pl.ds(row0, n), :] = lse
        return carry

    lax.fori_loop(0, QUAD * per4 // SWA_TILES, d4_body, 0)

    per16 = l16 // n

    def d16_body(c, carry):
        for a in range(SWA_TILES // per16):
            r4 = a % QUAD
            j = c * (SWA_TILES // per16 // QUAD) + a // QUAD
            r = r4 + QUAD * j
            for i in range(per16):
                o, lse = tile(q16[r, i * n:(i + 1) * n, :], k16[r, i * n:(i + 2) * n, :],
                              v16[r, i * n:(i + 2) * n, :], bias[4 + (1 if i == 0 else 0)])
                rows = pl.ds(r4 * l4 + QUAD * i * n + j, n, stride=QUAD)
                ob[2, rows, :] = o
                lb[2, rows, :] = lse
        return carry

    lax.fori_loop(0, QUAD * QUAD * per16 // SWA_TILES, d16_body, 0)

    def combine_body(c, carry):
        for r in range(QUAD):
            tok = pl.ds(r + QUAD * c * n, n, stride=QUAD)
            cls = pl.ds(pl.multiple_of(r * l4 + c * n, n), n)
            lses = [lb[0, tok, :], lb[1, cls, :], lb[2, cls, :]]
            outs = [ob[0, tok, :], ob[1, cls, :], ob[2, cls, :]]
            m = jnp.maximum(jnp.maximum(lses[0], lses[1]), lses[2])
            es = [jnp.exp(x - m) for x in lses]
            num = es[0] * outs[0] + es[1] * outs[1] + es[2] * outs[2]
            ob[0, tok, :] = num * (1.0 / (es[0] + es[1] + es[2]))
        return carry

    lax.fori_loop(0, l4 // n, combine_body, 0)
    o_ref[...] = ob[0].astype(o_ref.dtype)


def _swa_prompt(qkv, b, t):
    n, s_tok = SWA_BLOCK, SWA_SUPER
    assert t % s_tok == 0 and [d for _, d in SWA_PATTERNS] == [1, QUAD, QUAD * QUAD]
    assert all(w // d == n for w, d in SWA_PATTERNS)
    l4, l16 = s_tok // QUAD, s_tok // (QUAD * QUAD)
    assert (s_tok // n) % SWA_TILES == 0 and SWA_TILES % (l4 // n) == 0 and SWA_TILES % (QUAD * l16 // n) == 0
    view = qkv.reshape(3, SWA_HEADS, b, t, SWA_HD)

    def part(idx):
        return pl.BlockSpec((None, None, None, s_tok, SWA_HD), lambda bi, h, sb: (idx, h, bi, sb, 0))

    def rows(*lead):
        return pltpu.VMEM(lead + (SWA_HD,), BF16)

    o = pl.pallas_call(
        _swa_prompt_kernel,
        grid=(b, SWA_HEADS, t // s_tok),
        in_specs=[part(0), part(1), part(2)],
        out_specs=pl.BlockSpec((None, s_tok, SWA_HD), lambda bi, h, sb: (bi, sb, h)),
        out_shape=jax.ShapeDtypeStruct((b, t, SWA_W), BF16),
        scratch_shapes=[
            pltpu.VMEM((3, QUAD, l4, SWA_HD), F32),
            rows(s_tok), rows(QUAD, l4), rows(QUAD * QUAD, l16),
            rows(n + s_tok), rows(QUAD, n + l4), rows(QUAD * QUAD, n + l16),
            rows(n + s_tok), rows(QUAD, n + l4), rows(QUAD * QUAD, n + l16),
            pltpu.VMEM((3, s_tok, SWA_HD), F32), pltpu.VMEM((3, s_tok, SWA_HD), F32),
            pltpu.VMEM((2 * len(SWA_PATTERNS), n, 2 * n), F32),
        ],
        compiler_params=_params(("parallel", "parallel", "arbitrary")),
        name="swa_prompt",
    )(view, view, view)
    return o.reshape(b * t, SWA_W)


def _branch_multiplicity(d):
    mult = jnp.zeros(d.shape, F32)
    for window, dil in SWA_PATTERNS:
        hit = (d >= 0) & (d <= window) & ((d & (dil - 1)) == 0)
        mult = mult + jnp.where(hit, 1.0, 0.0)
    return mult


SAMPLE_NEAR = SWA_PATTERNS[1][0]
SAMPLE_GROUP = SWA_PATTERNS[2][1]


def _swa_sample_kernel(q_ref, kn_ref, vn_ref, kfar_ref, vfar_ref, knear_ref, vnear_ref, o_ref, *, w, tn):
    tp = q_ref.shape[1]
    groups, per = kfar_ref.shape[1], kfar_ref.shape[2]
    n_far, n_near = groups * tn, knear_ref.shape[1] // SWA_HEADS
    kfar, vfar = kfar_ref.reshape(groups * per, SWA_HD), vfar_ref.reshape(groups * per, SWA_HD)

    def query_row(shape):
        return lax.broadcasted_iota(jnp.int32, shape, 0)

    e_far = lax.broadcasted_iota(jnp.int32, (tp, n_far), 1)
    pos_far = (e_far // tn) * SAMPLE_GROUP + e_far % tn
    pos_near = (w - n_near) + lax.broadcasted_iota(jnp.int32, (tp, n_near), 1)
    j_new = lax.broadcasted_iota(jnp.int32, (tp, tp), 1)
    dists = [w + query_row((tp, n_far)) - pos_far,
             w + query_row((tp, n_near)) - pos_near,
             jnp.where(j_new < tn, query_row((tp, tp)) - j_new, -1)]
    mults = [_branch_multiplicity(d) for d in dists]
    dists = [d.astype(F32) for d in dists]
    scale = SWA_HD ** -0.5

    for h in range(SWA_HEADS):
        hs = slice(h * SWA_HD, (h + 1) * SWA_HD)
        far_rows = pl.ds(h, n_far, stride=SWA_HEADS)
        near_rows = pl.ds(h, n_near, stride=SWA_HEADS)
        slope = ALIBI_SLOPES[h]
        q = q_ref[0, :, hs]
        keys = [kfar[far_rows, :].astype(BF16), knear_ref[0, near_rows, :].astype(BF16), kn_ref[0, :, hs]]
        values = [vfar[far_rows, :].astype(BF16), vnear_ref[0, near_rows, :].astype(BF16), vn_ref[0, :, hs]]
        scores = [jnp.where(mult > 0.0, _dot_nt(q, k) * scale - slope * d, NEG)
                  for k, d, mult in zip(keys, dists, mults)]
        m = functools.reduce(jnp.maximum, [jnp.max(s, axis=1, keepdims=True) for s in scores])
        probs = [mult * jnp.exp(s - m) for s, mult in zip(scores, mults)]
        l = functools.reduce(lambda a, b: a + b, [jnp.sum(p, axis=1, keepdims=True) for p in probs])
        acc = functools.reduce(lambda a, b: a + b, [_dot(p.astype(BF16), v) for p, v in zip(probs, values)])
        o_ref[0, :, hs] = (acc / l).astype(o_ref.dtype)


def _swa_sample(projp, cache_k, cache_v, *, tn):
    b, tp, _ = projp.shape
    rows = cache_k.shape[1]
    w = rows // SWA_HEADS
    group_rows = SAMPLE_GROUP * SWA_HEADS
    near_rows = SAMPLE_NEAR * SWA_HEADS
    assert w % SAMPLE_NEAR == 0 and SAMPLE_NEAR % SAMPLE_GROUP == 0 and tn <= SAMPLE_GROUP
    assert [wd for wd, _ in SWA_PATTERNS][:2] == [SWA_BLOCK, SAMPLE_NEAR] and w >= SWA_PATTERNS[2][0]
    far_groups = (w - SAMPLE_NEAR) // SAMPLE_GROUP
    grouped = (b, rows // group_rows, group_rows, SWA_HD)

    def new(cb):
        return pl.BlockSpec((1, tp, SWA_W), lambda bi: (bi, 0, cb))

    far = pl.BlockSpec((1, far_groups, tn * SWA_HEADS, SWA_HD), lambda bi: (bi, 0, 0, 0))
    near = pl.BlockSpec((1, near_rows, SWA_HD), lambda bi: (bi, rows // near_rows - 1, 0))
    return pl.pallas_call(
        functools.partial(_swa_sample_kernel, w=w, tn=tn),
        grid=(b,),
        in_specs=[new(0), new(1), new(2), far, far, near, near],
        out_specs=pl.BlockSpec((1, tp, SWA_W), lambda bi: (bi, 0, 0)),
        out_shape=jax.ShapeDtypeStruct((b, tp, SWA_W), BF16),
        compiler_params=_params(("parallel",)),
        name="swa_sample",
    )(projp, projp, projp, cache_k.reshape(grouped), cache_v.reshape(grouped), cache_k, cache_v)


def _outproj_kernel(x_ref, oa_ref, ob_ref, wa_ref, wb_ref, h_ref):
    h_ref[...] = x_ref[...] + _dot(oa_ref[...], wa_ref[...]) + _dot(ob_ref[...], wb_ref[...])


def _outproj(x, o_a, o_b, w_out, *, tm):
    m, d = x.shape
    return pl.pallas_call(
        _outproj_kernel,
        grid=(m // tm,),
        in_specs=[pl.BlockSpec((tm, d), lambda i: (i, 0)),
                  pl.BlockSpec((tm, GLA_V_W), lambda i: (i, 0)),
                  pl.BlockSpec((tm, SWA_W), lambda i: (i, 0)),
                  pl.BlockSpec((GLA_V_W, d), lambda i: (0, 0)),
                  pl.BlockSpec((SWA_W, d), lambda i: (GLA_V_W // SWA_W, 0))],
        out_specs=pl.BlockSpec((tm, d), lambda i: (i, 0)),
        out_shape=jax.ShapeDtypeStruct((m, d), F32),
        compiler_params=_params(("parallel",)),
        name="outproj",
    )(x, o_a, o_b, w_out, w_out)


FFN_HALO = 16
CACHE_PIECES = 4


def _ffn_prompt_kernel(h_ref, halo_ref, g_ref, wa_ref, wv_ref, ca_ref, cv_ref, ba_ref, bv_ref, wd_ref,
                       gf_ref, kc_hbm, vc_hbm, kn_ref, vn_ref, y_ref, ua_ref, uv_ref, ko_hbm, vo_hbm,
                       xn_scr, ea_scr, ev_scr, ks_scr, vs_scr, sems, *, tiles_per_seq):
    i, j = pl.program_id(0), pl.program_id(1)
    tm = h_ref.shape[0]
    last_j = pl.num_programs(1) - 1

    nseq, rows = kc_hbm.shape[0], kc_hbm.shape[1]
    new = kn_ref.shape[1]
    old = rows - new
    piece = old // CACHE_PIECES
    moving = i < nseq

    def load(src_hbm, dst_scr, c, sem):
        part = pl.ds(c * piece, piece)
        return pltpu.make_async_copy(src_hbm.at[i, pl.ds(new + c * piece, piece)], dst_scr.at[part], sem)

    def store(src_scr, dst_hbm, c, sem):
        part = pl.ds(c * piece, piece)
        return pltpu.make_async_copy(src_scr.at[part], dst_hbm.at[i, part], sem)

    loads = [[load(kc_hbm, ks_scr, c, sems.at[0]), load(vc_hbm, vs_scr, c, sems.at[1])]
             for c in range(CACHE_PIECES)]
    stores = [[store(ks_scr, ko_hbm, c, sems.at[2]), store(vs_scr, vo_hbm, c, sems.at[3])]
              for c in range(CACHE_PIECES)]
    stores[0] += [pltpu.make_async_copy(kn_ref.at[0], ko_hbm.at[i, pl.ds(old, new)], sems.at[4]),
                  pltpu.make_async_copy(vn_ref.at[0], vo_hbm.at[i, pl.ds(old, new)], sems.at[5])]

    for c in range(CACHE_PIECES):
        @pl.when(moving & (j == c))
        def _():
            for cp in loads[c]:
                cp.start(priority=1)

        @pl.when(moving & (j == CACHE_PIECES + c))
        def _():
            if c == 0:
                for group in loads:
                    for cp in group:
                        cp.wait()
            for cp in stores[c]:
                cp.start(priority=1)

    @pl.when(j == 0)
    def _():
        first = (i % tiles_per_seq) == 0
        halo = _rms_rows(halo_ref[...], g_ref[...])
        xn_scr[0:FFN_HALO, :] = jnp.where(first, 0.0, halo).astype(BF16)
        xn_scr[FFN_HALO:, :] = _rms_rows(h_ref[...], g_ref[...]).astype(BF16)
        y_ref[...] = h_ref[...]

    xn = xn_scr[...]
    ea_scr[...] = _dot(xn, wa_ref[...])
    ev_scr[...] = _dot(xn, wv_ref[...])

    def conv(e_scr, c_ref, b_ref):
        out = b_ref[...]
        for tap in range(CONV_W):
            off = FFN_HALO - (CONV_W - 1) + tap
            out = out + c_ref[tap:tap + 1, :] * e_scr[off:off + tm, :]
        return out

    gate = (_silu(conv(ea_scr, ca_ref, ba_ref)) * conv(ev_scr, cv_ref, bv_ref)).astype(BF16)
    y_ref[...] += _dot(gate, wd_ref[...])
    ua_ref[0] = ea_scr[FFN_HALO + tm - 8:FFN_HALO + tm, :]
    uv_ref[0] = ev_scr[FFN_HALO + tm - 8:FFN_HALO + tm, :]

    @pl.when(j == last_j)
    def _():
        y_ref[...] = _rms_rows(y_ref[...], gf_ref[...])

    @pl.when(moving & (j == last_j))
    def _():
        for group in stores:
            for cp in group:
                cp.wait()


def _ffn_prompt(h, g_ffn, w_up, w_conv, b_conv, w_down, g_final, cache_k, cache_v, k_new, v_new, *, seq, tm, tf):
    m, d = h.shape
    f = w_down.shape[0]
    nf = f // tf
    ntile = m // tm
    hb = tm // FFN_HALO
    nseq, rows, _ = cache_k.shape
    new = k_new.shape[1]
    assert nseq <= ntile and nf > 2 * CACHE_PIECES, "one cache per row tile, moved within its column steps"
    assert (rows - new) % (8 * CACHE_PIECES) == 0
    hbm = pl.BlockSpec(memory_space=pl.ANY)
    fresh = pl.BlockSpec((1, new, SWA_HD), lambda i, j: (jnp.minimum(i, nseq - 1), 0, 0))
    y, ua, uv, k_all, v_all = pl.pallas_call(
        functools.partial(_ffn_prompt_kernel, tiles_per_seq=seq // tm),
        grid=(ntile, nf),
        in_specs=[
            pl.BlockSpec((tm, d), lambda i, j: (i, 0)),
            pl.BlockSpec((FFN_HALO, d), lambda i, j: (jnp.maximum(i * hb - 1, 0), 0)),
            pl.BlockSpec((1, d), lambda i, j: (0, 0)),
            pl.BlockSpec((d, tf), lambda i, j: (0, j)),
            pl.BlockSpec((d, tf), lambda i, j: (0, nf + j)),
            pl.BlockSpec((CONV_W, tf), lambda i, j: (0, j)),
            pl.BlockSpec((CONV_W, tf), lambda i, j: (0, nf + j)),
            pl.BlockSpec((1, tf), lambda i, j: (0, j)),
            pl.BlockSpec((1, tf), lambda i, j: (0, nf + j)),
            pl.BlockSpec((tf, d), lambda i, j: (j, 0)),
            pl.BlockSpec((1, d), lambda i, j: (0, 0)),
            hbm, hbm, fresh, fresh,
        ],
        out_specs=[
            pl.BlockSpec((tm, d), lambda i, j: (i, 0)),
            pl.BlockSpec((1, 8, tf), lambda i, j: (i, 0, j)),
            pl.BlockSpec((1, 8, tf), lambda i, j: (i, 0, j)),
            hbm, hbm,
        ],
        out_shape=[jax.ShapeDtypeStruct((m, d), F32),
                   jax.ShapeDtypeStruct((ntile, 8, f), F32),
                   jax.ShapeDtypeStruct((ntile, 8, f), F32),
                   jax.ShapeDtypeStruct(cache_k.shape, cache_k.dtype),
                   jax.ShapeDtypeStruct(cache_v.shape, cache_v.dtype)],
        scratch_shapes=[pltpu.VMEM((tm + FFN_HALO, d), BF16),
                        pltpu.VMEM((tm + FFN_HALO, tf), F32),
                        pltpu.VMEM((tm + FFN_HALO, tf), F32),
                        pltpu.VMEM((rows - new, SWA_HD), F32),
                        pltpu.VMEM((rows - new, SWA_HD), F32),
                        pltpu.SemaphoreType.DMA((6,))],
        compiler_params=_params(("arbitrary", "arbitrary")),
        name="convffn_prompt",
    )(h, h, g_ffn, w_up, w_up, w_conv, w_conv, b_conv, b_conv, w_down, g_final, cache_k, cache_v, k_new, v_new)
    return y, ua, uv, k_all, v_all


def _ffn_sample_kernel(h_ref, g_ref, sa_ref, sv_ref, wa_ref, wv_ref, ca_ref, cv_ref, ba_ref, bv_ref, wd_ref,
                       gf_ref, y_ref, na_ref, nv_ref, xn_scr, acc_scr, *, nb):
    j = pl.program_id(0)
    m = h_ref.shape[0]
    keep = (CONV_W - 1) * nb

    @pl.when(j == 0)
    def _():
        xn_scr[...] = _rms_rows(h_ref[...], g_ref[...]).astype(BF16)
        acc_scr[...] = jnp.zeros_like(acc_scr)

    xn = xn_scr[...]

    def conv(s_ref, w_ref, c_ref, b_ref, n_ref):
        u = _dot(xn, w_ref[...])
        ext = jnp.concatenate([s_ref[...], u], axis=0)
        n_ref[...] = ext[m:m + keep, :]
        out = b_ref[...]
        for tap in range(CONV_W):
            out = out + c_ref[tap:tap + 1, :] * ext[tap * nb:tap * nb + m, :]
        return out

    gate = (_silu(conv(sa_ref, wa_ref, ca_ref, ba_ref, na_ref))
            * conv(sv_ref, wv_ref, cv_ref, bv_ref, nv_ref)).astype(BF16)
    acc_scr[...] += _dot(gate, wd_ref[...])

    @pl.when(j == pl.num_programs(0) - 1)
    def _():
        y_ref[...] = _rms_rows(h_ref[...] + acc_scr[...], gf_ref[...])


def _ffn_sample(h_tm, state_tm, g_ffn, w_up, w_conv, b_conv, w_down, g_final, *, nb, tf):
    m, d = h_tm.shape
    f = w_down.shape[0]
    nf = f // tf
    keep = (CONV_W - 1) * nb
    return pl.pallas_call(
        functools.partial(_ffn_sample_kernel, nb=nb),
        grid=(nf,),
        in_specs=[
            pl.BlockSpec((m, d), lambda j: (0, 0)),
            pl.BlockSpec((1, d), lambda j: (0, 0)),
            pl.BlockSpec((keep, tf), lambda j: (0, j)),
            pl.BlockSpec((keep, tf), lambda j: (0, nf + j)),
            pl.BlockSpec((d, tf), lambda j: (0, j)),
            pl.BlockSpec((d, tf), lambda j: (0, nf + j)),
            pl.BlockSpec((CONV_W, tf), lambda j: (0, j)),
            pl.BlockSpec((CONV_W, tf), lambda j: (0, nf + j)),
            pl.BlockSpec((1, tf), lambda j: (0, j)),
            pl.BlockSpec((1, tf), lambda j: (0, nf + j)),
            pl.BlockSpec((tf, d), lambda j: (j, 0)),
            pl.BlockSpec((1, d), lambda j: (0, 0)),
        ],
        out_specs=[
            pl.BlockSpec((m, d), lambda j: (0, 0)),
            pl.BlockSpec((keep, tf), lambda j: (0, j)),
            pl.BlockSpec((keep, tf), lambda j: (0, j)),
        ],
        out_shape=[jax.ShapeDtypeStruct((m, d), F32),
                   jax.ShapeDtypeStruct((keep, f), F32),
                   jax.ShapeDtypeStruct((keep, f), F32)],
        scratch_shapes=[pltpu.VMEM((m, d), BF16), pltpu.VMEM((m, d), F32)],
        compiler_params=_params(("arbitrary",)),
        name="convffn_sample",
    )(h_tm, g_ffn, state_tm, state_tm, w_up, w_up, w_conv, w_conv, b_conv, b_conv, w_down, g_final)


def _pack_weights(g_attn_norm, w_in, w_gate_up, b_gate, g_gla_norm, w_out, g_ffn_norm, w_up, w_conv, b_conv,
                  w_down):
    z0 = 2 * GLA_QK_W + 2 * GLA_V_W
    z1 = z0 + GLA_GATE_RANK
    w_main = jnp.concatenate([w_in[:, :z0], w_in[:, z1:]], axis=1).astype(BF16)
    w_z = jnp.pad(w_in[:, z0:z1], ((0, 0), (0, LANE - GLA_GATE_RANK))).astype(BF16)
    w_gate = jnp.pad(w_gate_up, ((0, LANE - GLA_GATE_RANK), (0, 0))).astype(BF16)
    return dict(
        g_attn=g_attn_norm[None, :], w_main=w_main, w_z=w_z, w_gate=w_gate, b_gate=b_gate[None, :],
        g_gla=g_gla_norm[None, :], w_out=w_out.astype(BF16), g_ffn=g_ffn_norm[None, :],
        w_up=w_up.astype(BF16), w_conv=w_conv, b_conv=b_conv[None, :], w_down=w_down.astype(BF16))


def _prompt_layer(x, p, g_final, sample_cache):
    b, t, d = x.shape
    m = b * t
    f = p["w_down"].shape[0]
    keep = min(SWA_MAX_WINDOW, t)
    x2 = x.reshape(m, d)
    proj, loga, qkv, k_buf, v_buf = _inproj(x2, p["g_attn"], p["w_main"], p["w_z"], p["w_gate"], p["b_gate"],
                                            tm=1024, seq=t, keep=keep)
    o_a, s_fin = _gla(proj.reshape(b, t, GLA_COLS), loga.reshape(b, t, GLA_QK_W), p["g_gla"], None,
                      chunk=GLA_CHUNK, tc=512)
    o_b = _swa_prompt(qkv, b, t)
    h = _outproj(x2, o_a.reshape(m, GLA_V_W), o_b, p["w_out"], tm=512)
    tm = 512
    y, ua, uv, k_all, v_all = _ffn_prompt(h, p["g_ffn"], p["w_up"], p["w_conv"], p["b_conv"], p["w_down"], g_final,
                                          *sample_cache, seq=t, tm=tm, tf=512)
    shape = (b, keep, SWA_HEADS, SWA_HD)
    last = jnp.concatenate([ua, uv], axis=-1).reshape(b, t // tm, 8, 2 * f)[:, -1, 8 - (CONV_W - 1):, :]
    return y.reshape(b, t, d), s_fin, k_buf.reshape(shape), v_buf.reshape(shape), last, k_all, v_all


def _sample_layer(x, s0, conv_s0, cache_k, cache_v, p, g_final):
    b, tn, d = x.shape
    m = b * tn
    tp = 8
    w = cache_k.shape[1]
    x2 = x.reshape(m, d)
    proj, loga, qkv, k_new, v_new = _inproj(x2, p["g_attn"], p["w_main"], p["w_z"], p["w_gate"], p["b_gate"],
                                            tm=m, seq=m, keep=m)
    pad = ((0, 0), (0, tp - tn), (0, 0))
    projp = jnp.pad(proj.reshape(b, tn, GLA_COLS), pad)
    logap = jnp.pad(loga.reshape(b, tn, GLA_QK_W), pad)
    o_a, s_fin = _gla(projp, logap, p["g_gla"], s0, chunk=tp, tc=tp)
    qkvp = jnp.pad(qkv.transpose(2, 0, 1, 3).reshape(b, tn, 3 * SWA_W).astype(BF16), pad)
    rows = (b, tn * SWA_HEADS, SWA_HD)
    cache = (cache_k.reshape(b, w * SWA_HEADS, SWA_HD), cache_v.reshape(b, w * SWA_HEADS, SWA_HD),
             k_new.reshape(rows), v_new.reshape(rows))
    o_b = _swa_sample(qkvp, cache[0], cache[1], tn=tn)
    h = _outproj(x2, o_a[:, :tn].reshape(m, GLA_V_W), o_b[:, :tn].reshape(m, SWA_W), p["w_out"], tm=m)
    h_tm = h.reshape(b, tn, d).transpose(1, 0, 2).reshape(m, d)
    state_tm = conv_s0.transpose(1, 0, 2).reshape((CONV_W - 1) * b, -1)
    y_tm, na, nv = _ffn_sample(h_tm, state_tm, p["g_ffn"], p["w_up"], p["w_conv"], p["b_conv"], p["w_down"],
                               g_final, nb=b, tf=512)
    y = y_tm.reshape(tn, b, d).transpose(1, 0, 2)
    conv_new = jnp.concatenate([na, nv], axis=-1).reshape(CONV_W - 1, b, -1).transpose(1, 0, 2)
    return y, s_fin, conv_new, cache


def kernel(x_prompt, x_sample, state_gla, cache_swa_k, cache_swa_v, state_ffn_conv, g_attn_norm, w_in, w_gate_up,
           b_gate, g_gla_norm, w_out, g_ffn_norm, w_up, w_conv, b_conv, w_down, g_final):
    depth = w_in.shape[0]
    assert depth == 1, "the final RMSNorm is fused into the last layer's ConvFFN kernel"
    assert cache_swa_k.shape[2] == SWA_MAX_WINDOW and x_sample.shape[1] <= 8
    p = _pack_weights(g_attn_norm[0], w_in[0], w_gate_up[0], b_gate[0], g_gla_norm[0], w_out[0], g_ffn_norm[0],
                      w_up[0], w_conv[0], b_conv[0], w_down[0])
    gf = g_final[None, :]
    ys, s2, c2, sample_cache = _sample_layer(x_sample, state_gla[0], state_ffn_conv[0], cache_swa_k[0],
                                             cache_swa_v[0], p, gf)
    yp, s1, k1, v1, c1, k2, v2 = _prompt_layer(x_prompt, p, gf, sample_cache)
    k2, v2 = k2.reshape(cache_swa_k.shape[1:]), v2.reshape(cache_swa_v.shape[1:])
    return (yp, ys, s1[None], s2[None], k1[None], k2[None], v1[None], v2[None], c1[None], c2[None])
```

```python
import functools

import jax
import jax.numpy as jnp
from jax import lax
from jax.experimental import pallas as pl
from jax.experimental.pallas import tpu as pltpu

F32 = jnp.float32
BF16 = jnp.bfloat16

GLA_HEADS = 4
GLA_DK = 128
GLA_DV = 256
GLA_GATE_RANK = 16
GLA_TAU = 16.0
GLA_CHUNK = 64
SWA_HEADS = 8
SWA_HD = 128
SWA_PATTERNS = ((128, 1), (512, 4), (2048, 16))
SWA_BLOCK = 128
SWA_MAX_WINDOW = 2048
CONV_W = 3
EPS = 1e-6
NEG = -1e30

GLA_QK_W = GLA_HEADS * GLA_DK
GLA_V_W = GLA_HEADS * GLA_DV
SWA_W = SWA_HEADS * SWA_HD
PROJ_W = 2 * GLA_QK_W + 2 * GLA_V_W + 3 * SWA_W
LANE = 128
VMEM_LIMIT = 56 * 1024 * 1024

ALIBI_SLOPES = tuple(2.0 ** (-8.0 * (h + 1) / SWA_HEADS) for h in range(SWA_HEADS))


def _params(semantics):
    return pltpu.CompilerParams(dimension_semantics=semantics, vmem_limit_bytes=VMEM_LIMIT)


def _rms_rows(x, g):
    return x * lax.rsqrt(jnp.mean(x * x, axis=-1, keepdims=True) + EPS) * g


def _silu(x):
    return x / (1.0 + jnp.exp(-x))


def _log_sigmoid(x):
    return jnp.minimum(x, 0.0) - jnp.log1p(jnp.exp(-jnp.abs(x)))


def _dot(a, b):
    return jnp.dot(a, b, preferred_element_type=F32)


def _dot_nt(a, b):
    return lax.dot_general(a, b, (((1,), (1,)), ((), ())), preferred_element_type=F32)


def _dot_tn(a, b):
    return lax.dot_general(a, b, (((0,), (0,)), ((), ())), preferred_element_type=F32)


GLA_COLS = 2 * GLA_QK_W + 2 * GLA_V_W
GLA_STEPS = GLA_COLS // SWA_W


def _inproj_kernel(x_hbm, g_ref, w_ref, wz_ref, wg_ref, bg_ref, proj_ref, loga_ref, qkv_ref, kc_ref, vc_ref,
                   x_buf, xn_ref, sem, *, tiles_per_seq, kept_tiles):
    i, j = pl.program_id(0), pl.program_id(1)
    tm = x_buf.shape[0]

    def x_copy(tile):
        return pltpu.make_async_copy(x_hbm.at[pl.ds(tile * tm, tm)], x_buf, sem.at[0])

    @pl.when(j == 0)
    def _():
        @pl.when(i == 0)
        def _():
            x_copy(0).start()

        x_copy(i).wait()
        xn_ref[...] = _rms_rows(x_buf[...], g_ref[...]).astype(BF16)

        @pl.when(i + 1 < pl.num_programs(0))
        def _():
            x_copy(i + 1).start()

        za = _dot(xn_ref[...], wz_ref[...])
        gate = _dot(za.astype(BF16), wg_ref[...]) + bg_ref[...]
        loga_ref[...] = _log_sigmoid(gate) / GLA_TAU

    @pl.when(j < GLA_STEPS)
    def _():
        proj_ref[...] = _dot(xn_ref[...], w_ref[...]).astype(BF16)

    def swa_step(cache_ref):
        res = _dot(xn_ref[...], w_ref[...])
        for h in range(SWA_HEADS):
            qkv_ref[0, h] = res[:, h * SWA_HD:(h + 1) * SWA_HD]
        if cache_ref is not None:
            @pl.when(i % tiles_per_seq >= tiles_per_seq - kept_tiles)
            def _():
                for h in range(SWA_HEADS):
                    cache_ref[pl.ds(h, tm, stride=SWA_HEADS), :] = res[:, h * SWA_HD:(h + 1) * SWA_HD]

    for step, cache_ref in enumerate((None, kc_ref, vc_ref)):
        pl.when(j == GLA_STEPS + step)(functools.partial(swa_step, cache_ref))


def _inproj(x, g, w_main, w_z, w_gate, b_gate, *, tm, seq, keep):
    m, d = x.shape
    tn = SWA_W
    tps, kt = seq // tm, keep // tm
    assert seq % tm == 0 and keep % tm == 0 and m % seq == 0

    def cache_block(i, j):
        return ((i // tps) * kt + jnp.maximum(i % tps - (tps - kt), 0), 0)

    cache_rows = (m // seq) * keep * SWA_HEADS
    return pl.pallas_call(
        functools.partial(_inproj_kernel, tiles_per_seq=tps, kept_tiles=kt),
        grid=(m // tm, PROJ_W // tn),
        in_specs=[
            pl.BlockSpec(memory_space=pl.ANY),
            pl.BlockSpec((1, d), lambda i, j: (0, 0)),
            pl.BlockSpec((None, d, tn), lambda i, j: (j, 0, 0)),
            pl.BlockSpec((d, LANE), lambda i, j: (0, 0)),
            pl.BlockSpec((LANE, GLA_QK_W), lambda i, j: (0, 0)),
            pl.BlockSpec((1, GLA_QK_W), lambda i, j: (0, 0)),
        ],
        out_specs=[
            pl.BlockSpec((tm, tn), lambda i, j: (i, jnp.minimum(j, GLA_STEPS - 1))),
            pl.BlockSpec((tm, GLA_QK_W), lambda i, j: (i, 0)),
            pl.BlockSpec((1, SWA_HEADS, tm, SWA_HD), lambda i, j: (jnp.maximum(j - GLA_STEPS, 0), 0, i, 0)),
            pl.BlockSpec((tm * SWA_HEADS, SWA_HD), cache_block, pipeline_mode=pl.Buffered(1)),
            pl.BlockSpec((tm * SWA_HEADS, SWA_HD), cache_block, pipeline_mode=pl.Buffered(1)),
        ],
        out_shape=[jax.ShapeDtypeStruct((m, GLA_COLS), BF16),
                   jax.ShapeDtypeStruct((m, GLA_QK_W), F32),
                   jax.ShapeDtypeStruct((3, SWA_HEADS, m, SWA_HD), F32),
                   jax.ShapeDtypeStruct((cache_rows, SWA_HD), F32),
                   jax.ShapeDtypeStruct((cache_rows, SWA_HD), F32)],
        scratch_shapes=[pltpu.VMEM((tm, d), F32), pltpu.VMEM((tm, d), BF16), pltpu.SemaphoreType.DMA((1,))],
        compiler_params=_params(("arbitrary", "arbitrary")),
        name="norm_inproj",
    )(x, g, w_main, w_z, w_gate, b_gate)


def _split3(x):
    hi = x.astype(BF16)
    r1 = x - hi.astype(F32)
    mid = r1.astype(BF16)
    lo = (r1 - mid.astype(F32)).astype(BF16)
    return hi, mid, lo


def _gla_kernel(*refs, chunk, nsub, has_s0):
    scratch = refs[-5:]
    s_scr, qd_scr, oi_scr, u_scr, dec_scr = scratch
    if has_s0:
        q_ref, k_ref, v_ref, ga_ref, la_ref, gn_ref, s0_ref, o_ref, sfin_ref = refs[:-5]
    else:
        q_ref, k_ref, v_ref, ga_ref, la_ref, gn_ref, o_ref, sfin_ref = refs[:-5]
    c = pl.program_id(1)

    @pl.when(c == 0)
    def _():
        if has_s0:
            s_scr[...] = s0_ref[0]
        else:
            s_scr[...] = jnp.zeros_like(s_scr)

    row = lax.broadcasted_iota(jnp.int32, (chunk, chunk), 0)
    col = lax.broadcasted_iota(jnp.int32, (chunk, chunk), 1)
    causal = row >= col
    tril = jnp.where(causal, 1.0, 0.0).astype(BF16)
    eye = (lax.broadcasted_iota(jnp.int32, (GLA_DK, GLA_DK), 0)
           == lax.broadcasted_iota(jnp.int32, (GLA_DK, GLA_DK), 1))

    group = min(nsub, 4)
    heads = range(GLA_HEADS)
    ks = [slice(h * GLA_DK, (h + 1) * GLA_DK) for h in heads]
    vs = [slice(h * GLA_DV, (h + 1) * GLA_DV) for h in heads]

    def local(c2, carry):
        subs = [c2 * group + a for a in range(group)]
        sls = [pl.ds(pl.multiple_of(i * chunk, chunk), chunk) for i in subs]
        pieces = [_split3(la_ref[0, sl, :]) for sl in sls]
        cums = [_dot(tril, hi) + _dot(tril, mid) + _dot(tril, lo) for hi, mid, lo in pieces]
        q_dec, k_dec, k_end, grow = [], [], [], []
        for sl, cum in zip(sls, cums):
            total = cum[chunk - 1:chunk, :]
            k = k_ref[0, sl, :].astype(F32)
            q_dec.append((q_ref[0, sl, :].astype(F32) * (GLA_DK ** -0.5) * jnp.exp(cum)).astype(BF16))
            k_dec.append((k * jnp.exp(-cum)).astype(BF16))
            k_end.append((k * jnp.exp(total - cum)).astype(BF16))
            grow.append(jnp.exp(total))
            qd_scr[sl, :] = q_dec[-1]
        units = [(a, h) for a in range(group) for h in heads]
        vals = {(a, h): v_ref[0, sls[a], vs[h]] for a, h in units}
        raw = {(a, h): _dot_nt(q_dec[a][:, ks[h]], k_dec[a][:, ks[h]]) for a, h in units}
        for a, h in units:
            u_scr[subs[a], h] = _dot_tn(k_end[a][:, ks[h]], vals[a, h])
        attn = {(a, h): jnp.where(causal, raw[a, h], 0.0).astype(BF16) for a, h in units}
        for a, h in units:
            oi_scr[sls[a], vs[h]] = _dot(attn[a, h], vals[a, h])
        for a, h in units:
            decay = jnp.sum(jnp.where(eye, grow[a][:, ks[h]], 0.0), axis=1, keepdims=True)
            dec_scr[subs[a], h] = jnp.broadcast_to(decay, (GLA_DK, GLA_DK))
        return carry

    if nsub == group:
        local(0, 0)
    else:
        lax.fori_loop(0, nsub // group, local, 0)

    for i in range(nsub):
        sl = slice(i * chunk, (i + 1) * chunk)
        states = [s_scr[h] for h in heads]
        inter = [_dot(qd_scr[sl, ks[h]], states[h].astype(BF16)) for h in heads]
        for h in heads:
            decay = dec_scr[i, h]
            s_scr[h] = states[h] * jnp.concatenate([decay] * (GLA_DV // GLA_DK), axis=1) + u_scr[i, h]
        for h in heads:
            on = _rms_rows(oi_scr[sl, vs[h]] + inter[h], gn_ref[:, vs[h]])
            o_ref[0, sl, vs[h]] = (on * _silu(ga_ref[0, sl, vs[h]].astype(F32))).astype(o_ref.dtype)

    @pl.when(c == pl.num_programs(1) - 1)
    def _():
        sfin_ref[0] = s_scr[...]


def _gla(proj3, loga3, g_norm, s0, *, chunk, tc):
    b, t, _ = proj3.shape
    nsub = tc // chunk
    state = pl.BlockSpec((1, GLA_HEADS, GLA_DK, GLA_DV), lambda bi, c: (bi, 0, 0, 0))
    in_specs = [
        pl.BlockSpec((1, tc, GLA_QK_W), lambda bi, c: (bi, c, 0)),
        pl.BlockSpec((1, tc, GLA_QK_W), lambda bi, c: (bi, c, 1)),
        pl.BlockSpec((1, tc, GLA_V_W), lambda bi, c: (bi, c, 1)),
        pl.BlockSpec((1, tc, GLA_V_W), lambda bi, c: (bi, c, 2)),
        pl.BlockSpec((1, tc, GLA_QK_W), lambda bi, c: (bi, c, 0)),
        pl.BlockSpec((1, GLA_V_W), lambda bi, c: (0, 0)),
    ]
    args = [proj3, proj3, proj3, proj3, loga3, g_norm]
    if s0 is not None:
        in_specs.append(state)
        args.append(s0)
    return pl.pallas_call(
        functools.partial(_gla_kernel, chunk=chunk, nsub=nsub, has_s0=s0 is not None),
        grid=(b, t // tc),
        in_specs=in_specs,
        out_specs=[pl.BlockSpec((1, tc, GLA_V_W), lambda bi, c: (bi, c, 0)), state],
        out_shape=[jax.ShapeDtypeStruct((b, t, GLA_V_W), BF16),
                   jax.ShapeDtypeStruct((b, GLA_HEADS, GLA_DK, GLA_DV), F32)],
        scratch_shapes=[pltpu.VMEM((GLA_HEADS, GLA_DK, GLA_DV), F32),
                        pltpu.VMEM((tc, GLA_QK_W), BF16),
                        pltpu.VMEM((tc, GLA_V_W), F32),
                        pltpu.VMEM((nsub, GLA_HEADS, GLA_DK, GLA_DV), F32),
                        pltpu.VMEM((nsub, GLA_HEADS, GLA_DK, GLA_DK), F32)],
        compiler_params=_params(("parallel", "arbitrary")),
        name="gla",
    )(*args)


SWA_SUPER = 2048
SWA_TILES = 16
QUAD = 4


def _swa_prompt_kernel(q_ref, k_ref, v_ref, o_ref, x4, q1, q4, q16, k1, k4, k16, v1, v4, v16, ob, lb, bias):
    n = SWA_BLOCK
    s_tok = q_ref.shape[0]
    l4, l16 = s_tok // QUAD, s_tok // (QUAD * QUAD)
    head, has_prev = pl.program_id(1), pl.program_id(2) > 0
    scale = SWA_HD ** -0.5

    slope = ALIBI_SLOPES[0]
    for h in range(1, SWA_HEADS):
        slope = jnp.where(head == h, ALIBI_SLOPES[h], slope)
    qi = lax.broadcasted_iota(jnp.int32, (n, 2 * n), 0)
    kj = lax.broadcasted_iota(jnp.int32, (n, 2 * n), 1)
    steps = qi + n - kj
    visible = (steps >= 0) & (steps <= n)
    first_visible = visible & (has_prev | (kj >= n))
    for g, (_, dil) in enumerate(SWA_PATTERNS):
        alibi = (-float(dil) * slope) * steps.astype(F32)
        bias[2 * g] = jnp.where(visible, alibi, NEG)
        bias[2 * g + 1] = jnp.where(first_visible, alibi, NEG)

    for buf, length in ((k1, s_tok), (v1, s_tok), (k4, l4), (v4, l4), (k16, l16), (v16, l16)):
        lead = buf.shape[:-2]
        idx = (slice(None),) * len(lead)

        @pl.when(has_prev)
        def _():
            buf[idx + (slice(0, n),)] = buf[idx + (slice(length, length + n),)]

        @pl.when(jnp.logical_not(has_prev))
        def _():
            buf[idx + (slice(0, n),)] = jnp.zeros(lead + (n, SWA_HD), BF16)

    for t, src in enumerate((q_ref, k_ref, v_ref)):
        for r in range(QUAD):
            x4[t, r] = src[pl.ds(r, l4, stride=QUAD), :]
    q1[...] = q_ref[...].astype(BF16)
    k1[n:, :] = k_ref[...].astype(BF16)
    v1[n:, :] = v_ref[...].astype(BF16)
    for r in range(QUAD):
        q4[r] = x4[0, r].astype(BF16)
        k4[r, n:, :] = x4[1, r].astype(BF16)
        v4[r, n:, :] = x4[2, r].astype(BF16)
    for r in range(QUAD * QUAD):
        sub = pl.ds(r // QUAD, l16, stride=QUAD)
        q16[r] = x4[0, r % QUAD, sub, :].astype(BF16)
        k16[r, n:, :] = x4[1, r % QUAD, sub, :].astype(BF16)
        v16[r, n:, :] = x4[2, r % QUAD, sub, :].astype(BF16)

    def tile(q, keys, values, table):
        s = _dot_nt(q, keys) * scale + table
        m = jnp.max(s, axis=1, keepdims=True)
        p = jnp.exp(s - m)
        l = jnp.sum(p, axis=1, keepdims=True)
        acc = _dot(p.astype(BF16), values)
        return acc * (1.0 / l), jnp.broadcast_to(m + jnp.log(l), (n, SWA_HD))

    def d1_body(c, carry):
        for a in range(SWA_TILES):
            i = c * SWA_TILES + a
            row0 = pl.multiple_of(i * n, n)
            o, lse = tile(q1[pl.ds(row0, n), :], k1[pl.ds(row0, 2 * n), :], v1[pl.ds(row0, 2 * n), :],
                          bias[jnp.where(i == 0, 1, 0)])
            ob[0, pl.ds(row0, n), :] = o
            lb[0, pl.ds(row0, n), :] = lse
        return carry

    lax.fori_loop(0, s_tok // n // SWA_TILES, d1_body, 0)

    per4 = l4 // n

    def d4_body(c, carry):
        for a in range(SWA_TILES // per4):
            r = c * (SWA_TILES // per4) + a
            for i in range(per4):
                o, lse = tile(q4[r, i * n:(i + 1) * n, :], k4[r, i * n:(i + 2) * n, :], v4[r, i * n:(i + 2) * n, :],
                              bias[2 + (1 if i == 0 else 0)])
                row0 = pl.multiple_of(r * l4 + i * n, n)
                ob[1, pl.ds(row0, n), :] = o
                lb[1, pl.ds(row0, n), :] = lse
        return carry

    lax.fori_loop(0, QUAD * per4 // SWA_TILES, d4_body, 0)

    per16 = l16 // n

    def d16_body(c, carry):
        for a in range(SWA_TILES // per16):
            r4 = a % QUAD
            j = c * (SWA_TILES // per16 // QUAD) + a // QUAD
            r = r4 + QUAD * j
            for i in range(per16):
                o, lse = tile(q16[r, i * n:(i + 1) * n, :], k16[r, i * n:(i + 2) * n, :],
                              v16[r, i * n:(i + 2) * n, :], bias[4 + (1 if i == 0 else 0)])
                rows = pl.ds(r4 * l4 + QUAD * i * n + j, n, stride=QUAD)
                ob[2, rows, :] = o
                lb[2, rows, :] = lse
        return carry

    lax.fori_loop(0, QUAD * QUAD * per16 // SWA_TILES, d16_body, 0)

    def combine_body(c, carry):
        for r in range(QUAD):
            tok = pl.ds(r + QUAD * c * n, n, stride=QUAD)
            cls = pl.ds(pl.multiple_of(r * l4 + c * n, n), n)
            lses = [lb[0, tok, :], lb[1, cls, :], lb[2, cls, :]]
            outs = [ob[0, tok, :], ob[1, cls, :], ob[2, cls, :]]
            m = jnp.maximum(jnp.maximum(lses[0], lses[1]), lses[2])
            es = [jnp.exp(x - m) for x in lses]
            num = es[0] * outs[0] + es[1] * outs[1] + es[2] * outs[2]
            ob[0, tok, :] = num * (1.0 / (es[0] + es[1] + es[2]))
        return carry

    lax.fori_loop(0, l4 // n, combine_body, 0)
    o_ref[...] = ob[0].astype(o_ref.dtype)


def _swa_prompt(qkv, b, t):
    n, s_tok = SWA_BLOCK, SWA_SUPER
    assert t % s_tok == 0 and [d for _, d in SWA_PATTERNS] == [1, QUAD, QUAD * QUAD]
    assert all(w // d == n for w, d in SWA_PATTERNS)
    l4, l16 = s_tok // QUAD, s_tok // (QUAD * QUAD)
    assert (s_tok // n) % SWA_TILES == 0 and SWA_TILES % (l4 // n) == 0 and SWA_TILES % (QUAD * l16 // n) == 0
    view = qkv.reshape(3, SWA_HEADS, b, t, SWA_HD)

    def part(idx):
        return pl.BlockSpec((None, None, None, s_tok, SWA_HD), lambda bi, h, sb: (idx, h, bi, sb, 0))

    def rows(*lead):
        return pltpu.VMEM(lead + (SWA_HD,), BF16)

    o = pl.pallas_call(
        _swa_prompt_kernel,
        grid=(b, SWA_HEADS, t // s_tok),
        in_specs=[part(0), part(1), part(2)],
        out_specs=pl.BlockSpec((None, s_tok, SWA_HD), lambda bi, h, sb: (bi, sb, h)),
        out_shape=jax.ShapeDtypeStruct((b, t, SWA_W), BF16),
        scratch_shapes=[
            pltpu.VMEM((3, QUAD, l4, SWA_HD), F32),
            rows(s_tok), rows(QUAD, l4), rows(QUAD * QUAD, l16),
            rows(n + s_tok), rows(QUAD, n + l4), rows(QUAD * QUAD, n + l16),
            rows(n + s_tok), rows(QUAD, n + l4), rows(QUAD * QUAD, n + l16),
            pltpu.VMEM((3, s_tok, SWA_HD), F32), pltpu.VMEM((3, s_tok, SWA_HD), F32),
            pltpu.VMEM((2 * len(SWA_PATTERNS), n, 2 * n), F32),
        ],
        compiler_params=_params(("parallel", "parallel", "arbitrary")),
        name="swa_prompt",
    )(view, view, view)
    return o.reshape(b * t, SWA_W)


def _branch_multiplicity(d):
    mult = jnp.zeros(d.shape, F32)
    for window, dil in SWA_PATTERNS:
        hit = (d >= 0) & (d <= window) & ((d & (dil - 1)) == 0)
        mult = mult + jnp.where(hit, 1.0, 0.0)
    return mult


SAMPLE_NEAR = SWA_PATTERNS[1][0]
SAMPLE_GROUP = SWA_PATTERNS[2][1]


def _swa_sample_kernel(q_ref, kn_ref, vn_ref, kfar_ref, vfar_ref, knear_ref, vnear_ref, o_ref, *, w, tn):
    tp = q_ref.shape[1]
    groups, per = kfar_ref.shape[1], kfar_ref.shape[2]
    n_far, n_near = groups * tn, knear_ref.shape[1] // SWA_HEADS
    kfar, vfar = kfar_ref.reshape(groups * per, SWA_HD), vfar_ref.reshape(groups * per, SWA_HD)

    def query_row(shape):
        return lax.broadcasted_iota(jnp.int32, shape, 0)

    e_far = lax.broadcasted_iota(jnp.int32, (tp, n_far), 1)
    pos_far = (e_far // tn) * SAMPLE_GROUP + e_far % tn
    pos_near = (w - n_near) + lax.broadcasted_iota(jnp.int32, (tp, n_near), 1)
    j_new = lax.broadcasted_iota(jnp.int32, (tp, tp), 1)
    dists = [w + query_row((tp, n_far)) - pos_far,
             w + query_row((tp, n_near)) - pos_near,
             jnp.where(j_new < tn, query_row((tp, tp)) - j_new, -1)]
    mults = [_branch_multiplicity(d) for d in dists]
    dists = [d.astype(F32) for d in dists]
    scale = SWA_HD ** -0.5

    for h in range(SWA_HEADS):
        hs = slice(h * SWA_HD, (h + 1) * SWA_HD)
        far_rows = pl.ds(h, n_far, stride=SWA_HEADS)
        near_rows = pl.ds(h, n_near, stride=SWA_HEADS)
        slope = ALIBI_SLOPES[h]
        q = q_ref[0, :, hs]
        keys = [kfar[far_rows, :].astype(BF16), knear_ref[0, near_rows, :].astype(BF16), kn_ref[0, :, hs]]
        values = [vfar[far_rows, :].astype(BF16), vnear_ref[0, near_rows, :].astype(BF16), vn_ref[0, :, hs]]
        scores = [jnp.where(mult > 0.0, _dot_nt(q, k) * scale - slope * d, NEG)
                  for k, d, mult in zip(keys, dists, mults)]
        m = functools.reduce(jnp.maximum, [jnp.max(s, axis=1, keepdims=True) for s in scores])
        probs = [mult * jnp.exp(s - m) for s, mult in zip(scores, mults)]
        l = functools.reduce(lambda a, b: a + b, [jnp.sum(p, axis=1, keepdims=True) for p in probs])
        acc = functools.reduce(lambda a, b: a + b, [_dot(p.astype(BF16), v) for p, v in zip(probs, values)])
        o_ref[0, :, hs] = (acc / l).astype(o_ref.dtype)


def _swa_sample(projp, cache_k, cache_v, *, tn):
    b, tp, _ = projp.shape
    rows = cache_k.shape[1]
    w = rows // SWA_HEADS
    group_rows = SAMPLE_GROUP * SWA_HEADS
    near_rows = SAMPLE_NEAR * SWA_HEADS
    assert w % SAMPLE_NEAR == 0 and SAMPLE_NEAR % SAMPLE_GROUP == 0 and tn <= SAMPLE_GROUP
    assert [wd for wd, _ in SWA_PATTERNS][:2] == [SWA_BLOCK, SAMPLE_NEAR] and w >= SWA_PATTERNS[2][0]
    far_groups = (w - SAMPLE_NEAR) // SAMPLE_GROUP
    grouped = (b, rows // group_rows, group_rows, SWA_HD)

    def new(cb):
        return pl.BlockSpec((1, tp, SWA_W), lambda bi: (bi, 0, cb))

    far = pl.BlockSpec((1, far_groups, tn * SWA_HEADS, SWA_HD), lambda bi: (bi, 0, 0, 0))
    near = pl.BlockSpec((1, near_rows, SWA_HD), lambda bi: (bi, rows // near_rows - 1, 0))
    return pl.pallas_call(
        functools.partial(_swa_sample_kernel, w=w, tn=tn),
        grid=(b,),
        in_specs=[new(0), new(1), new(2), far, far, near, near],
        out_specs=pl.BlockSpec((1, tp, SWA_W), lambda bi: (bi, 0, 0)),
        out_shape=jax.ShapeDtypeStruct((b, tp, SWA_W), BF16),
        compiler_params=_params(("parallel",)),
        name="swa_sample",
    )(projp, projp, projp, cache_k.reshape(grouped), cache_v.reshape(grouped), cache_k, cache_v)


def _outproj_kernel(x_ref, oa_ref, ob_ref, wa_ref, wb_ref, h_ref):
    h_ref[...] = x_ref[...] + _dot(oa_ref[...], wa_ref[...]) + _dot(ob_ref[...], wb_ref[...])


def _outproj(x, o_a, o_b, w_out, *, tm):
    m, d = x.shape
    return pl.pallas_call(
        _outproj_kernel,
        grid=(m // tm,),
        in_specs=[pl.BlockSpec((tm, d), lambda i: (i, 0)),
                  pl.BlockSpec((tm, GLA_V_W), lambda i: (i, 0)),
                  pl.BlockSpec((tm, SWA_W), lambda i: (i, 0)),
                  pl.BlockSpec((GLA_V_W, d), lambda i: (0, 0)),
                  pl.BlockSpec((SWA_W, d), lambda i: (GLA_V_W // SWA_W, 0))],
        out_specs=pl.BlockSpec((tm, d), lambda i: (i, 0)),
        out_shape=jax.ShapeDtypeStruct((m, d), F32),
        compiler_params=_params(("parallel",)),
        name="outproj",
    )(x, o_a, o_b, w_out, w_out)


FFN_HALO = 16
FFN_TF = 512
CACHE_PIECES = 4


def _ffn_prompt_kernel(h_ref, halo_ref, g_ref, wu_ref, cp_ref, wd_ref, gf_ref, kc_hbm, vc_hbm, kn_ref, vn_ref,
                       y_ref, u_ref, ko_hbm, vo_hbm, xn_scr, ea_scr, ev_scr, ks_scr, vs_scr, sems, *, tiles_per_seq):
    i, j = pl.program_id(0), pl.program_id(1)
    tm = h_ref.shape[0]
    last_j = pl.num_programs(1) - 1

    nseq, rows = kc_hbm.shape[0], kc_hbm.shape[1]
    new = kn_ref.shape[1]
    old = rows - new
    piece = old // CACHE_PIECES
    moving = i < nseq

    def load(src_hbm, dst_scr, c, sem):
        part = pl.ds(c * piece, piece)
        return pltpu.make_async_copy(src_hbm.at[i, pl.ds(new + c * piece, piece)], dst_scr.at[part], sem)

    def store(src_scr, dst_hbm, c, sem):
        part = pl.ds(c * piece, piece)
        return pltpu.make_async_copy(src_scr.at[part], dst_hbm.at[i, part], sem)

    loads = [[load(kc_hbm, ks_scr, c, sems.at[0]), load(vc_hbm, vs_scr, c, sems.at[1])]
             for c in range(CACHE_PIECES)]
    stores = [[store(ks_scr, ko_hbm, c, sems.at[2]), store(vs_scr, vo_hbm, c, sems.at[3])]
              for c in range(CACHE_PIECES)]
    stores[0] += [pltpu.make_async_copy(kn_ref.at[0], ko_hbm.at[i, pl.ds(old, new)], sems.at[4]),
                  pltpu.make_async_copy(vn_ref.at[0], vo_hbm.at[i, pl.ds(old, new)], sems.at[5])]

    for c in range(CACHE_PIECES):
        @pl.when(moving & (j == c))
        def _():
            for cp in loads[c]:
                cp.start(priority=1)

        @pl.when(moving & (j == CACHE_PIECES + c))
        def _():
            if c == 0:
                for group in loads:
                    for cp in group:
                        cp.wait()
            for cp in stores[c]:
                cp.start(priority=1)

    @pl.when(j == 0)
    def _():
        first = (i % tiles_per_seq) == 0
        halo = _rms_rows(halo_ref[...], g_ref[...])
        xn_scr[0:FFN_HALO, :] = jnp.where(first, 0.0, halo).astype(BF16)
        xn_scr[FFN_HALO:, :] = _rms_rows(h_ref[...], g_ref[...]).astype(BF16)
        y_ref[...] = h_ref[...]

    xn = xn_scr[...]
    ea_scr[...] = _dot(xn, wu_ref[0])
    ev_scr[...] = _dot(xn, wu_ref[1])

    def conv(e_scr, part):
        out = cp_ref[part, CONV_W:CONV_W + 1, :]
        for tap in range(CONV_W):
            off = FFN_HALO - (CONV_W - 1) + tap
            out = out + cp_ref[part, tap:tap + 1, :] * e_scr[off:off + tm, :]
        return out

    gate = (_silu(conv(ea_scr, 0)) * conv(ev_scr, 1)).astype(BF16)
    y_ref[...] += _dot(gate, wd_ref[...])
    u_ref[0, j, 0] = ea_scr[FFN_HALO + tm - 8:FFN_HALO + tm, :]
    u_ref[0, j, 1] = ev_scr[FFN_HALO + tm - 8:FFN_HALO + tm, :]

    @pl.when(j == last_j)
    def _():
        y_ref[...] = _rms_rows(y_ref[...], gf_ref[...])

    @pl.when(moving & (j == last_j))
    def _():
        for group in stores:
            for cp in group:
                cp.wait()


def _ffn_prompt(h, g_ffn, w_up, conv_params, w_down, g_final, cache_k, cache_v, k_new, v_new, *, seq, tm):
    m, d = h.shape
    f = w_down.shape[0]
    tf = w_up.shape[-1]
    nf = f // tf
    ntile = m // tm
    hb = tm // FFN_HALO
    nseq, rows, _ = cache_k.shape
    new = k_new.shape[1]
    assert nseq <= ntile and nf > 2 * CACHE_PIECES, "one cache per row tile, moved within its column steps"
    assert (rows - new) % (8 * CACHE_PIECES) == 0
    hbm = pl.BlockSpec(memory_space=pl.ANY)
    fresh = pl.BlockSpec((1, new, SWA_HD), lambda i, j: (jnp.minimum(i, nseq - 1), 0, 0))
    y, u_last, k_all, v_all = pl.pallas_call(
        functools.partial(_ffn_prompt_kernel, tiles_per_seq=seq // tm),
        grid=(ntile, nf),
        in_specs=[
            pl.BlockSpec((tm, d), lambda i, j: (i, 0)),
            pl.BlockSpec((FFN_HALO, d), lambda i, j: (jnp.maximum(i * hb - 1, 0), 0)),
            pl.BlockSpec((1, d), lambda i, j: (0, 0)),
            pl.BlockSpec((None, 2, d, tf), lambda i, j: (j, 0, 0, 0)),
            pl.BlockSpec((None, 2, 8, tf), lambda i, j: (j, 0, 0, 0)),
            pl.BlockSpec((tf, d), lambda i, j: (j, 0)),
            pl.BlockSpec((1, d), lambda i, j: (0, 0)),
            hbm, hbm, fresh, fresh,
        ],
        out_specs=[
            pl.BlockSpec((tm, d), lambda i, j: (i, 0)),
            pl.BlockSpec((1, nf, 2, 8, tf), lambda i, j: (i, 0, 0, 0, 0)),
            hbm, hbm,
        ],
        out_shape=[jax.ShapeDtypeStruct((m, d), F32),
                   jax.ShapeDtypeStruct((ntile, nf, 2, 8, tf), F32),
                   jax.ShapeDtypeStruct(cache_k.shape, cache_k.dtype),
                   jax.ShapeDtypeStruct(cache_v.shape, cache_v.dtype)],
        scratch_shapes=[pltpu.VMEM((tm + FFN_HALO, d), BF16),
                        pltpu.VMEM((tm + FFN_HALO, tf), F32),
                        pltpu.VMEM((tm + FFN_HALO, tf), F32),
                        pltpu.VMEM((rows - new, SWA_HD), F32),
                        pltpu.VMEM((rows - new, SWA_HD), F32),
                        pltpu.SemaphoreType.DMA((6,))],
        compiler_params=_params(("arbitrary", "arbitrary")),
        name="convffn_prompt",
    )(h, h, g_ffn, w_up, conv_params, w_down, g_final, cache_k, cache_v, k_new, v_new)
    u_last = u_last.transpose(0, 3, 2, 1, 4).reshape(ntile, 8, 2 * f)
    return y, u_last, k_all, v_all


def _ffn_sample_kernel(h_ref, g_ref, sa_ref, sv_ref, wa_ref, wv_ref, ca_ref, cv_ref, wd_ref,
                       gf_ref, y_ref, na_ref, nv_ref, xn_scr, acc_scr, *, nb):
    j = pl.program_id(0)
    m = h_ref.shape[0]
    keep = (CONV_W - 1) * nb

    @pl.when(j == 0)
    def _():
        xn_scr[...] = _rms_rows(h_ref[...], g_ref[...]).astype(BF16)
        acc_scr[...] = jnp.zeros_like(acc_scr)

    xn = xn_scr[...]

    def conv(s_ref, w_ref, c_ref, n_ref):
        u = _dot(xn, w_ref[...])
        ext = jnp.concatenate([s_ref[...], u], axis=0)
        n_ref[...] = ext[m:m + keep, :]
        out = c_ref[CONV_W:CONV_W + 1, :]
        for tap in range(CONV_W):
            out = out + c_ref[tap:tap + 1, :] * ext[tap * nb:tap * nb + m, :]
        return out

    gate = (_silu(conv(sa_ref, wa_ref, ca_ref, na_ref)) * conv(sv_ref, wv_ref, cv_ref, nv_ref)).astype(BF16)
    acc_scr[...] += _dot(gate, wd_ref[...])

    @pl.when(j == pl.num_programs(0) - 1)
    def _():
        y_ref[...] = _rms_rows(h_ref[...] + acc_scr[...], gf_ref[...])


def _ffn_sample(h_tm, state_tm, g_ffn, w_up, conv_params, w_down, g_final, *, nb):
    m, d = h_tm.shape
    f = w_down.shape[0]
    tf = w_up.shape[-1]
    nf = f // tf
    keep = (CONV_W - 1) * nb
    return pl.pallas_call(
        functools.partial(_ffn_sample_kernel, nb=nb),
        grid=(nf,),
        in_specs=[
            pl.BlockSpec((m, d), lambda j: (0, 0)),
            pl.BlockSpec((1, d), lambda j: (0, 0)),
            pl.BlockSpec((keep, tf), lambda j: (0, j)),
            pl.BlockSpec((keep, tf), lambda j: (0, nf + j)),
            pl.BlockSpec((None, None, d, tf), lambda j: (j, 0, 0, 0)),
            pl.BlockSpec((None, None, d, tf), lambda j: (j, 1, 0, 0)),
            pl.BlockSpec((None, None, 8, tf), lambda j: (j, 0, 0, 0)),
            pl.BlockSpec((None, None, 8, tf), lambda j: (j, 1, 0, 0)),
            pl.BlockSpec((tf, d), lambda j: (j, 0)),
            pl.BlockSpec((1, d), lambda j: (0, 0)),
        ],
        out_specs=[
            pl.BlockSpec((m, d), lambda j: (0, 0)),
            pl.BlockSpec((keep, tf), lambda j: (0, j)),
            pl.BlockSpec((keep, tf), lambda j: (0, j)),
        ],
        out_shape=[jax.ShapeDtypeStruct((m, d), F32),
                   jax.ShapeDtypeStruct((keep, f), F32),
                   jax.ShapeDtypeStruct((keep, f), F32)],
        scratch_shapes=[pltpu.VMEM((m, d), BF16), pltpu.VMEM((m, d), F32)],
        compiler_params=_params(("arbitrary",)),
        name="convffn_sample",
    )(h_tm, g_ffn, state_tm, state_tm, w_up, w_up, conv_params, conv_params, w_down, g_final)


def _pack_weights(g_attn_norm, w_in, w_gate_up, b_gate, g_gla_norm, w_out, g_ffn_norm, w_up, w_conv, b_conv,
                  w_down):
    z0 = 2 * GLA_QK_W + 2 * GLA_V_W
    z1 = z0 + GLA_GATE_RANK
    def column_tiles(w, width):
        k, n = w.shape
        return w.astype(BF16).reshape(k, n // width, width).transpose(1, 0, 2)

    def gate_value_tiles(w):
        k, n = w.shape
        return w.reshape(k, 2, n // (2 * FFN_TF), FFN_TF).transpose(2, 1, 0, 3)

    w_main = column_tiles(jnp.concatenate([w_in[:, :z0], w_in[:, z1:]], axis=1), SWA_W)
    w_z = jnp.pad(w_in[:, z0:z1], ((0, 0), (0, LANE - GLA_GATE_RANK))).astype(BF16)
    w_gate = jnp.pad(w_gate_up, ((0, LANE - GLA_GATE_RANK), (0, 0))).astype(BF16)
    conv_rows = jnp.concatenate([w_conv, b_conv[None, :], jnp.zeros((8 - CONV_W - 1, b_conv.shape[0]), F32)])
    return dict(
        g_attn=g_attn_norm[None, :], w_main=w_main, w_z=w_z, w_gate=w_gate, b_gate=b_gate[None, :],
        g_gla=g_gla_norm[None, :], w_out=w_out.astype(BF16), g_ffn=g_ffn_norm[None, :],
        w_up=gate_value_tiles(w_up.astype(BF16)), conv_params=gate_value_tiles(conv_rows),
        w_down=w_down.astype(BF16))


def _prompt_layer(x, p, g_final, sample_cache):
    b, t, d = x.shape
    m = b * t
    f = p["w_down"].shape[0]
    keep = min(SWA_MAX_WINDOW, t)
    x2 = x.reshape(m, d)
    proj, loga, qkv, k_buf, v_buf = _inproj(x2, p["g_attn"], p["w_main"], p["w_z"], p["w_gate"], p["b_gate"],
                                            tm=1024, seq=t, keep=keep)
    o_a, s_fin = _gla(proj.reshape(b, t, GLA_COLS), loga.reshape(b, t, GLA_QK_W), p["g_gla"], None,
                      chunk=GLA_CHUNK, tc=512)
    o_b = _swa_prompt(qkv, b, t)
    h = _outproj(x2, o_a.reshape(m, GLA_V_W), o_b, p["w_out"], tm=512)
    tm = 512
    y, u_last, k_all, v_all = _ffn_prompt(h, p["g_ffn"], p["w_up"], p["conv_params"], p["w_down"], g_final,
                                          *sample_cache, seq=t, tm=tm)
    shape = (b, keep, SWA_HEADS, SWA_HD)
    last = u_last.reshape(b, t // tm, 8, 2 * f)[:, -1, 8 - (CONV_W - 1):, :]
    return y.reshape(b, t, d), s_fin, k_buf.reshape(shape), v_buf.reshape(shape), last, k_all, v_all


def _sample_layer(x, s0, conv_s0, cache_k, cache_v, p, g_final):
    b, tn, d = x.shape
    m = b * tn
    tp = 8
    w = cache_k.shape[1]
    x2 = x.reshape(m, d)
    proj, loga, qkv, k_new, v_new = _inproj(x2, p["g_attn"], p["w_main"], p["w_z"], p["w_gate"], p["b_gate"],
                                            tm=m, seq=m, keep=m)
    pad = ((0, 0), (0, tp - tn), (0, 0))
    projp = jnp.pad(proj.reshape(b, tn, GLA_COLS), pad)
    logap = jnp.pad(loga.reshape(b, tn, GLA_QK_W), pad)
    o_a, s_fin = _gla(projp, logap, p["g_gla"], s0, chunk=tp, tc=tp)
    qkvp = jnp.pad(qkv.transpose(2, 0, 1, 3).reshape(b, tn, 3 * SWA_W).astype(BF16), pad)
    rows = (b, tn * SWA_HEADS, SWA_HD)
    cache = (cache_k.reshape(b, w * SWA_HEADS, SWA_HD), cache_v.reshape(b, w * SWA_HEADS, SWA_HD),
             k_new.reshape(rows), v_new.reshape(rows))
    o_b = _swa_sample(qkvp, cache[0], cache[1], tn=tn)
    h = _outproj(x2, o_a[:, :tn].reshape(m, GLA_V_W), o_b[:, :tn].reshape(m, SWA_W), p["w_out"], tm=m)
    h_tm = h.reshape(b, tn, d).transpose(1, 0, 2).reshape(m, d)
    state_tm = conv_s0.transpose(1, 0, 2).reshape((CONV_W - 1) * b, -1)
    y_tm, na, nv = _ffn_sample(h_tm, state_tm, p["g_ffn"], p["w_up"], p["conv_params"], p["w_down"], g_final, nb=b)
    y = y_tm.reshape(tn, b, d).transpose(1, 0, 2)
    conv_new = jnp.concatenate([na, nv], axis=-1).reshape(CONV_W - 1, b, -1).transpose(1, 0, 2)
    return y, s_fin, conv_new, cache


def kernel(x_prompt, x_sample, state_gla, cache_swa_k, cache_swa_v, state_ffn_conv, g_attn_norm, w_in, w_gate_up,
           b_gate, g_gla_norm, w_out, g_ffn_norm, w_up, w_conv, b_conv, w_down, g_final):
    depth = w_in.shape[0]
    assert depth == 1, "the final RMSNorm is fused into the last layer's ConvFFN kernel"
    assert cache_swa_k.shape[2] == SWA_MAX_WINDOW and x_sample.shape[1] <= 8
    p = _pack_weights(g_attn_norm[0], w_in[0], w_gate_up[0], b_gate[0], g_gla_norm[0], w_out[0], g_ffn_norm[0],
                      w_up[0], w_conv[0], b_conv[0], w_down[0])
    gf = g_final[None, :]
    ys, s2, c2, sample_cache = _sample_layer(x_sample, state_gla[0], state_ffn_conv[0], cache_swa_k[0],
                                             cache_swa_v[0], p, gf)
    yp, s1, k1, v1, c1, k2, v2 = _prompt_layer(x_prompt, p, gf, sample_cache)
    k2, v2 = k2.reshape(cache_swa_k.shape[1:]), v2.reshape(cache_swa_v.shape[1:])
    return (yp, ys, s1[None], s2[None], k1[None], k2[None], v1[None], v2[None], c1[None], c2[None])
```

```python
import functools

import jax
import jax.numpy as jnp
from jax import lax
from jax.experimental import pallas as pl
from jax.experimental.pallas import tpu as pltpu

F32 = jnp.float32
BF16 = jnp.bfloat16

GLA_HEADS = 4
GLA_DK = 128
GLA_DV = 256
GLA_GATE_RANK = 16
GLA_TAU = 16.0
GLA_CHUNK = 64
SWA_HEADS = 8
SWA_HD = 128
SWA_PATTERNS = ((128, 1), (512, 4), (2048, 16))
SWA_BLOCK = 128
SWA_MAX_WINDOW = 2048
CONV_W = 3
EPS = 1e-6
NEG = -1e30

GLA_QK_W = GLA_HEADS * GLA_DK
GLA_V_W = GLA_HEADS * GLA_DV
SWA_W = SWA_HEADS * SWA_HD
PROJ_W = 2 * GLA_QK_W + 2 * GLA_V_W + 3 * SWA_W
LANE = 128
VMEM_LIMIT = 56 * 1024 * 1024

ALIBI_SLOPES = tuple(2.0 ** (-8.0 * (h + 1) / SWA_HEADS) for h in range(SWA_HEADS))


def _params(semantics):
    return pltpu.CompilerParams(dimension_semantics=semantics, vmem_limit_bytes=VMEM_LIMIT)


def _rms_rows(x, g):
    return x * lax.rsqrt(jnp.mean(x * x, axis=-1, keepdims=True) + EPS) * g


def _silu(x):
    return x / (1.0 + jnp.exp(-x))


def _log_sigmoid(x):
    return jnp.minimum(x, 0.0) - jnp.log1p(jnp.exp(-jnp.abs(x)))


def _dot(a, b):
    return jnp.dot(a, b, preferred_element_type=F32)


def _dot_nt(a, b):
    return lax.dot_general(a, b, (((1,), (1,)), ((), ())), preferred_element_type=F32)


def _dot_tn(a, b):
    return lax.dot_general(a, b, (((0,), (0,)), ((), ())), preferred_element_type=F32)


GLA_COLS = 2 * GLA_QK_W + 2 * GLA_V_W
GLA_STEPS = GLA_COLS // SWA_W


def _inproj_kernel(x_hbm, g_ref, w_ref, wz_ref, wg_ref, bg_ref, proj_ref, loga_ref, qkv_ref, kc_ref, vc_ref,
                   x_buf, xn_ref, sem, *, tiles_per_seq, kept_tiles):
    i, j = pl.program_id(0), pl.program_id(1)
    tm = x_buf.shape[0]

    def x_copy(tile):
        return pltpu.make_async_copy(x_hbm.at[pl.ds(tile * tm, tm)], x_buf, sem.at[0])

    @pl.when(j == 0)
    def _():
        @pl.when(i == 0)
        def _():
            x_copy(0).start()

        x_copy(i).wait()
        xn_ref[...] = _rms_rows(x_buf[...], g_ref[...]).astype(BF16)

        @pl.when(i + 1 < pl.num_programs(0))
        def _():
            x_copy(i + 1).start()

        za = _dot(xn_ref[...], wz_ref[...])
        gate = _dot(za.astype(BF16), wg_ref[...]) + bg_ref[...]
        loga_ref[...] = _log_sigmoid(gate) / GLA_TAU

    @pl.when(j < GLA_STEPS)
    def _():
        proj_ref[...] = _dot(xn_ref[...], w_ref[...]).astype(BF16)

    def swa_step(cache_ref):
        res = _dot(xn_ref[...], w_ref[...])
        for h in range(SWA_HEADS):
            qkv_ref[0, h] = res[:, h * SWA_HD:(h + 1) * SWA_HD]
        if cache_ref is not None:
            @pl.when(i % tiles_per_seq >= tiles_per_seq - kept_tiles)
            def _():
                for h in range(SWA_HEADS):
                    cache_ref[pl.ds(h, tm, stride=SWA_HEADS), :] = res[:, h * SWA_HD:(h + 1) * SWA_HD]

    for step, cache_ref in enumerate((None, kc_ref, vc_ref)):
        pl.when(j == GLA_STEPS + step)(functools.partial(swa_step, cache_ref))


def _inproj(x, g, w_main, w_z, w_gate, b_gate, *, tm, seq, keep):
    m, d = x.shape
    tn = SWA_W
    tps, kt = seq // tm, keep // tm
    assert seq % tm == 0 and keep % tm == 0 and m % seq == 0

    def cache_block(i, j):
        return ((i // tps) * kt + jnp.maximum(i % tps - (tps - kt), 0), 0)

    cache_rows = (m // seq) * keep * SWA_HEADS
    return pl.pallas_call(
        functools.partial(_inproj_kernel, tiles_per_seq=tps, kept_tiles=kt),
        grid=(m // tm, PROJ_W // tn),
        in_specs=[
            pl.BlockSpec(memory_space=pl.ANY),
            pl.BlockSpec((1, d), lambda i, j: (0, 0)),
            pl.BlockSpec((d, tn), lambda i, j: (0, j)),
            pl.BlockSpec((d, LANE), lambda i, j: (0, 0)),
            pl.BlockSpec((LANE, GLA_QK_W), lambda i, j: (0, 0)),
            pl.BlockSpec((1, GLA_QK_W), lambda i, j: (0, 0)),
        ],
        out_specs=[
            pl.BlockSpec((tm, tn), lambda i, j: (i, jnp.minimum(j, GLA_STEPS - 1))),
            pl.BlockSpec((tm, GLA_QK_W), lambda i, j: (i, 0)),
            pl.BlockSpec((1, SWA_HEADS, tm, SWA_HD), lambda i, j: (jnp.maximum(j - GLA_STEPS, 0), 0, i, 0)),
            pl.BlockSpec((tm * SWA_HEADS, SWA_HD), cache_block, pipeline_mode=pl.Buffered(1)),
            pl.BlockSpec((tm * SWA_HEADS, SWA_HD), cache_block, pipeline_mode=pl.Buffered(1)),
        ],
        out_shape=[jax.ShapeDtypeStruct((m, GLA_COLS), BF16),
                   jax.ShapeDtypeStruct((m, GLA_QK_W), F32),
                   jax.ShapeDtypeStruct((3, SWA_HEADS, m, SWA_HD), F32),
                   jax.ShapeDtypeStruct((cache_rows, SWA_HD), F32),
                   jax.ShapeDtypeStruct((cache_rows, SWA_HD), F32)],
        scratch_shapes=[pltpu.VMEM((tm, d), F32), pltpu.VMEM((tm, d), BF16), pltpu.SemaphoreType.DMA((1,))],
        compiler_params=_params(("arbitrary", "arbitrary")),
        name="norm_inproj",
    )(x, g, w_main, w_z, w_gate, b_gate)


def _split3(x):
    hi = x.astype(BF16)
    r1 = x - hi.astype(F32)
    mid = r1.astype(BF16)
    lo = (r1 - mid.astype(F32)).astype(BF16)
    return hi, mid, lo


def _gla_kernel(*refs, chunk, nsub, has_s0):
    scratch = refs[-5:]
    s_scr, qd_scr, oi_scr, u_scr, dec_scr = scratch
    if has_s0:
        q_ref, k_ref, v_ref, ga_ref, la_ref, gn_ref, s0_ref, o_ref, sfin_ref = refs[:-5]
    else:
        q_ref, k_ref, v_ref, ga_ref, la_ref, gn_ref, o_ref, sfin_ref = refs[:-5]
    c = pl.program_id(1)

    @pl.when(c == 0)
    def _():
        if has_s0:
            s_scr[...] = s0_ref[0]
        else:
            s_scr[...] = jnp.zeros_like(s_scr)

    row = lax.broadcasted_iota(jnp.int32, (chunk, chunk), 0)
    col = lax.broadcasted_iota(jnp.int32, (chunk, chunk), 1)
    causal = row >= col
    tril = jnp.where(causal, 1.0, 0.0).astype(BF16)
    eye = (lax.broadcasted_iota(jnp.int32, (GLA_DK, GLA_DK), 0)
           == lax.broadcasted_iota(jnp.int32, (GLA_DK, GLA_DK), 1))

    group = min(nsub, 4)
    heads = range(GLA_HEADS)
    ks = [slice(h * GLA_DK, (h + 1) * GLA_DK) for h in heads]
    vs = [slice(h * GLA_DV, (h + 1) * GLA_DV) for h in heads]

    def local(c2, carry):
        subs = [c2 * group + a for a in range(group)]
        sls = [pl.ds(pl.multiple_of(i * chunk, chunk), chunk) for i in subs]
        pieces = [_split3(la_ref[0, sl, :]) for sl in sls]
        cums = [_dot(tril, hi) + _dot(tril, mid) + _dot(tril, lo) for hi, mid, lo in pieces]
        q_dec, k_dec, k_end, grow = [], [], [], []
        for sl, cum in zip(sls, cums):
            total = cum[chunk - 1:chunk, :]
            k = k_ref[0, sl, :].astype(F32)
            q_dec.append((q_ref[0, sl, :].astype(F32) * (GLA_DK ** -0.5) * jnp.exp(cum)).astype(BF16))
            k_dec.append((k * jnp.exp(-cum)).astype(BF16))
            k_end.append((k * jnp.exp(total - cum)).astype(BF16))
            grow.append(jnp.exp(total))
            qd_scr[sl, :] = q_dec[-1]
        units = [(a, h) for a in range(group) for h in heads]
        vals = {(a, h): v_ref[0, sls[a], vs[h]] for a, h in units}
        raw = {(a, h): _dot_nt(q_dec[a][:, ks[h]], k_dec[a][:, ks[h]]) for a, h in units}
        for a, h in units:
            u_scr[subs[a], h] = _dot_tn(k_end[a][:, ks[h]], vals[a, h])
        attn = {(a, h): jnp.where(causal, raw[a, h], 0.0).astype(BF16) for a, h in units}
        for a, h in units:
            oi_scr[sls[a], vs[h]] = _dot(attn[a, h], vals[a, h])
        for a, h in units:
            decay = jnp.sum(jnp.where(eye, grow[a][:, ks[h]], 0.0), axis=1, keepdims=True)
            dec_scr[subs[a], h] = jnp.broadcast_to(decay, (GLA_DK, GLA_DK))
        return carry

    if nsub == group:
        local(0, 0)
    else:
        lax.fori_loop(0, nsub // group, local, 0)

    for i in range(nsub):
        sl = slice(i * chunk, (i + 1) * chunk)
        states = [s_scr[h] for h in heads]
        inter = [_dot(qd_scr[sl, ks[h]], states[h].astype(BF16)) for h in heads]
        for h in heads:
            decay = dec_scr[i, h]
            s_scr[h] = states[h] * jnp.concatenate([decay] * (GLA_DV // GLA_DK), axis=1) + u_scr[i, h]
        for h in heads:
            on = _rms_rows(oi_scr[sl, vs[h]] + inter[h], gn_ref[:, vs[h]])
            o_ref[0, sl, vs[h]] = (on * _silu(ga_ref[0, sl, vs[h]].astype(F32))).astype(o_ref.dtype)

    @pl.when(c == pl.num_programs(1) - 1)
    def _():
        sfin_ref[0] = s_scr[...]


def _gla(proj3, loga3, g_norm, s0, *, chunk, tc):
    b, t, _ = proj3.shape
    nsub = tc // chunk
    state = pl.BlockSpec((1, GLA_HEADS, GLA_DK, GLA_DV), lambda bi, c: (bi, 0, 0, 0))
    in_specs = [
        pl.BlockSpec((1, tc, GLA_QK_W), lambda bi, c: (bi, c, 0)),
        pl.BlockSpec((1, tc, GLA_QK_W), lambda bi, c: (bi, c, 1)),
        pl.BlockSpec((1, tc, GLA_V_W), lambda bi, c: (bi, c, 1)),
        pl.BlockSpec((1, tc, GLA_V_W), lambda bi, c: (bi, c, 2)),
        pl.BlockSpec((1, tc, GLA_QK_W), lambda bi, c: (bi, c, 0)),
        pl.BlockSpec((1, GLA_V_W), lambda bi, c: (0, 0)),
    ]
    args = [proj3, proj3, proj3, proj3, loga3, g_norm]
    if s0 is not None:
        in_specs.append(state)
        args.append(s0)
    return pl.pallas_call(
        functools.partial(_gla_kernel, chunk=chunk, nsub=nsub, has_s0=s0 is not None),
        grid=(b, t // tc),
        in_specs=in_specs,
        out_specs=[pl.BlockSpec((1, tc, GLA_V_W), lambda bi, c: (bi, c, 0)), state],
        out_shape=[jax.ShapeDtypeStruct((b, t, GLA_V_W), BF16),
                   jax.ShapeDtypeStruct((b, GLA_HEADS, GLA_DK, GLA_DV), F32)],
        scratch_shapes=[pltpu.VMEM((GLA_HEADS, GLA_DK, GLA_DV), F32),
                        pltpu.VMEM((tc, GLA_QK_W), BF16),
                        pltpu.VMEM((tc, GLA_V_W), F32),
                        pltpu.VMEM((nsub, GLA_HEADS, GLA_DK, GLA_DV), F32),
                        pltpu.VMEM((nsub, GLA_HEADS, GLA_DK, GLA_DK), F32)],
        compiler_params=_params(("parallel", "arbitrary")),
        name="gla",
    )(*args)


SWA_SUPER = 2048
SWA_TILES = 16
SWA_GROUP = 8
QUAD = 4


def _swa_prompt_kernel(q_ref, k_ref, v_ref, o_ref, x4, q1, q4, q16, k1, k4, k16, v1, v4, v16, ob, lb, bias):
    n = SWA_BLOCK
    s_tok = q_ref.shape[0]
    l4, l16 = s_tok // QUAD, s_tok // (QUAD * QUAD)
    head, has_prev = pl.program_id(1), pl.program_id(2) > 0
    scale = SWA_HD ** -0.5

    slope = ALIBI_SLOPES[0]
    for h in range(1, SWA_HEADS):
        slope = jnp.where(head == h, ALIBI_SLOPES[h], slope)
    qi = lax.broadcasted_iota(jnp.int32, (n, 2 * n), 0)
    kj = lax.broadcasted_iota(jnp.int32, (n, 2 * n), 1)
    steps = qi + n - kj
    visible = (steps >= 0) & (steps <= n)
    first_visible = visible & (has_prev | (kj >= n))
    for g, (_, dil) in enumerate(SWA_PATTERNS):
        alibi = (-float(dil) * slope) * steps.astype(F32)
        bias[2 * g] = jnp.where(visible, alibi, NEG)
        bias[2 * g + 1] = jnp.where(first_visible, alibi, NEG)

    for buf, length in ((k1, s_tok), (v1, s_tok), (k4, l4), (v4, l4), (k16, l16), (v16, l16)):
        lead = buf.shape[:-2]
        idx = (slice(None),) * len(lead)

        @pl.when(has_prev)
        def _():
            buf[idx + (slice(0, n),)] = buf[idx + (slice(length, length + n),)]

        @pl.when(jnp.logical_not(has_prev))
        def _():
            buf[idx + (slice(0, n),)] = jnp.zeros(lead + (n, SWA_HD), BF16)

    for t, src in enumerate((q_ref, k_ref, v_ref)):
        for r in range(QUAD):
            x4[t, r] = src[pl.ds(r, l4, stride=QUAD), :]
    q1[...] = q_ref[...].astype(BF16)
    k1[n:, :] = k_ref[...].astype(BF16)
    v1[n:, :] = v_ref[...].astype(BF16)
    for r in range(QUAD):
        q4[r] = x4[0, r].astype(BF16)
        k4[r, n:, :] = x4[1, r].astype(BF16)
        v4[r, n:, :] = x4[2, r].astype(BF16)
    for r in range(QUAD * QUAD):
        sub = pl.ds(r // QUAD, l16, stride=QUAD)
        q16[r] = x4[0, r % QUAD, sub, :].astype(BF16)
        k16[r, n:, :] = x4[1, r % QUAD, sub, :].astype(BF16)
        v16[r, n:, :] = x4[2, r % QUAD, sub, :].astype(BF16)

    def run_tiles(tiles):
        for g0 in range(0, len(tiles), SWA_GROUP):
            group = tiles[g0:g0 + SWA_GROUP]
            scores = [_dot_nt(q(), keys()) * scale + table() for q, keys, _, table, _, _ in group]
            probs, lses = [], []
            for s in scores:
                m = jnp.max(s, axis=1, keepdims=True)
                p = jnp.exp(s - m)
                l = jnp.sum(p, axis=1, keepdims=True)
                probs.append(p.astype(BF16))
                lses.append((1.0 / l, m + jnp.log(l)))
            accs = [_dot(p, tile[2]()) for p, tile in zip(probs, group)]
            for acc, (inv, lse), (_, _, _, _, branch, rows) in zip(accs, lses, group):
                ob[branch, rows, :] = acc * inv
                lb[branch, rows, :] = jnp.broadcast_to(lse, (n, SWA_HD))

    def window(ref, lead, start, size):
        return lambda: ref[lead + (pl.ds(start, size), slice(None))]

    def d1_body(c, carry):
        tiles = []
        for a in range(SWA_TILES):
            i = c * SWA_TILES + a
            row0 = pl.multiple_of(i * n, n)
            tiles.append((window(q1, (), row0, n), window(k1, (), row0, 2 * n), window(v1, (), row0, 2 * n),
                          lambda i=i: bias[jnp.where(i == 0, 1, 0)], 0, pl.ds(row0, n)))
        run_tiles(tiles)
        return carry

    lax.fori_loop(0, s_tok // n // SWA_TILES, d1_body, 0)

    per4 = l4 // n

    def d4_body(c, carry):
        tiles = []
        for a in range(SWA_TILES // per4):
            r = c * (SWA_TILES // per4) + a
            for i in range(per4):
                tiles.append((window(q4, (r,), i * n, n), window(k4, (r,), i * n, 2 * n),
                              window(v4, (r,), i * n, 2 * n), lambda i=i: bias[2 + (1 if i == 0 else 0)], 1,
                              pl.ds(pl.multiple_of(r * l4 + i * n, n), n)))
        run_tiles(tiles)
        return carry

    lax.fori_loop(0, QUAD * per4 // SWA_TILES, d4_body, 0)

    per16 = l16 // n

    def d16_body(c, carry):
        tiles = []
        for a in range(SWA_TILES // per16):
            r4 = a % QUAD
            j = c * (SWA_TILES // per16 // QUAD) + a // QUAD
            r = r4 + QUAD * j
            for i in range(per16):
                tiles.append((window(q16, (r,), i * n, n), window(k16, (r,), i * n, 2 * n),
                              window(v16, (r,), i * n, 2 * n), lambda i=i: bias[4 + (1 if i == 0 else 0)], 2,
                              pl.ds(r4 * l4 + QUAD * i * n + j, n, stride=QUAD)))
        run_tiles(tiles)
        return carry

    lax.fori_loop(0, QUAD * QUAD * per16 // SWA_TILES, d16_body, 0)

    def combine_body(c, carry):
        for r in range(QUAD):
            tok = pl.ds(r + QUAD * c * n, n, stride=QUAD)
            cls = pl.ds(pl.multiple_of(r * l4 + c * n, n), n)
            lses = [lb[0, tok, :], lb[1, cls, :], lb[2, cls, :]]
            outs = [ob[0, tok, :], ob[1, cls, :], ob[2, cls, :]]
            m = jnp.maximum(jnp.maximum(lses[0], lses[1]), lses[2])
            es = [jnp.exp(x - m) for x in lses]
            num = es[0] * outs[0] + es[1] * outs[1] + es[2] * outs[2]
            ob[0, tok, :] = num * (1.0 / (es[0] + es[1] + es[2]))
        return carry

    lax.fori_loop(0, l4 // n, combine_body, 0)
    o_ref[...] = ob[0].astype(o_ref.dtype)


def _swa_prompt(qkv, b, t):
    n, s_tok = SWA_BLOCK, SWA_SUPER
    assert t % s_tok == 0 and [d for _, d in SWA_PATTERNS] == [1, QUAD, QUAD * QUAD]
    assert all(w // d == n for w, d in SWA_PATTERNS)
    l4, l16 = s_tok // QUAD, s_tok // (QUAD * QUAD)
    assert (s_tok // n) % SWA_TILES == 0 and SWA_TILES % (l4 // n) == 0 and SWA_TILES % (QUAD * l16 // n) == 0
    view = qkv.reshape(3, SWA_HEADS, b, t, SWA_HD)

    def part(idx):
        return pl.BlockSpec((None, None, None, s_tok, SWA_HD), lambda bi, h, sb: (idx, h, bi, sb, 0))

    def rows(*lead):
        return pltpu.VMEM(lead + (SWA_HD,), BF16)

    o = pl.pallas_call(
        _swa_prompt_kernel,
        grid=(b, SWA_HEADS, t // s_tok),
        in_specs=[part(0), part(1), part(2)],
        out_specs=pl.BlockSpec((None, s_tok, SWA_HD), lambda bi, h, sb: (bi, sb, h)),
        out_shape=jax.ShapeDtypeStruct((b, t, SWA_W), BF16),
        scratch_shapes=[
            pltpu.VMEM((3, QUAD, l4, SWA_HD), F32),
            rows(s_tok), rows(QUAD, l4), rows(QUAD * QUAD, l16),
            rows(n + s_tok), rows(QUAD, n + l4), rows(QUAD * QUAD, n + l16),
            rows(n + s_tok), rows(QUAD, n + l4), rows(QUAD * QUAD, n + l16),
            pltpu.VMEM((3, s_tok, SWA_HD), F32), pltpu.VMEM((3, s_tok, SWA_HD), F32),
            pltpu.VMEM((2 * len(SWA_PATTERNS), n, 2 * n), F32),
        ],
        compiler_params=_params(("parallel", "parallel", "arbitrary")),
        name="swa_prompt",
    )(view, view, view)
    return o.reshape(b * t, SWA_W)


def _branch_multiplicity(d):
    mult = jnp.zeros(d.shape, F32)
    for window, dil in SWA_PATTERNS:
        hit = (d >= 0) & (d <= window) & ((d & (dil - 1)) == 0)
        mult = mult + jnp.where(hit, 1.0, 0.0)
    return mult


SAMPLE_NEAR = SWA_PATTERNS[1][0]
SAMPLE_GROUP = SWA_PATTERNS[2][1]


def _swa_sample_kernel(q_ref, kn_ref, vn_ref, kfar_ref, vfar_ref, knear_ref, vnear_ref, o_ref, *, w, tn):
    tp = q_ref.shape[1]
    groups, per = kfar_ref.shape[1], kfar_ref.shape[2]
    n_far, n_near = groups * tn, knear_ref.shape[1] // SWA_HEADS
    kfar, vfar = kfar_ref.reshape(groups * per, SWA_HD), vfar_ref.reshape(groups * per, SWA_HD)

    def query_row(shape):
        return lax.broadcasted_iota(jnp.int32, shape, 0)

    e_far = lax.broadcasted_iota(jnp.int32, (tp, n_far), 1)
    pos_far = (e_far // tn) * SAMPLE_GROUP + e_far % tn
    pos_near = (w - n_near) + lax.broadcasted_iota(jnp.int32, (tp, n_near), 1)
    j_new = lax.broadcasted_iota(jnp.int32, (tp, tp), 1)
    dists = [w + query_row((tp, n_far)) - pos_far,
             w + query_row((tp, n_near)) - pos_near,
             jnp.where(j_new < tn, query_row((tp, tp)) - j_new, -1)]
    mults = [_branch_multiplicity(d) for d in dists]
    dists = [d.astype(F32) for d in dists]
    scale = SWA_HD ** -0.5

    for h in range(SWA_HEADS):
        hs = slice(h * SWA_HD, (h + 1) * SWA_HD)
        far_rows = pl.ds(h, n_far, stride=SWA_HEADS)
        near_rows = pl.ds(h, n_near, stride=SWA_HEADS)
        slope = ALIBI_SLOPES[h]
        q = q_ref[0, :, hs]
        keys = [kfar[far_rows, :].astype(BF16), knear_ref[0, near_rows, :].astype(BF16), kn_ref[0, :, hs]]
        values = [vfar[far_rows, :].astype(BF16), vnear_ref[0, near_rows, :].astype(BF16), vn_ref[0, :, hs]]
        scores = [jnp.where(mult > 0.0, _dot_nt(q, k) * scale - slope * d, NEG)
                  for k, d, mult in zip(keys, dists, mults)]
        m = functools.reduce(jnp.maximum, [jnp.max(s, axis=1, keepdims=True) for s in scores])
        probs = [mult * jnp.exp(s - m) for s, mult in zip(scores, mults)]
        l = functools.reduce(lambda a, b: a + b, [jnp.sum(p, axis=1, keepdims=True) for p in probs])
        acc = functools.reduce(lambda a, b: a + b, [_dot(p.astype(BF16), v) for p, v in zip(probs, values)])
        o_ref[0, :, hs] = (acc / l).astype(o_ref.dtype)


def _swa_sample(projp, cache_k, cache_v, *, tn):
    b, tp, _ = projp.shape
    rows = cache_k.shape[1]
    w = rows // SWA_HEADS
    group_rows = SAMPLE_GROUP * SWA_HEADS
    near_rows = SAMPLE_NEAR * SWA_HEADS
    assert w % SAMPLE_NEAR == 0 and SAMPLE_NEAR % SAMPLE_GROUP == 0 and tn <= SAMPLE_GROUP
    assert [wd for wd, _ in SWA_PATTERNS][:2] == [SWA_BLOCK, SAMPLE_NEAR] and w >= SWA_PATTERNS[2][0]
    far_groups = (w - SAMPLE_NEAR) // SAMPLE_GROUP
    grouped = (b, rows // group_rows, group_rows, SWA_HD)

    def new(cb):
        return pl.BlockSpec((1, tp, SWA_W), lambda bi: (bi, 0, cb))

    far = pl.BlockSpec((1, far_groups, tn * SWA_HEADS, SWA_HD), lambda bi: (bi, 0, 0, 0))
    near = pl.BlockSpec((1, near_rows, SWA_HD), lambda bi: (bi, rows // near_rows - 1, 0))
    return pl.pallas_call(
        functools.partial(_swa_sample_kernel, w=w, tn=tn),
        grid=(b,),
        in_specs=[new(0), new(1), new(2), far, far, near, near],
        out_specs=pl.BlockSpec((1, tp, SWA_W), lambda bi: (bi, 0, 0)),
        out_shape=jax.ShapeDtypeStruct((b, tp, SWA_W), BF16),
        compiler_params=_params(("parallel",)),
        name="swa_sample",
    )(projp, projp, projp, cache_k.reshape(grouped), cache_v.reshape(grouped), cache_k, cache_v)


def _outproj_kernel(x_ref, oa_ref, ob_ref, wa_ref, wb_ref, h_ref):
    h_ref[...] = x_ref[...] + _dot(oa_ref[...], wa_ref[...]) + _dot(ob_ref[...], wb_ref[...])


def _outproj(x, o_a, o_b, w_out, *, tm):
    m, d = x.shape
    return pl.pallas_call(
        _outproj_kernel,
        grid=(m // tm,),
        in_specs=[pl.BlockSpec((tm, d), lambda i: (i, 0)),
                  pl.BlockSpec((tm, GLA_V_W), lambda i: (i, 0)),
                  pl.BlockSpec((tm, SWA_W), lambda i: (i, 0)),
                  pl.BlockSpec((GLA_V_W, d), lambda i: (0, 0)),
                  pl.BlockSpec((SWA_W, d), lambda i: (GLA_V_W // SWA_W, 0))],
        out_specs=pl.BlockSpec((tm, d), lambda i: (i, 0)),
        out_shape=jax.ShapeDtypeStruct((m, d), F32),
        compiler_params=_params(("parallel",)),
        name="outproj",
    )(x, o_a, o_b, w_out, w_out)


FFN_HALO = 16
FFN_TF = 512
CACHE_PIECES = 4


def _ffn_prompt_kernel(h_ref, halo_ref, g_ref, wa_ref, wv_ref, cp_ref, wd_ref, gf_ref, kc_hbm, vc_hbm, kn_ref,
                       vn_ref, y_ref, u_ref, ko_hbm, vo_hbm, xn_scr, ea_scr, ev_scr, ks_scr, vs_scr, sems,
                       *, tiles_per_seq):
    i, j = pl.program_id(0), pl.program_id(1)
    tm = h_ref.shape[0]
    last_j = pl.num_programs(1) - 1

    nseq, rows = kc_hbm.shape[0], kc_hbm.shape[1]
    new = kn_ref.shape[1]
    old = rows - new
    piece = old // CACHE_PIECES
    moving = i < nseq

    def load(src_hbm, dst_scr, c, sem):
        part = pl.ds(c * piece, piece)
        return pltpu.make_async_copy(src_hbm.at[i, pl.ds(new + c * piece, piece)], dst_scr.at[part], sem)

    def store(src_scr, dst_hbm, c, sem):
        part = pl.ds(c * piece, piece)
        return pltpu.make_async_copy(src_scr.at[part], dst_hbm.at[i, part], sem)

    loads = [[load(kc_hbm, ks_scr, c, sems.at[0]), load(vc_hbm, vs_scr, c, sems.at[1])]
             for c in range(CACHE_PIECES)]
    stores = [[store(ks_scr, ko_hbm, c, sems.at[2]), store(vs_scr, vo_hbm, c, sems.at[3])]
              for c in range(CACHE_PIECES)]
    stores[0] += [pltpu.make_async_copy(kn_ref.at[0], ko_hbm.at[i, pl.ds(old, new)], sems.at[4]),
                  pltpu.make_async_copy(vn_ref.at[0], vo_hbm.at[i, pl.ds(old, new)], sems.at[5])]

    for c in range(CACHE_PIECES):
        @pl.when(moving & (j == c))
        def _():
            for cp in loads[c]:
                cp.start(priority=1)

        @pl.when(moving & (j == CACHE_PIECES + c))
        def _():
            if c == 0:
                for group in loads:
                    for cp in group:
                        cp.wait()
            for cp in stores[c]:
                cp.start(priority=1)

    @pl.when(j == 0)
    def _():
        first = (i % tiles_per_seq) == 0
        halo = _rms_rows(halo_ref[...], g_ref[...])
        xn_scr[0:FFN_HALO, :] = jnp.where(first, 0.0, halo).astype(BF16)
        xn_scr[FFN_HALO:, :] = _rms_rows(h_ref[...], g_ref[...]).astype(BF16)
        y_ref[...] = h_ref[...]

    xn = xn_scr[...]
    ea_scr[...] = _dot(xn, wa_ref[...])
    ev_scr[...] = _dot(xn, wv_ref[...])

    def conv(e_scr, part):
        out = cp_ref[part, CONV_W:CONV_W + 1, :]
        for tap in range(CONV_W):
            off = FFN_HALO - (CONV_W - 1) + tap
            out = out + cp_ref[part, tap:tap + 1, :] * e_scr[off:off + tm, :]
        return out

    gate = (_silu(conv(ea_scr, 0)) * conv(ev_scr, 1)).astype(BF16)
    y_ref[...] += _dot(gate, wd_ref[...])
    u_ref[0, j, 0] = ea_scr[FFN_HALO + tm - 8:FFN_HALO + tm, :]
    u_ref[0, j, 1] = ev_scr[FFN_HALO + tm - 8:FFN_HALO + tm, :]

    @pl.when(j == last_j)
    def _():
        y_ref[...] = _rms_rows(y_ref[...], gf_ref[...])

    @pl.when(moving & (j == last_j))
    def _():
        for group in stores:
            for cp in group:
                cp.wait()


def _ffn_prompt(h, g_ffn, w_up, conv_params, w_down, g_final, cache_k, cache_v, k_new, v_new, *, seq, tm):
    m, d = h.shape
    f = w_down.shape[0]
    tf = conv_params.shape[-1]
    nf = f // tf
    ntile = m // tm
    hb = tm // FFN_HALO
    nseq, rows, _ = cache_k.shape
    new = k_new.shape[1]
    assert nseq <= ntile and nf > 2 * CACHE_PIECES, "one cache per row tile, moved within its column steps"
    assert (rows - new) % (8 * CACHE_PIECES) == 0
    hbm = pl.BlockSpec(memory_space=pl.ANY)
    fresh = pl.BlockSpec((1, new, SWA_HD), lambda i, j: (jnp.minimum(i, nseq - 1), 0, 0))
    y, u_last, k_all, v_all = pl.pallas_call(
        functools.partial(_ffn_prompt_kernel, tiles_per_seq=seq // tm),
        grid=(ntile, nf),
        in_specs=[
            pl.BlockSpec((tm, d), lambda i, j: (i, 0)),
            pl.BlockSpec((FFN_HALO, d), lambda i, j: (jnp.maximum(i * hb - 1, 0), 0)),
            pl.BlockSpec((1, d), lambda i, j: (0, 0)),
            pl.BlockSpec((d, tf), lambda i, j: (0, j)),
            pl.BlockSpec((d, tf), lambda i, j: (0, nf + j)),
            pl.BlockSpec((None, 2, 8, tf), lambda i, j: (j, 0, 0, 0)),
            pl.BlockSpec((tf, d), lambda i, j: (j, 0)),
            pl.BlockSpec((1, d), lambda i, j: (0, 0)),
            hbm, hbm, fresh, fresh,
        ],
        out_specs=[
            pl.BlockSpec((tm, d), lambda i, j: (i, 0)),
            pl.BlockSpec((1, nf, 2, 8, tf), lambda i, j: (i, 0, 0, 0, 0)),
            hbm, hbm,
        ],
        out_shape=[jax.ShapeDtypeStruct((m, d), F32),
                   jax.ShapeDtypeStruct((ntile, nf, 2, 8, tf), F32),
                   jax.ShapeDtypeStruct(cache_k.shape, cache_k.dtype),
                   jax.ShapeDtypeStruct(cache_v.shape, cache_v.dtype)],
        scratch_shapes=[pltpu.VMEM((tm + FFN_HALO, d), BF16),
                        pltpu.VMEM((tm + FFN_HALO, tf), F32),
                        pltpu.VMEM((tm + FFN_HALO, tf), F32),
                        pltpu.VMEM((rows - new, SWA_HD), F32),
                        pltpu.VMEM((rows - new, SWA_HD), F32),
                        pltpu.SemaphoreType.DMA((6,))],
        compiler_params=_params(("arbitrary", "arbitrary")),
        name="convffn_prompt",
    )(h, h, g_ffn, w_up, w_up, conv_params, w_down, g_final, cache_k, cache_v, k_new, v_new)
    u_last = u_last.transpose(0, 3, 2, 1, 4).reshape(ntile, 8, 2 * f)
    return y, u_last, k_all, v_all


def _ffn_sample_kernel(h_ref, g_ref, sa_ref, sv_ref, wa_ref, wv_ref, ca_ref, cv_ref, wd_ref,
                       gf_ref, y_ref, na_ref, nv_ref, xn_scr, acc_scr, *, nb):
    j = pl.program_id(0)
    m = h_ref.shape[0]
    keep = (CONV_W - 1) * nb

    @pl.when(j == 0)
    def _():
        xn_scr[...] = _rms_rows(h_ref[...], g_ref[...]).astype(BF16)
        acc_scr[...] = jnp.zeros_like(acc_scr)

    xn = xn_scr[...]

    def conv(s_ref, w_ref, c_ref, n_ref):
        u = _dot(xn, w_ref[...])
        ext = jnp.concatenate([s_ref[...], u], axis=0)
        n_ref[...] = ext[m:m + keep, :]
        out = c_ref[CONV_W:CONV_W + 1, :]
        for tap in range(CONV_W):
            out = out + c_ref[tap:tap + 1, :] * ext[tap * nb:tap * nb + m, :]
        return out

    gate = (_silu(conv(sa_ref, wa_ref, ca_ref, na_ref)) * conv(sv_ref, wv_ref, cv_ref, nv_ref)).astype(BF16)
    acc_scr[...] += _dot(gate, wd_ref[...])

    @pl.when(j == pl.num_programs(0) - 1)
    def _():
        y_ref[...] = _rms_rows(h_ref[...] + acc_scr[...], gf_ref[...])


def _ffn_sample(h_tm, state_tm, g_ffn, w_up, conv_params, w_down, g_final, *, nb):
    m, d = h_tm.shape
    f = w_down.shape[0]
    tf = conv_params.shape[-1]
    nf = f // tf
    keep = (CONV_W - 1) * nb
    return pl.pallas_call(
        functools.partial(_ffn_sample_kernel, nb=nb),
        grid=(nf,),
        in_specs=[
            pl.BlockSpec((m, d), lambda j: (0, 0)),
            pl.BlockSpec((1, d), lambda j: (0, 0)),
            pl.BlockSpec((keep, tf), lambda j: (0, j)),
            pl.BlockSpec((keep, tf), lambda j: (0, nf + j)),
            pl.BlockSpec((d, tf), lambda j: (0, j)),
            pl.BlockSpec((d, tf), lambda j: (0, nf + j)),
            pl.BlockSpec((None, None, 8, tf), lambda j: (j, 0, 0, 0)),
            pl.BlockSpec((None, None, 8, tf), lambda j: (j, 1, 0, 0)),
            pl.BlockSpec((tf, d), lambda j: (j, 0)),
            pl.BlockSpec((1, d), lambda j: (0, 0)),
        ],
        out_specs=[
            pl.BlockSpec((m, d), lambda j: (0, 0)),
            pl.BlockSpec((keep, tf), lambda j: (0, j)),
            pl.BlockSpec((keep, tf), lambda j: (0, j)),
        ],
        out_shape=[jax.ShapeDtypeStruct((m, d), F32),
                   jax.ShapeDtypeStruct((keep, f), F32),
                   jax.ShapeDtypeStruct((keep, f), F32)],
        scratch_shapes=[pltpu.VMEM((m, d), BF16), pltpu.VMEM((m, d), F32)],
        compiler_params=_params(("arbitrary",)),
        name="convffn_sample",
    )(h_tm, g_ffn, state_tm, state_tm, w_up, w_up, conv_params, conv_params, w_down, g_final)


def _pack_weights(g_attn_norm, w_in, w_gate_up, b_gate, g_gla_norm, w_out, g_ffn_norm, w_up, w_conv, b_conv,
                  w_down):
    z0 = 2 * GLA_QK_W + 2 * GLA_V_W
    z1 = z0 + GLA_GATE_RANK
    w_main = jnp.concatenate([w_in[:, :z0], w_in[:, z1:]], axis=1).astype(BF16)
    w_z = jnp.pad(w_in[:, z0:z1], ((0, 0), (0, LANE - GLA_GATE_RANK))).astype(BF16)
    w_gate = jnp.pad(w_gate_up, ((0, LANE - GLA_GATE_RANK), (0, 0))).astype(BF16)
    conv_rows = jnp.concatenate([w_conv, b_conv[None, :], jnp.zeros((8 - CONV_W - 1, b_conv.shape[0]), F32)])
    conv_params = conv_rows.reshape(8, 2, -1, FFN_TF).transpose(2, 1, 0, 3)
    return dict(
        g_attn=g_attn_norm[None, :], w_main=w_main, w_z=w_z, w_gate=w_gate, b_gate=b_gate[None, :],
        g_gla=g_gla_norm[None, :], w_out=w_out.astype(BF16), g_ffn=g_ffn_norm[None, :],
        w_up=w_up.astype(BF16), conv_params=conv_params, w_down=w_down.astype(BF16))


def _prompt_layer(x, p, g_final, sample_cache):
    b, t, d = x.shape
    m = b * t
    f = p["w_down"].shape[0]
    keep = min(SWA_MAX_WINDOW, t)
    x2 = x.reshape(m, d)
    proj, loga, qkv, k_buf, v_buf = _inproj(x2, p["g_attn"], p["w_main"], p["w_z"], p["w_gate"], p["b_gate"],
                                            tm=1024, seq=t, keep=keep)
    o_a, s_fin = _gla(proj.reshape(b, t, GLA_COLS), loga.reshape(b, t, GLA_QK_W), p["g_gla"], None,
                      chunk=GLA_CHUNK, tc=512)
    o_b = _swa_prompt(qkv, b, t)
    h = _outproj(x2, o_a.reshape(m, GLA_V_W), o_b, p["w_out"], tm=512)
    tm = 512
    y, u_last, k_all, v_all = _ffn_prompt(h, p["g_ffn"], p["w_up"], p["conv_params"], p["w_down"], g_final,
                                          *sample_cache, seq=t, tm=tm)
    shape = (b, keep, SWA_HEADS, SWA_HD)
    last = u_last.reshape(b, t // tm, 8, 2 * f)[:, -1, 8 - (CONV_W - 1):, :]
    return y.reshape(b, t, d), s_fin, k_buf.reshape(shape), v_buf.reshape(shape), last, k_all, v_all


def _sample_layer(x, s0, conv_s0, cache_k, cache_v, p, g_final):
    b, tn, d = x.shape
    m = b * tn
    tp = 8
    w = cache_k.shape[1]
    x2 = x.reshape(m, d)
    proj, loga, qkv, k_new, v_new = _inproj(x2, p["g_attn"], p["w_main"], p["w_z"], p["w_gate"], p["b_gate"],
                                            tm=m, seq=m, keep=m)
    pad = ((0, 0), (0, tp - tn), (0, 0))
    projp = jnp.pad(proj.reshape(b, tn, GLA_COLS), pad)
    logap = jnp.pad(loga.reshape(b, tn, GLA_QK_W), pad)
    o_a, s_fin = _gla(projp, logap, p["g_gla"], s0, chunk=tp, tc=tp)
    qkvp = jnp.pad(qkv.transpose(2, 0, 1, 3).reshape(b, tn, 3 * SWA_W).astype(BF16), pad)
    rows = (b, tn * SWA_HEADS, SWA_HD)
    cache = (cache_k.reshape(b, w * SWA_HEADS, SWA_HD), cache_v.reshape(b, w * SWA_HEADS, SWA_HD),
             k_new.reshape(rows), v_new.reshape(rows))
    o_b = _swa_sample(qkvp, cache[0], cache[1], tn=tn)
    h = _outproj(x2, o_a[:, :tn].reshape(m, GLA_V_W), o_b[:, :tn].reshape(m, SWA_W), p["w_out"], tm=m)
    h_tm = h.reshape(b, tn, d).transpose(1, 0, 2).reshape(m, d)
    state_tm = conv_s0.transpose(1, 0, 2).reshape((CONV_W - 1) * b, -1)
    y_tm, na, nv = _ffn_sample(h_tm, state_tm, p["g_ffn"], p["w_up"], p["conv_params"], p["w_down"], g_final, nb=b)
    y = y_tm.reshape(tn, b, d).transpose(1, 0, 2)
    conv_new = jnp.concatenate([na, nv], axis=-1).reshape(CONV_W - 1, b, -1).transpose(1, 0, 2)
    return y, s_fin, conv_new, cache


def kernel(x_prompt, x_sample, state_gla, cache_swa_k, cache_swa_v, state_ffn_conv, g_attn_norm, w_in, w_gate_up,
           b_gate, g_gla_norm, w_out, g_ffn_norm, w_up, w_conv, b_conv, w_down, g_final):
    depth = w_in.shape[0]
    assert depth == 1, "the final RMSNorm is fused into the last layer's ConvFFN kernel"
    assert cache_swa_k.shape[2] == SWA_MAX_WINDOW and x_sample.shape[1] <= 8
    p = _pack_weights(g_attn_norm[0], w_in[0], w_gate_up[0], b_gate[0], g_gla_norm[0], w_out[0], g_ffn_norm[0],
                      w_up[0], w_conv[0], b_conv[0], w_down[0])
    gf = g_final[None, :]
    ys, s2, c2, sample_cache = _sample_layer(x_sample, state_gla[0], state_ffn_conv[0], cache_swa_k[0],
                                             cache_swa_v[0], p, gf)
    yp, s1, k1, v1, c1, k2, v2 = _prompt_layer(x_prompt, p, gf, sample_cache)
    k2, v2 = k2.reshape(cache_swa_k.shape[1:]), v2.reshape(cache_swa_v.shape[1:])
    return (yp, ys, s1[None], s2[None], k1[None], k2[None], v1[None], v2[None], c1[None], c2[None])
```

```python
import functools

import jax
import jax.numpy as jnp
from jax import lax
from jax.experimental import pallas as pl
from jax.experimental.pallas import tpu as pltpu

F32 = jnp.float32
BF16 = jnp.bfloat16

GLA_HEADS = 4
GLA_DK = 128
GLA_DV = 256
GLA_GATE_RANK = 16
GLA_TAU = 16.0
GLA_CHUNK = 64
SWA_HEADS = 8
SWA_HD = 128
SWA_PATTERNS = ((128, 1), (512, 4), (2048, 16))
SWA_BLOCK = 128
SWA_MAX_WINDOW = 2048
CONV_W = 3
EPS = 1e-6
NEG = -1e30

GLA_QK_W = GLA_HEADS * GLA_DK
GLA_V_W = GLA_HEADS * GLA_DV
SWA_W = SWA_HEADS * SWA_HD
PROJ_W = 2 * GLA_QK_W + 2 * GLA_V_W + 3 * SWA_W
LANE = 128
VMEM_LIMIT = 56 * 1024 * 1024

ALIBI_SLOPES = tuple(2.0 ** (-8.0 * (h + 1) / SWA_HEADS) for h in range(SWA_HEADS))


def _params(semantics):
    return pltpu.CompilerParams(dimension_semantics=semantics, vmem_limit_bytes=VMEM_LIMIT)


def _rms_rows(x, g):
    return x * lax.rsqrt(jnp.mean(x * x, axis=-1, keepdims=True) + EPS) * g


def _silu(x):
    return x / (1.0 + jnp.exp(-x))


def _log_sigmoid(x):
    return jnp.minimum(x, 0.0) - jnp.log1p(jnp.exp(-jnp.abs(x)))


def _dot(a, b):
    return jnp.dot(a, b, preferred_element_type=F32)


def _dot_nt(a, b):
    return lax.dot_general(a, b, (((1,), (1,)), ((), ())), preferred_element_type=F32)


def _dot_tn(a, b):
    return lax.dot_general(a, b, (((0,), (0,)), ((), ())), preferred_element_type=F32)


GLA_COLS = 2 * GLA_QK_W + 2 * GLA_V_W
GLA_STEPS = GLA_COLS // SWA_W


def _inproj_kernel(x_hbm, g_ref, w_ref, wz_ref, wg_ref, bg_ref, proj_ref, loga_ref, qkv_ref, kc_hbm, vc_hbm,
                   x_buf, xn_ref, kc_buf, vc_buf, sem, *, tiles_per_seq, kept_tiles):
    i, j = pl.program_id(0), pl.program_id(1)
    tm = x_buf.shape[0]
    last_i = pl.num_programs(0) - 1

    def x_copy(tile):
        return pltpu.make_async_copy(x_hbm.at[pl.ds(tile * tm, tm)], x_buf, sem.at[0])

    def kept(tile):
        return tile % tiles_per_seq >= tiles_per_seq - kept_tiles

    def cache_copy(buf, out_hbm, tile, s):
        block = (tile // tiles_per_seq) * kept_tiles + tile % tiles_per_seq - (tiles_per_seq - kept_tiles)
        rows = tm * SWA_HEADS
        return pltpu.make_async_copy(buf, out_hbm.at[pl.ds(block * rows, rows)], sem.at[s])

    @pl.when((j == 1) & (i > 0) & kept(i - 1))
    def _():
        cache_copy(vc_buf, vc_hbm, i - 1, 2).wait()

    @pl.when(j == 0)
    def _():
        @pl.when(i == 0)
        def _():
            x_copy(0).start(priority=1)

        x_copy(i).wait()
        xn_ref[...] = _rms_rows(x_buf[...], g_ref[...]).astype(BF16)

        @pl.when(i < last_i)
        def _():
            x_copy(i + 1).start(priority=1)

        za = _dot(xn_ref[...], wz_ref[...])
        gate = _dot(za.astype(BF16), wg_ref[...]) + bg_ref[...]
        loga_ref[...] = _log_sigmoid(gate) / GLA_TAU

    @pl.when(j < GLA_STEPS)
    def _():
        proj_ref[...] = _dot(xn_ref[...], w_ref[...]).astype(BF16)

    def swa_step(cache):
        res = _dot(xn_ref[...], w_ref[...])
        for h in range(SWA_HEADS):
            qkv_ref[0, h] = res[:, h * SWA_HD:(h + 1) * SWA_HD]
        if cache is not None:
            buf, out_hbm, s = cache

            @pl.when(kept(i))
            def _():
                for h in range(SWA_HEADS):
                    buf[pl.ds(h, tm, stride=SWA_HEADS), :] = res[:, h * SWA_HD:(h + 1) * SWA_HD]
                cache_copy(buf, out_hbm, i, s).start(priority=1)

    for step, cache in enumerate((None, (kc_buf, kc_hbm, 1), (vc_buf, vc_hbm, 2))):
        pl.when(j == GLA_STEPS + step)(functools.partial(swa_step, cache))

    @pl.when((j == GLA_STEPS + 2) & kept(i))
    def _():
        cache_copy(kc_buf, kc_hbm, i, 1).wait()

        @pl.when(i == last_i)
        def _():
            cache_copy(vc_buf, vc_hbm, i, 2).wait()


def _inproj(x, g, w_main, w_z, w_gate, b_gate, *, tm, seq, keep):
    m, d = x.shape
    tn = SWA_W
    tps, kt = seq // tm, keep // tm
    assert seq % tm == 0 and keep % tm == 0 and m % seq == 0

    cache_rows = (m // seq) * keep * SWA_HEADS
    hbm = pl.BlockSpec(memory_space=pl.ANY)
    return pl.pallas_call(
        functools.partial(_inproj_kernel, tiles_per_seq=tps, kept_tiles=kt),
        grid=(m // tm, PROJ_W // tn),
        in_specs=[
            pl.BlockSpec(memory_space=pl.ANY),
            pl.BlockSpec((1, d), lambda i, j: (0, 0)),
            pl.BlockSpec((d, tn), lambda i, j: (0, j)),
            pl.BlockSpec((d, LANE), lambda i, j: (0, 0)),
            pl.BlockSpec((LANE, GLA_QK_W), lambda i, j: (0, 0)),
            pl.BlockSpec((1, GLA_QK_W), lambda i, j: (0, 0)),
        ],
        out_specs=[
            pl.BlockSpec((tm, tn), lambda i, j: (i, jnp.minimum(j, GLA_STEPS - 1))),
            pl.BlockSpec((tm, GLA_QK_W), lambda i, j: (i, 0)),
            pl.BlockSpec((1, SWA_HEADS, tm, SWA_HD), lambda i, j: (jnp.maximum(j - GLA_STEPS, 0), 0, i, 0)),
            hbm, hbm,
        ],
        out_shape=[jax.ShapeDtypeStruct((m, GLA_COLS), BF16),
                   jax.ShapeDtypeStruct((m, GLA_QK_W), F32),
                   jax.ShapeDtypeStruct((3, SWA_HEADS, m, SWA_HD), F32),
                   jax.ShapeDtypeStruct((cache_rows, SWA_HD), F32),
                   jax.ShapeDtypeStruct((cache_rows, SWA_HD), F32)],
        scratch_shapes=[pltpu.VMEM((tm, d), F32), pltpu.VMEM((tm, d), BF16),
                        pltpu.VMEM((tm * SWA_HEADS, SWA_HD), F32), pltpu.VMEM((tm * SWA_HEADS, SWA_HD), F32),
                        pltpu.SemaphoreType.DMA((3,))],
        compiler_params=_params(("arbitrary", "arbitrary")),
        name="norm_inproj",
    )(x, g, w_main, w_z, w_gate, b_gate)


def _split3(x):
    hi = x.astype(BF16)
    r1 = x - hi.astype(F32)
    mid = r1.astype(BF16)
    lo = (r1 - mid.astype(F32)).astype(BF16)
    return hi, mid, lo


def _gla_kernel(*refs, chunk, nsub, has_s0):
    scratch = refs[-5:]
    s_scr, qd_scr, oi_scr, u_scr, dec_scr = scratch
    if has_s0:
        q_ref, k_ref, v_ref, ga_ref, la_ref, gn_ref, s0_ref, o_ref, sfin_ref = refs[:-5]
    else:
        q_ref, k_ref, v_ref, ga_ref, la_ref, gn_ref, o_ref, sfin_ref = refs[:-5]
    c = pl.program_id(1)

    @pl.when(c == 0)
    def _():
        if has_s0:
            s_scr[...] = s0_ref[0]
        else:
            s_scr[...] = jnp.zeros_like(s_scr)

    row = lax.broadcasted_iota(jnp.int32, (chunk, chunk), 0)
    col = lax.broadcasted_iota(jnp.int32, (chunk, chunk), 1)
    causal = row >= col
    tril = jnp.where(causal, 1.0, 0.0).astype(BF16)
    eye = (lax.broadcasted_iota(jnp.int32, (GLA_DK, GLA_DK), 0)
           == lax.broadcasted_iota(jnp.int32, (GLA_DK, GLA_DK), 1))

    group = min(nsub, 4)
    heads = range(GLA_HEADS)
    ks = [slice(h * GLA_DK, (h + 1) * GLA_DK) for h in heads]
    vs = [slice(h * GLA_DV, (h + 1) * GLA_DV) for h in heads]

    def local(c2, carry):
        subs = [c2 * group + a for a in range(group)]
        sls = [pl.ds(pl.multiple_of(i * chunk, chunk), chunk) for i in subs]
        pieces = [_split3(la_ref[0, sl, :]) for sl in sls]
        cums = [_dot(tril, hi) + _dot(tril, mid) + _dot(tril, lo) for hi, mid, lo in pieces]
        q_dec, k_dec, k_end, grow = [], [], [], []
        for sl, cum in zip(sls, cums):
            total = cum[chunk - 1:chunk, :]
            k = k_ref[0, sl, :].astype(F32)
            q_dec.append((q_ref[0, sl, :].astype(F32) * (GLA_DK ** -0.5) * jnp.exp(cum)).astype(BF16))
            k_dec.append((k * jnp.exp(-cum)).astype(BF16))
            k_end.append((k * jnp.exp(total - cum)).astype(BF16))
            grow.append(jnp.exp(total))
            qd_scr[sl, :] = q_dec[-1]
        units = [(a, h) for a in range(group) for h in heads]
        vals = {(a, h): v_ref[0, sls[a], vs[h]] for a, h in units}
        raw = {(a, h): _dot_nt(q_dec[a][:, ks[h]], k_dec[a][:, ks[h]]) for a, h in units}
        for a, h in units:
            u_scr[subs[a], h] = _dot_tn(k_end[a][:, ks[h]], vals[a, h])
        attn = {(a, h): jnp.where(causal, raw[a, h], 0.0).astype(BF16) for a, h in units}
        for a, h in units:
            oi_scr[sls[a], vs[h]] = _dot(attn[a, h], vals[a, h])
        for a, h in units:
            decay = jnp.sum(jnp.where(eye, grow[a][:, ks[h]], 0.0), axis=1, keepdims=True)
            dec_scr[subs[a], h] = jnp.broadcast_to(decay, (GLA_DK, GLA_DK))
        return carry

    if nsub == group:
        local(0, 0)
    else:
        lax.fori_loop(0, nsub // group, local, 0)

    for i in range(nsub):
        sl = slice(i * chunk, (i + 1) * chunk)
        states = [s_scr[h] for h in heads]
        inter = [_dot(qd_scr[sl, ks[h]], states[h].astype(BF16)) for h in heads]
        for h in heads:
            decay = dec_scr[i, h]
            s_scr[h] = states[h] * jnp.concatenate([decay] * (GLA_DV // GLA_DK), axis=1) + u_scr[i, h]
        for h in heads:
            on = _rms_rows(oi_scr[sl, vs[h]] + inter[h], gn_ref[:, vs[h]])
            o_ref[0, sl, vs[h]] = (on * _silu(ga_ref[0, sl, vs[h]].astype(F32))).astype(o_ref.dtype)

    @pl.when(c == pl.num_programs(1) - 1)
    def _():
        sfin_ref[0] = s_scr[...]


def _gla(proj3, loga3, g_norm, s0, *, chunk, tc):
    b, t, _ = proj3.shape
    nsub = tc // chunk
    state = pl.BlockSpec((1, GLA_HEADS, GLA_DK, GLA_DV), lambda bi, c: (bi, 0, 0, 0))
    in_specs = [
        pl.BlockSpec((1, tc, GLA_QK_W), lambda bi, c: (bi, c, 0)),
        pl.BlockSpec((1, tc, GLA_QK_W), lambda bi, c: (bi, c, 1)),
        pl.BlockSpec((1, tc, GLA_V_W), lambda bi, c: (bi, c, 1)),
        pl.BlockSpec((1, tc, GLA_V_W), lambda bi, c: (bi, c, 2)),
        pl.BlockSpec((1, tc, GLA_QK_W), lambda bi, c: (bi, c, 0)),
        pl.BlockSpec((1, GLA_V_W), lambda bi, c: (0, 0)),
    ]
    args = [proj3, proj3, proj3, proj3, loga3, g_norm]
    if s0 is not None:
        in_specs.append(state)
        args.append(s0)
    return pl.pallas_call(
        functools.partial(_gla_kernel, chunk=chunk, nsub=nsub, has_s0=s0 is not None),
        grid=(b, t // tc),
        in_specs=in_specs,
        out_specs=[pl.BlockSpec((1, tc, GLA_V_W), lambda bi, c: (bi, c, 0)), state],
        out_shape=[jax.ShapeDtypeStruct((b, t, GLA_V_W), BF16),
                   jax.ShapeDtypeStruct((b, GLA_HEADS, GLA_DK, GLA_DV), F32)],
        scratch_shapes=[pltpu.VMEM((GLA_HEADS, GLA_DK, GLA_DV), F32),
                        pltpu.VMEM((tc, GLA_QK_W), BF16),
                        pltpu.VMEM((tc, GLA_V_W), F32),
                        pltpu.VMEM((nsub, GLA_HEADS, GLA_DK, GLA_DV), F32),
                        pltpu.VMEM((nsub, GLA_HEADS, GLA_DK, GLA_DK), F32)],
        compiler_params=_params(("parallel", "arbitrary")),
        name="gla",
    )(*args)


SWA_SUPER = 2048
SWA_TILES = 16
SWA_GROUP = 8
QUAD = 4


def _swa_prompt_kernel(q_ref, k_ref, v_ref, o_ref, x4, q1, q4, q16, k1, k4, k16, v1, v4, v16, ob, lb, bias):
    n = SWA_BLOCK
    s_tok = q_ref.shape[0]
    l4, l16 = s_tok // QUAD, s_tok // (QUAD * QUAD)
    head, has_prev = pl.program_id(1), pl.program_id(2) > 0
    scale = SWA_HD ** -0.5

    slope = ALIBI_SLOPES[0]
    for h in range(1, SWA_HEADS):
        slope = jnp.where(head == h, ALIBI_SLOPES[h], slope)
    qi = lax.broadcasted_iota(jnp.int32, (n, 2 * n), 0)
    kj = lax.broadcasted_iota(jnp.int32, (n, 2 * n), 1)
    steps = qi + n - kj
    visible = (steps >= 0) & (steps <= n)
    first_visible = visible & (has_prev | (kj >= n))
    for g, (_, dil) in enumerate(SWA_PATTERNS):
        alibi = (-float(dil) * slope) * steps.astype(F32)
        bias[2 * g] = jnp.where(visible, alibi, NEG)
        bias[2 * g + 1] = jnp.where(first_visible, alibi, NEG)

    for buf, length in ((k1, s_tok), (v1, s_tok), (k4, l4), (v4, l4), (k16, l16), (v16, l16)):
        lead = buf.shape[:-2]
        idx = (slice(None),) * len(lead)

        @pl.when(has_prev)
        def _():
            buf[idx + (slice(0, n),)] = buf[idx + (slice(length, length + n),)]

        @pl.when(jnp.logical_not(has_prev))
        def _():
            buf[idx + (slice(0, n),)] = jnp.zeros(lead + (n, SWA_HD), BF16)

    for t, src in enumerate((q_ref, k_ref, v_ref)):
        for r in range(QUAD):
            x4[t, r] = src[pl.ds(r, l4, stride=QUAD), :]
    q1[...] = q_ref[...].astype(BF16)
    k1[n:, :] = k_ref[...].astype(BF16)
    v1[n:, :] = v_ref[...].astype(BF16)
    for r in range(QUAD):
        q4[r] = x4[0, r].astype(BF16)
        k4[r, n:, :] = x4[1, r].astype(BF16)
        v4[r, n:, :] = x4[2, r].astype(BF16)
    for r in range(QUAD * QUAD):
        sub = pl.ds(r // QUAD, l16, stride=QUAD)
        q16[r] = x4[0, r % QUAD, sub, :].astype(BF16)
        k16[r, n:, :] = x4[1, r % QUAD, sub, :].astype(BF16)
        v16[r, n:, :] = x4[2, r % QUAD, sub, :].astype(BF16)

    def run_tiles(tiles):
        for g0 in range(0, len(tiles), SWA_GROUP):
            group = tiles[g0:g0 + SWA_GROUP]
            scores = [_dot_nt(q(), keys()) * scale + table() for q, keys, _, table, _, _ in group]
            probs, lses = [], []
            for s in scores:
                m = jnp.max(s, axis=1, keepdims=True)
                p = jnp.exp(s - m)
                l = jnp.sum(p, axis=1, keepdims=True)
                probs.append(p.astype(BF16))
                lses.append((1.0 / l, m + jnp.log(l)))
            accs = [_dot(p, tile[2]()) for p, tile in zip(probs, group)]
            for acc, (inv, lse), (_, _, _, _, branch, rows) in zip(accs, lses, group):
                ob[branch, rows, :] = acc * inv
                lb[branch, rows, :] = jnp.broadcast_to(lse, (n, SWA_HD))

    def window(ref, lead, start, size):
        return lambda: ref[lead + (pl.ds(start, size), slice(None))]

    def d1_body(c, carry):
        tiles = []
        for a in range(SWA_TILES):
            i = c * SWA_TILES + a
            row0 = pl.multiple_of(i * n, n)
            tiles.append((window(q1, (), row0, n), window(k1, (), row0, 2 * n), window(v1, (), row0, 2 * n),
                          lambda i=i: bias[jnp.where(i == 0, 1, 0)], 0, pl.ds(row0, n)))
        run_tiles(tiles)
        return carry

    lax.fori_loop(0, s_tok // n // SWA_TILES, d1_body, 0)

    per4 = l4 // n

    def d4_body(c, carry):
        tiles = []
        for a in range(SWA_TILES // per4):
            r = c * (SWA_TILES // per4) + a
            for i in range(per4):
                tiles.append((window(q4, (r,), i * n, n), window(k4, (r,), i * n, 2 * n),
                              window(v4, (r,), i * n, 2 * n), lambda i=i: bias[2 + (1 if i == 0 else 0)], 1,
                              pl.ds(pl.multiple_of(r * l4 + i * n, n), n)))
        run_tiles(tiles)
        return carry

    lax.fori_loop(0, QUAD * per4 // SWA_TILES, d4_body, 0)

    per16 = l16 // n

    def d16_body(c, carry):
        tiles = []
        for a in range(SWA_TILES // per16):
            r4 = a % QUAD
            j = c * (SWA_TILES // per16 // QUAD) + a // QUAD
            r = r4 + QUAD * j
            for i in range(per16):
                tiles.append((window(q16, (r,), i * n, n), window(k16, (r,), i * n, 2 * n),
                              window(v16, (r,), i * n, 2 * n), lambda i=i: bias[4 + (1 if i == 0 else 0)], 2,
                              pl.ds(r4 * l4 + QUAD * i * n + j, n, stride=QUAD)))
        run_tiles(tiles)
        return carry

    lax.fori_loop(0, QUAD * QUAD * per16 // SWA_TILES, d16_body, 0)

    def combine_body(c, carry):
        for r in range(QUAD):
            tok = pl.ds(r + QUAD * c * n, n, stride=QUAD)
            cls = pl.ds(pl.multiple_of(r * l4 + c * n, n), n)
            lses = [lb[0, tok, :], lb[1, cls, :], lb[2, cls, :]]
            outs = [ob[0, tok, :], ob[1, cls, :], ob[2, cls, :]]
            m = jnp.maximum(jnp.maximum(lses[0], lses[1]), lses[2])
            es = [jnp.exp(x - m) for x in lses]
            num = es[0] * outs[0] + es[1] * outs[1] + es[2] * outs[2]
            ob[0, tok, :] = num * (1.0 / (es[0] + es[1] + es[2]))
        return carry

    lax.fori_loop(0, l4 // n, combine_body, 0)
    o_ref[...] = ob[0].astype(o_ref.dtype)


def _swa_prompt(qkv, b, t):
    n, s_tok = SWA_BLOCK, SWA_SUPER
    assert t % s_tok == 0 and [d for _, d in SWA_PATTERNS] == [1, QUAD, QUAD * QUAD]
    assert all(w // d == n for w, d in SWA_PATTERNS)
    l4, l16 = s_tok // QUAD, s_tok // (QUAD * QUAD)
    assert (s_tok // n) % SWA_TILES == 0 and SWA_TILES % (l4 // n) == 0 and SWA_TILES % (QUAD * l16 // n) == 0
    view = qkv.reshape(3, SWA_HEADS, b, t, SWA_HD)

    def part(idx):
        return pl.BlockSpec((None, None, None, s_tok, SWA_HD), lambda bi, h, sb: (idx, h, bi, sb, 0))

    def rows(*lead):
        return pltpu.VMEM(lead + (SWA_HD,), BF16)

    o = pl.pallas_call(
        _swa_prompt_kernel,
        grid=(b, SWA_HEADS, t // s_tok),
        in_specs=[part(0), part(1), part(2)],
        out_specs=pl.BlockSpec((None, s_tok, SWA_HD), lambda bi, h, sb: (bi, sb, h)),
        out_shape=jax.ShapeDtypeStruct((b, t, SWA_W), BF16),
        scratch_shapes=[
            pltpu.VMEM((3, QUAD, l4, SWA_HD), F32),
            rows(s_tok), rows(QUAD, l4), rows(QUAD * QUAD, l16),
            rows(n + s_tok), rows(QUAD, n + l4), rows(QUAD * QUAD, n + l16),
            rows(n + s_tok), rows(QUAD, n + l4), rows(QUAD * QUAD, n + l16),
            pltpu.VMEM((3, s_tok, SWA_HD), F32), pltpu.VMEM((3, s_tok, SWA_HD), F32),
            pltpu.VMEM((2 * len(SWA_PATTERNS), n, 2 * n), F32),
        ],
        compiler_params=_params(("parallel", "parallel", "arbitrary")),
        name="swa_prompt",
    )(view, view, view)
    return o.reshape(b * t, SWA_W)


def _branch_multiplicity(d):
    mult = jnp.zeros(d.shape, F32)
    for window, dil in SWA_PATTERNS:
        hit = (d >= 0) & (d <= window) & ((d & (dil - 1)) == 0)
        mult = mult + jnp.where(hit, 1.0, 0.0)
    return mult


SAMPLE_NEAR = SWA_PATTERNS[1][0]
SAMPLE_GROUP = SWA_PATTERNS[2][1]


def _swa_sample_kernel(q_ref, kn_ref, vn_ref, kfar_ref, vfar_ref, knear_ref, vnear_ref, o_ref, *, w, tn):
    tp = q_ref.shape[1]
    groups, per = kfar_ref.shape[1], kfar_ref.shape[2]
    n_far, n_near = groups * tn, knear_ref.shape[1] // SWA_HEADS
    kfar, vfar = kfar_ref.reshape(groups * per, SWA_HD), vfar_ref.reshape(groups * per, SWA_HD)

    def query_row(shape):
        return lax.broadcasted_iota(jnp.int32, shape, 0)

    e_far = lax.broadcasted_iota(jnp.int32, (tp, n_far), 1)
    pos_far = (e_far // tn) * SAMPLE_GROUP + e_far % tn
    pos_near = (w - n_near) + lax.broadcasted_iota(jnp.int32, (tp, n_near), 1)
    j_new = lax.broadcasted_iota(jnp.int32, (tp, tp), 1)
    dists = [w + query_row((tp, n_far)) - pos_far,
             w + query_row((tp, n_near)) - pos_near,
             jnp.where(j_new < tn, query_row((tp, tp)) - j_new, -1)]
    mults = [_branch_multiplicity(d) for d in dists]
    dists = [d.astype(F32) for d in dists]
    scale = SWA_HD ** -0.5

    for h in range(SWA_HEADS):
        hs = slice(h * SWA_HD, (h + 1) * SWA_HD)
        far_rows = pl.ds(h, n_far, stride=SWA_HEADS)
        near_rows = pl.ds(h, n_near, stride=SWA_HEADS)
        slope = ALIBI_SLOPES[h]
        q = q_ref[0, :, hs]
        keys = [kfar[far_rows, :].astype(BF16), knear_ref[0, near_rows, :].astype(BF16), kn_ref[0, :, hs]]
        values = [vfar[far_rows, :].astype(BF16), vnear_ref[0, near_rows, :].astype(BF16), vn_ref[0, :, hs]]
        scores = [jnp.where(mult > 0.0, _dot_nt(q, k) * scale - slope * d, NEG)
                  for k, d, mult in zip(keys, dists, mults)]
        m = functools.reduce(jnp.maximum, [jnp.max(s, axis=1, keepdims=True) for s in scores])
        probs = [mult * jnp.exp(s - m) for s, mult in zip(scores, mults)]
        l = functools.reduce(lambda a, b: a + b, [jnp.sum(p, axis=1, keepdims=True) for p in probs])
        acc = functools.reduce(lambda a, b: a + b, [_dot(p.astype(BF16), v) for p, v in zip(probs, values)])
        o_ref[0, :, hs] = (acc / l).astype(o_ref.dtype)


def _swa_sample(projp, cache_k, cache_v, *, tn):
    b, tp, _ = projp.shape
    rows = cache_k.shape[1]
    w = rows // SWA_HEADS
    group_rows = SAMPLE_GROUP * SWA_HEADS
    near_rows = SAMPLE_NEAR * SWA_HEADS
    assert w % SAMPLE_NEAR == 0 and SAMPLE_NEAR % SAMPLE_GROUP == 0 and tn <= SAMPLE_GROUP
    assert [wd for wd, _ in SWA_PATTERNS][:2] == [SWA_BLOCK, SAMPLE_NEAR] and w >= SWA_PATTERNS[2][0]
    far_groups = (w - SAMPLE_NEAR) // SAMPLE_GROUP
    grouped = (b, rows // group_rows, group_rows, SWA_HD)

    def new(cb):
        return pl.BlockSpec((1, tp, SWA_W), lambda bi: (bi, 0, cb))

    far = pl.BlockSpec((1, far_groups, tn * SWA_HEADS, SWA_HD), lambda bi: (bi, 0, 0, 0))
    near = pl.BlockSpec((1, near_rows, SWA_HD), lambda bi: (bi, rows // near_rows - 1, 0))
    return pl.pallas_call(
        functools.partial(_swa_sample_kernel, w=w, tn=tn),
        grid=(b,),
        in_specs=[new(0), new(1), new(2), far, far, near, near],
        out_specs=pl.BlockSpec((1, tp, SWA_W), lambda bi: (bi, 0, 0)),
        out_shape=jax.ShapeDtypeStruct((b, tp, SWA_W), BF16),
        compiler_params=_params(("parallel",)),
        name="swa_sample",
    )(projp, projp, projp, cache_k.reshape(grouped), cache_v.reshape(grouped), cache_k, cache_v)


def _outproj_kernel(x_ref, oa_ref, ob_ref, wa_ref, wb_ref, h_ref):
    h_ref[...] = x_ref[...] + _dot(oa_ref[...], wa_ref[...]) + _dot(ob_ref[...], wb_ref[...])


def _outproj(x, o_a, o_b, w_out, *, tm):
    m, d = x.shape
    return pl.pallas_call(
        _outproj_kernel,
        grid=(m // tm,),
        in_specs=[pl.BlockSpec((tm, d), lambda i: (i, 0)),
                  pl.BlockSpec((tm, GLA_V_W), lambda i: (i, 0)),
                  pl.BlockSpec((tm, SWA_W), lambda i: (i, 0)),
                  pl.BlockSpec((GLA_V_W, d), lambda i: (0, 0)),
                  pl.BlockSpec((SWA_W, d), lambda i: (GLA_V_W // SWA_W, 0))],
        out_specs=pl.BlockSpec((tm, d), lambda i: (i, 0)),
        out_shape=jax.ShapeDtypeStruct((m, d), F32),
        compiler_params=_params(("parallel",)),
        name="outproj",
    )(x, o_a, o_b, w_out, w_out)


FFN_HALO = 16
FFN_TF = 512
CACHE_PIECES = 4


def _ffn_prompt_kernel(h_ref, halo_ref, g_ref, wa_ref, wv_ref, cp_ref, wd_ref, gf_ref, kc_hbm, vc_hbm, kn_ref,
                       vn_ref, y_ref, u_ref, ko_hbm, vo_hbm, xn_scr, ea_scr, ev_scr, ks_scr, vs_scr, sems,
                       *, tiles_per_seq):
    i, j = pl.program_id(0), pl.program_id(1)
    tm = h_ref.shape[0]
    last_j = pl.num_programs(1) - 1

    nseq, rows = kc_hbm.shape[0], kc_hbm.shape[1]
    new = kn_ref.shape[1]
    old = rows - new
    piece = old // CACHE_PIECES
    moving = i < nseq

    def load(src_hbm, dst_scr, c, sem):
        part = pl.ds(c * piece, piece)
        return pltpu.make_async_copy(src_hbm.at[i, pl.ds(new + c * piece, piece)], dst_scr.at[part], sem)

    def store(src_scr, dst_hbm, c, sem):
        part = pl.ds(c * piece, piece)
        return pltpu.make_async_copy(src_scr.at[part], dst_hbm.at[i, part], sem)

    loads = [[load(kc_hbm, ks_scr, c, sems.at[0]), load(vc_hbm, vs_scr, c, sems.at[1])]
             for c in range(CACHE_PIECES)]
    stores = [[store(ks_scr, ko_hbm, c, sems.at[2]), store(vs_scr, vo_hbm, c, sems.at[3])]
              for c in range(CACHE_PIECES)]
    stores[0] += [pltpu.make_async_copy(kn_ref.at[0], ko_hbm.at[i, pl.ds(old, new)], sems.at[4]),
                  pltpu.make_async_copy(vn_ref.at[0], vo_hbm.at[i, pl.ds(old, new)], sems.at[5])]

    for c in range(CACHE_PIECES):
        @pl.when(moving & (j == c))
        def _():
            for cp in loads[c]:
                cp.start(priority=1)

        @pl.when(moving & (j == CACHE_PIECES + c))
        def _():
            if c == 0:
                for group in loads:
                    for cp in group:
                        cp.wait()
            for cp in stores[c]:
                cp.start(priority=1)

    @pl.when(j == 0)
    def _():
        first = (i % tiles_per_seq) == 0
        halo = _rms_rows(halo_ref[...], g_ref[...])
        xn_scr[0:FFN_HALO, :] = jnp.where(first, 0.0, halo).astype(BF16)
        xn_scr[FFN_HALO:, :] = _rms_rows(h_ref[...], g_ref[...]).astype(BF16)
        y_ref[...] = h_ref[...]

    xn = xn_scr[...]
    ea_scr[...] = _dot(xn, wa_ref[...])
    ev_scr[...] = _dot(xn, wv_ref[...])

    def conv(e_scr, part):
        out = cp_ref[part, CONV_W:CONV_W + 1, :]
        for tap in range(CONV_W):
            off = FFN_HALO - (CONV_W - 1) + tap
            out = out + cp_ref[part, tap:tap + 1, :] * e_scr[off:off + tm, :]
        return out

    gate = (_silu(conv(ea_scr, 0)) * conv(ev_scr, 1)).astype(BF16)
    y_ref[...] += _dot(gate, wd_ref[...])
    u_ref[0, j, 0] = ea_scr[FFN_HALO + tm - 8:FFN_HALO + tm, :]
    u_ref[0, j, 1] = ev_scr[FFN_HALO + tm - 8:FFN_HALO + tm, :]

    @pl.when(j == last_j)
    def _():
        y_ref[...] = _rms_rows(y_ref[...], gf_ref[...])

    @pl.when(moving & (j == last_j))
    def _():
        for group in stores:
            for cp in group:
                cp.wait()


def _ffn_prompt(h, g_ffn, w_up, conv_params, w_down, g_final, cache_k, cache_v, k_new, v_new, *, seq, tm):
    m, d = h.shape
    f = w_down.shape[0]
    tf = conv_params.shape[-1]
    nf = f // tf
    ntile = m // tm
    hb = tm // FFN_HALO
    nseq, rows, _ = cache_k.shape
    new = k_new.shape[1]
    assert nseq <= ntile and nf > 2 * CACHE_PIECES, "one cache per row tile, moved within its column steps"
    assert (rows - new) % (8 * CACHE_PIECES) == 0
    hbm = pl.BlockSpec(memory_space=pl.ANY)
    fresh = pl.BlockSpec((1, new, SWA_HD), lambda i, j: (jnp.minimum(i, nseq - 1), 0, 0))
    y, u_last, k_all, v_all = pl.pallas_call(
        functools.partial(_ffn_prompt_kernel, tiles_per_seq=seq // tm),
        grid=(ntile, nf),
        in_specs=[
            pl.BlockSpec((tm, d), lambda i, j: (i, 0)),
            pl.BlockSpec((FFN_HALO, d), lambda i, j: (jnp.maximum(i * hb - 1, 0), 0)),
            pl.BlockSpec((1, d), lambda i, j: (0, 0)),
            pl.BlockSpec((d, tf), lambda i, j: (0, j)),
            pl.BlockSpec((d, tf), lambda i, j: (0, nf + j)),
            pl.BlockSpec((None, 2, 8, tf), lambda i, j: (j, 0, 0, 0)),
            pl.BlockSpec((tf, d), lambda i, j: (j, 0)),
            pl.BlockSpec((1, d), lambda i, j: (0, 0)),
            hbm, hbm, fresh, fresh,
        ],
        out_specs=[
            pl.BlockSpec((tm, d), lambda i, j: (i, 0)),
            pl.BlockSpec((1, nf, 2, 8, tf), lambda i, j: (i, 0, 0, 0, 0)),
            hbm, hbm,
        ],
        out_shape=[jax.ShapeDtypeStruct((m, d), F32),
                   jax.ShapeDtypeStruct((ntile, nf, 2, 8, tf), F32),
                   jax.ShapeDtypeStruct(cache_k.shape, cache_k.dtype),
                   jax.ShapeDtypeStruct(cache_v.shape, cache_v.dtype)],
        scratch_shapes=[pltpu.VMEM((tm + FFN_HALO, d), BF16),
                        pltpu.VMEM((tm + FFN_HALO, tf), F32),
                        pltpu.VMEM((tm + FFN_HALO, tf), F32),
                        pltpu.VMEM((rows - new, SWA_HD), F32),
                        pltpu.VMEM((rows - new, SWA_HD), F32),
                        pltpu.SemaphoreType.DMA((6,))],
        compiler_params=_params(("arbitrary", "arbitrary")),
        name="convffn_prompt",
    )(h, h, g_ffn, w_up, w_up, conv_params, w_down, g_final, cache_k, cache_v, k_new, v_new)
    u_last = u_last.transpose(0, 3, 2, 1, 4).reshape(ntile, 8, 2 * f)
    return y, u_last, k_all, v_all


def _ffn_sample_kernel(h_ref, g_ref, sa_ref, sv_ref, wa_ref, wv_ref, ca_ref, cv_ref, wd_ref,
                       gf_ref, y_ref, na_ref, nv_ref, xn_scr, acc_scr, *, nb):
    j = pl.program_id(0)
    m = h_ref.shape[0]
    keep = (CONV_W - 1) * nb

    @pl.when(j == 0)
    def _():
        xn_scr[...] = _rms_rows(h_ref[...], g_ref[...]).astype(BF16)
        acc_scr[...] = jnp.zeros_like(acc_scr)

    xn = xn_scr[...]

    def conv(s_ref, w_ref, c_ref, n_ref):
        u = _dot(xn, w_ref[...])
        ext = jnp.concatenate([s_ref[...], u], axis=0)
        n_ref[...] = ext[m:m + keep, :]
        out = c_ref[CONV_W:CONV_W + 1, :]
        for tap in range(CONV_W):
            out = out + c_ref[tap:tap + 1, :] * ext[tap * nb:tap * nb + m, :]
        return out

    gate = (_silu(conv(sa_ref, wa_ref, ca_ref, na_ref)) * conv(sv_ref, wv_ref, cv_ref, nv_ref)).astype(BF16)
    acc_scr[...] += _dot(gate, wd_ref[...])

    @pl.when(j == pl.num_programs(0) - 1)
    def _():
        y_ref[...] = _rms_rows(h_ref[...] + acc_scr[...], gf_ref[...])


def _ffn_sample(h_tm, state_tm, g_ffn, w_up, conv_params, w_down, g_final, *, nb):
    m, d = h_tm.shape
    f = w_down.shape[0]
    tf = conv_params.shape[-1]
    nf = f // tf
    keep = (CONV_W - 1) * nb
    return pl.pallas_call(
        functools.partial(_ffn_sample_kernel, nb=nb),
        grid=(nf,),
        in_specs=[
            pl.BlockSpec((m, d), lambda j: (0, 0)),
            pl.BlockSpec((1, d), lambda j: (0, 0)),
            pl.BlockSpec((keep, tf), lambda j: (0, j)),
            pl.BlockSpec((keep, tf), lambda j: (0, nf + j)),
            pl.BlockSpec((d, tf), lambda j: (0, j)),
            pl.BlockSpec((d, tf), lambda j: (0, nf + j)),
            pl.BlockSpec((None, None, 8, tf), lambda j: (j, 0, 0, 0)),
            pl.BlockSpec((None, None, 8, tf), lambda j: (j, 1, 0, 0)),
            pl.BlockSpec((tf, d), lambda j: (j, 0)),
            pl.BlockSpec((1, d), lambda j: (0, 0)),
        ],
        out_specs=[
            pl.BlockSpec((m, d), lambda j: (0, 0)),
            pl.BlockSpec((keep, tf), lambda j: (0, j)),
            pl.BlockSpec((keep, tf), lambda j: (0, j)),
        ],
        out_shape=[jax.ShapeDtypeStruct((m, d), F32),
                   jax.ShapeDtypeStruct((keep, f), F32),
                   jax.ShapeDtypeStruct((keep, f), F32)],
        scratch_shapes=[pltpu.VMEM((m, d), BF16), pltpu.VMEM((m, d), F32)],
        compiler_params=_params(("arbitrary",)),
        name="convffn_sample",
    )(h_tm, g_ffn, state_tm, state_tm, w_up, w_up, conv_params, conv_params, w_down, g_final)


def _repack_w_in_kernel(w_ref, main_ref, z_ref):
    z0 = GLA_COLS
    rest = w_ref[:, z0:]
    main_ref[:, :z0] = w_ref[:, :z0].astype(BF16)
    main_ref[:, z0:] = rest[:, GLA_GATE_RANK:].astype(BF16)
    lane = lax.broadcasted_iota(jnp.int32, (w_ref.shape[0], LANE), 1)
    z_ref[...] = jnp.where(lane < GLA_GATE_RANK, rest[:, :LANE], 0.0).astype(BF16)


def _repack_w_in(w_in, *, tr):
    d, n = w_in.shape
    assert n == PROJ_W + GLA_GATE_RANK and d % tr == 0
    return pl.pallas_call(
        _repack_w_in_kernel,
        grid=(d // tr,),
        in_specs=[pl.BlockSpec((tr, n), lambda i: (i, 0))],
        out_specs=[pl.BlockSpec((tr, PROJ_W), lambda i: (i, 0)), pl.BlockSpec((tr, LANE), lambda i: (i, 0))],
        out_shape=[jax.ShapeDtypeStruct((d, PROJ_W), BF16), jax.ShapeDtypeStruct((d, LANE), BF16)],
        compiler_params=_params(("parallel",)),
        name="repack_w_in",
    )(w_in)


def _pack_weights(g_attn_norm, w_in, w_gate_up, b_gate, g_gla_norm, w_out, g_ffn_norm, w_up, w_conv, b_conv,
                  w_down):
    w_main, w_z = _repack_w_in(w_in, tr=256)
    w_gate = jnp.pad(w_gate_up, ((0, LANE - GLA_GATE_RANK), (0, 0))).astype(BF16)
    conv_rows = jnp.concatenate([w_conv, b_conv[None, :], jnp.zeros((8 - CONV_W - 1, b_conv.shape[0]), F32)])
    conv_params = conv_rows.reshape(8, 2, -1, FFN_TF).transpose(2, 1, 0, 3)
    return dict(
        g_attn=g_attn_norm[None, :], w_main=w_main, w_z=w_z, w_gate=w_gate, b_gate=b_gate[None, :],
        g_gla=g_gla_norm[None, :], w_out=w_out.astype(BF16), g_ffn=g_ffn_norm[None, :],
        w_up=w_up.astype(BF16), conv_params=conv_params, w_down=w_down.astype(BF16))


def _prompt_layer(x, p, g_final, sample_cache):
    b, t, d = x.shape
    m = b * t
    f = p["w_down"].shape[0]
    keep = min(SWA_MAX_WINDOW, t)
    x2 = x.reshape(m, d)
    proj, loga, qkv, k_buf, v_buf = _inproj(x2, p["g_attn"], p["w_main"], p["w_z"], p["w_gate"], p["b_gate"],
                                            tm=1024, seq=t, keep=keep)
    o_a, s_fin = _gla(proj.reshape(b, t, GLA_COLS), loga.reshape(b, t, GLA_QK_W), p["g_gla"], None,
                      chunk=GLA_CHUNK, tc=512)
    o_b = _swa_prompt(qkv, b, t)
    h = _outproj(x2, o_a.reshape(m, GLA_V_W), o_b, p["w_out"], tm=512)
    tm = 512
    y, u_last, k_all, v_all = _ffn_prompt(h, p["g_ffn"], p["w_up"], p["conv_params"], p["w_down"], g_final,
                                          *sample_cache, seq=t, tm=tm)
    shape = (b, keep, SWA_HEADS, SWA_HD)
    last = u_last.reshape(b, t // tm, 8, 2 * f)[:, -1, 8 - (CONV_W - 1):, :]
    return y.reshape(b, t, d), s_fin, k_buf.reshape(shape), v_buf.reshape(shape), last, k_all, v_all


def _sample_layer(x, s0, conv_s0, cache_k, cache_v, p, g_final):
    b, tn, d = x.shape
    m = b * tn
    tp = 8
    w = cache_k.shape[1]
    x2 = x.reshape(m, d)
    proj, loga, qkv, k_new, v_new = _inproj(x2, p["g_attn"], p["w_main"], p["w_z"], p["w_gate"], p["b_gate"],
                                            tm=m, seq=m, keep=m)
    pad = ((0, 0), (0, tp - tn), (0, 0))
    projp = jnp.pad(proj.reshape(b, tn, GLA_COLS), pad)
    logap = jnp.pad(loga.reshape(b, tn, GLA_QK_W), pad)
    o_a, s_fin = _gla(projp, logap, p["g_gla"], s0, chunk=tp, tc=tp)
    qkvp = jnp.pad(qkv.transpose(2, 0, 1, 3).reshape(b, tn, 3 * SWA_W).astype(BF16), pad)
    rows = (b, tn * SWA_HEADS, SWA_HD)
    cache = (cache_k.reshape(b, w * SWA_HEADS, SWA_HD), cache_v.reshape(b, w * SWA_HEADS, SWA_HD),
             k_new.reshape(rows), v_new.reshape(rows))
    o_b = _swa_sample(qkvp, cache[0], cache[1], tn=tn)
    h = _outproj(x2, o_a[:, :tn].reshape(m, GLA_V_W), o_b[:, :tn].reshape(m, SWA_W), p["w_out"], tm=m)
    h_tm = h.reshape(b, tn, d).transpose(1, 0, 2).reshape(m, d)
    state_tm = conv_s0.transpose(1, 0, 2).reshape((CONV_W - 1) * b, -1)
    y_tm, na, nv = _ffn_sample(h_tm, state_tm, p["g_ffn"], p["w_up"], p["conv_params"], p["w_down"], g_final, nb=b)
    y = y_tm.reshape(tn, b, d).transpose(1, 0, 2)
    conv_new = jnp.concatenate([na, nv], axis=-1).reshape(CONV_W - 1, b, -1).transpose(1, 0, 2)
    return y, s_fin, conv_new, cache


def kernel(x_prompt, x_sample, state_gla, cache_swa_k, cache_swa_v, state_ffn_conv, g_attn_norm, w_in, w_gate_up,
           b_gate, g_gla_norm, w_out, g_ffn_norm, w_up, w_conv, b_conv, w_down, g_final):
    depth = w_in.shape[0]
    assert depth == 1, "the final RMSNorm is fused into the last layer's ConvFFN kernel"
    assert cache_swa_k.shape[2] == SWA_MAX_WINDOW and x_sample.shape[1] <= 8
    p = _pack_weights(g_attn_norm[0], w_in[0], w_gate_up[0], b_gate[0], g_gla_norm[0], w_out[0], g_ffn_norm[0],
                      w_up[0], w_conv[0], b_conv[0], w_down[0])
    gf = g_final[None, :]
    ys, s2, c2, sample_cache = _sample_layer(x_sample, state_gla[0], state_ffn_conv[0], cache_swa_k[0],
                                             cache_swa_v[0], p, gf)
    yp, s1, k1, v1, c1, k2, v2 = _prompt_layer(x_prompt, p, gf, sample_cache)
    k2, v2 = k2.reshape(cache_swa_k.shape[1:]), v2.reshape(cache_swa_v.shape[1:])
    return (yp, ys, s1[None], s2[None], k1[None], k2[None], v1[None], v2[None], c1[None], c2[None])
```

```python
import functools

import jax
import jax.numpy as jnp
from jax import lax
from jax.experimental import pallas as pl
from jax.experimental.pallas import tpu as pltpu

F32 = jnp.float32
BF16 = jnp.bfloat16

GLA_HEADS = 4
GLA_DK = 128
GLA_DV = 256
GLA_GATE_RANK = 16
GLA_TAU = 16.0
GLA_CHUNK = 64
SWA_HEADS = 8
SWA_HD = 128
SWA_PATTERNS = ((128, 1), (512, 4), (2048, 16))
SWA_BLOCK = 128
SWA_MAX_WINDOW = 2048
CONV_W = 3
EPS = 1e-6
NEG = -1e30

GLA_QK_W = GLA_HEADS * GLA_DK
GLA_V_W = GLA_HEADS * GLA_DV
SWA_W = SWA_HEADS * SWA_HD
PROJ_W = 2 * GLA_QK_W + 2 * GLA_V_W + 3 * SWA_W
LANE = 128
VMEM_LIMIT = 56 * 1024 * 1024

ALIBI_SLOPES = tuple(2.0 ** (-8.0 * (h + 1) / SWA_HEADS) for h in range(SWA_HEADS))


def _params(semantics):
    return pltpu.CompilerParams(dimension_semantics=semantics, vmem_limit_bytes=VMEM_LIMIT)


def _rms_rows(x, g):
    return x * lax.rsqrt(jnp.mean(x * x, axis=-1, keepdims=True) + EPS) * g


def _silu(x):
    return x / (1.0 + jnp.exp(-x))


def _log_sigmoid(x):
    return jnp.minimum(x, 0.0) - jnp.log1p(jnp.exp(-jnp.abs(x)))


def _dot(a, b):
    return jnp.dot(a, b, preferred_element_type=F32)


def _dot_nt(a, b):
    return lax.dot_general(a, b, (((1,), (1,)), ((), ())), preferred_element_type=F32)


def _dot_tn(a, b):
    return lax.dot_general(a, b, (((0,), (0,)), ((), ())), preferred_element_type=F32)


GLA_COLS = 2 * GLA_QK_W + 2 * GLA_V_W
GLA_STEPS = GLA_COLS // SWA_W


def _inproj_kernel(x_hbm, g_ref, w_ref, wz_ref, wg_ref, bg_ref, proj_ref, loga_ref, qkv_ref, kc_hbm, vc_hbm,
                   x_buf, xn_ref, kc_buf, vc_buf, sem, *, tiles_per_seq, kept_tiles):
    i, j = pl.program_id(0), pl.program_id(1)
    tm = x_buf.shape[0]
    last_i = pl.num_programs(0) - 1

    def x_copy(tile):
        return pltpu.make_async_copy(x_hbm.at[pl.ds(tile * tm, tm)], x_buf, sem.at[0])

    def kept(tile):
        return tile % tiles_per_seq >= tiles_per_seq - kept_tiles

    def cache_copy(buf, out_hbm, tile, s):
        block = (tile // tiles_per_seq) * kept_tiles + tile % tiles_per_seq - (tiles_per_seq - kept_tiles)
        rows = tm * SWA_HEADS
        return pltpu.make_async_copy(buf, out_hbm.at[pl.ds(block * rows, rows)], sem.at[s])

    @pl.when((j == 1) & (i > 0) & kept(i - 1))
    def _():
        cache_copy(vc_buf, vc_hbm, i - 1, 2).wait()

    @pl.when(j == 0)
    def _():
        @pl.when(i == 0)
        def _():
            x_copy(0).start(priority=1)

        x_copy(i).wait()
        xn_ref[...] = _rms_rows(x_buf[...], g_ref[...]).astype(BF16)

        @pl.when(i < last_i)
        def _():
            x_copy(i + 1).start(priority=1)

        za = _dot_nt(xn_ref[...], wz_ref[...])
        gate = _dot(za.astype(BF16), wg_ref[...]) + bg_ref[...]
        loga_ref[...] = _log_sigmoid(gate) / GLA_TAU

    @pl.when(j < GLA_STEPS)
    def _():
        proj_ref[...] = _dot_nt(xn_ref[...], w_ref[...]).astype(BF16)

    def swa_step(cache):
        res = _dot_nt(xn_ref[...], w_ref[...])
        for h in range(SWA_HEADS):
            qkv_ref[0, h] = res[:, h * SWA_HD:(h + 1) * SWA_HD]
        if cache is not None:
            buf, out_hbm, s = cache

            @pl.when(kept(i))
            def _():
                for h in range(SWA_HEADS):
                    buf[pl.ds(h, tm, stride=SWA_HEADS), :] = res[:, h * SWA_HD:(h + 1) * SWA_HD]
                cache_copy(buf, out_hbm, i, s).start(priority=1)

    for step, cache in enumerate((None, (kc_buf, kc_hbm, 1), (vc_buf, vc_hbm, 2))):
        pl.when(j == GLA_STEPS + step)(functools.partial(swa_step, cache))

    @pl.when((j == GLA_STEPS + 2) & kept(i))
    def _():
        cache_copy(kc_buf, kc_hbm, i, 1).wait()

        @pl.when(i == last_i)
        def _():
            cache_copy(vc_buf, vc_hbm, i, 2).wait()


def _inproj(x, g, w_main, w_z, w_gate, b_gate, *, tm, seq, keep):
    m, d = x.shape
    tn = SWA_W
    tps, kt = seq // tm, keep // tm
    assert seq % tm == 0 and keep % tm == 0 and m % seq == 0

    cache_rows = (m // seq) * keep * SWA_HEADS
    hbm = pl.BlockSpec(memory_space=pl.ANY)
    return pl.pallas_call(
        functools.partial(_inproj_kernel, tiles_per_seq=tps, kept_tiles=kt),
        grid=(m // tm, PROJ_W // tn),
        in_specs=[
            pl.BlockSpec(memory_space=pl.ANY),
            pl.BlockSpec((1, d), lambda i, j: (0, 0)),
            pl.BlockSpec((tn, d), lambda i, j: (j, 0)),
            pl.BlockSpec((LANE, d), lambda i, j: (0, 0)),
            pl.BlockSpec((LANE, GLA_QK_W), lambda i, j: (0, 0)),
            pl.BlockSpec((1, GLA_QK_W), lambda i, j: (0, 0)),
        ],
        out_specs=[
            pl.BlockSpec((tm, tn), lambda i, j: (i, jnp.minimum(j, GLA_STEPS - 1))),
            pl.BlockSpec((tm, GLA_QK_W), lambda i, j: (i, 0)),
            pl.BlockSpec((1, SWA_HEADS, tm, SWA_HD), lambda i, j: (jnp.maximum(j - GLA_STEPS, 0), 0, i, 0)),
            hbm, hbm,
        ],
        out_shape=[jax.ShapeDtypeStruct((m, GLA_COLS), BF16),
                   jax.ShapeDtypeStruct((m, GLA_QK_W), F32),
                   jax.ShapeDtypeStruct((3, SWA_HEADS, m, SWA_HD), F32),
                   jax.ShapeDtypeStruct((cache_rows, SWA_HD), F32),
                   jax.ShapeDtypeStruct((cache_rows, SWA_HD), F32)],
        scratch_shapes=[pltpu.VMEM((tm, d), F32), pltpu.VMEM((tm, d), BF16),
                        pltpu.VMEM((tm * SWA_HEADS, SWA_HD), F32), pltpu.VMEM((tm * SWA_HEADS, SWA_HD), F32),
                        pltpu.SemaphoreType.DMA((3,))],
        compiler_params=_params(("arbitrary", "arbitrary")),
        name="norm_inproj",
    )(x, g, w_main, w_z, w_gate, b_gate)


def _split3(x):
    hi = x.astype(BF16)
    r1 = x - hi.astype(F32)
    mid = r1.astype(BF16)
    lo = (r1 - mid.astype(F32)).astype(BF16)
    return hi, mid, lo


def _gla_kernel(*refs, chunk, nsub, has_s0):
    scratch = refs[-5:]
    s_scr, qd_scr, oi_scr, u_scr, dec_scr = scratch
    if has_s0:
        q_ref, k_ref, v_ref, ga_ref, la_ref, gn_ref, s0_ref, o_ref, sfin_ref = refs[:-5]
    else:
        q_ref, k_ref, v_ref, ga_ref, la_ref, gn_ref, o_ref, sfin_ref = refs[:-5]
    c = pl.program_id(1)

    @pl.when(c == 0)
    def _():
        if has_s0:
            s_scr[...] = s0_ref[0]
        else:
            s_scr[...] = jnp.zeros_like(s_scr)

    row = lax.broadcasted_iota(jnp.int32, (chunk, chunk), 0)
    col = lax.broadcasted_iota(jnp.int32, (chunk, chunk), 1)
    causal = row >= col
    tril = jnp.where(causal, 1.0, 0.0).astype(BF16)
    eye = (lax.broadcasted_iota(jnp.int32, (GLA_DK, GLA_DK), 0)
           == lax.broadcasted_iota(jnp.int32, (GLA_DK, GLA_DK), 1))

    group = min(nsub, 4)
    heads = range(GLA_HEADS)
    ks = [slice(h * GLA_DK, (h + 1) * GLA_DK) for h in heads]
    vs = [slice(h * GLA_DV, (h + 1) * GLA_DV) for h in heads]

    def local(c2, carry):
        subs = [c2 * group + a for a in range(group)]
        sls = [pl.ds(pl.multiple_of(i * chunk, chunk), chunk) for i in subs]
        pieces = [_split3(la_ref[0, sl, :]) for sl in sls]
        cums = [_dot(tril, hi) + _dot(tril, mid) + _dot(tril, lo) for hi, mid, lo in pieces]
        q_dec, k_dec, k_end, grow = [], [], [], []
        for sl, cum in zip(sls, cums):
            total = cum[chunk - 1:chunk, :]
            k = k_ref[0, sl, :].astype(F32)
            q_dec.append((q_ref[0, sl, :].astype(F32) * (GLA_DK ** -0.5) * jnp.exp(cum)).astype(BF16))
            k_dec.append((k * jnp.exp(-cum)).astype(BF16))
            k_end.append((k * jnp.exp(total - cum)).astype(BF16))
            grow.append(jnp.exp(total))
            qd_scr[sl, :] = q_dec[-1]
        units = [(a, h) for a in range(group) for h in heads]
        vals = {(a, h): v_ref[0, sls[a], vs[h]] for a, h in units}
        raw = {(a, h): _dot_nt(q_dec[a][:, ks[h]], k_dec[a][:, ks[h]]) for a, h in units}
        for a, h in units:
            u_scr[subs[a], h] = _dot_tn(k_end[a][:, ks[h]], vals[a, h])
        attn = {(a, h): jnp.where(causal, raw[a, h], 0.0).astype(BF16) for a, h in units}
        for a, h in units:
            oi_scr[sls[a], vs[h]] = _dot(attn[a, h], vals[a, h])
        for a, h in units:
            decay = jnp.sum(jnp.where(eye, grow[a][:, ks[h]], 0.0), axis=1, keepdims=True)
            dec_scr[subs[a], h] = jnp.broadcast_to(decay, (GLA_DK, GLA_DK))
        return carry

    if nsub == group:
        local(0, 0)
    else:
        lax.fori_loop(0, nsub // group, local, 0)

    for i in range(nsub):
        sl = slice(i * chunk, (i + 1) * chunk)
        states = [s_scr[h] for h in heads]
        inter = [_dot(qd_scr[sl, ks[h]], states[h].astype(BF16)) for h in heads]
        for h in heads:
            decay = dec_scr[i, h]
            s_scr[h] = states[h] * jnp.concatenate([decay] * (GLA_DV // GLA_DK), axis=1) + u_scr[i, h]
        for h in heads:
            on = _rms_rows(oi_scr[sl, vs[h]] + inter[h], gn_ref[:, vs[h]])
            o_ref[0, sl, vs[h]] = (on * _silu(ga_ref[0, sl, vs[h]].astype(F32))).astype(o_ref.dtype)

    @pl.when(c == pl.num_programs(1) - 1)
    def _():
        sfin_ref[0] = s_scr[...]


def _gla(proj3, loga3, g_norm, s0, *, chunk, tc):
    b, t, _ = proj3.shape
    nsub = tc // chunk
    state = pl.BlockSpec((1, GLA_HEADS, GLA_DK, GLA_DV), lambda bi, c: (bi, 0, 0, 0))
    in_specs = [
        pl.BlockSpec((1, tc, GLA_QK_W), lambda bi, c: (bi, c, 0)),
        pl.BlockSpec((1, tc, GLA_QK_W), lambda bi, c: (bi, c, 1)),
        pl.BlockSpec((1, tc, GLA_V_W), lambda bi, c: (bi, c, 1)),
        pl.BlockSpec((1, tc, GLA_V_W), lambda bi, c: (bi, c, 2)),
        pl.BlockSpec((1, tc, GLA_QK_W), lambda bi, c: (bi, c, 0)),
        pl.BlockSpec((1, GLA_V_W), lambda bi, c: (0, 0)),
    ]
    args = [proj3, proj3, proj3, proj3, loga3, g_norm]
    if s0 is not None:
        in_specs.append(state)
        args.append(s0)
    return pl.pallas_call(
        functools.partial(_gla_kernel, chunk=chunk, nsub=nsub, has_s0=s0 is not None),
        grid=(b, t // tc),
        in_specs=in_specs,
        out_specs=[pl.BlockSpec((1, tc, GLA_V_W), lambda bi, c: (bi, c, 0)), state],
        out_shape=[jax.ShapeDtypeStruct((b, t, GLA_V_W), BF16),
                   jax.ShapeDtypeStruct((b, GLA_HEADS, GLA_DK, GLA_DV), F32)],
        scratch_shapes=[pltpu.VMEM((GLA_HEADS, GLA_DK, GLA_DV), F32),
                        pltpu.VMEM((tc, GLA_QK_W), BF16),
                        pltpu.VMEM((tc, GLA_V_W), F32),
                        pltpu.VMEM((nsub, GLA_HEADS, GLA_DK, GLA_DV), F32),
                        pltpu.VMEM((nsub, GLA_HEADS, GLA_DK, GLA_DK), F32)],
        compiler_params=_params(("parallel", "arbitrary")),
        name="gla",
    )(*args)


SWA_SUPER = 2048
SWA_TILES = 16
SWA_GROUP = 8
QUAD = 4


def _swa_prompt_kernel(q_ref, k_ref, v_ref, o_ref, x4, q1, q4, q16, k1, k4, k16, v1, v4, v16, ob, lb, bias):
    n = SWA_BLOCK
    s_tok = q_ref.shape[0]
    l4, l16 = s_tok // QUAD, s_tok // (QUAD * QUAD)
    head, has_prev = pl.program_id(1), pl.program_id(2) > 0
    scale = SWA_HD ** -0.5

    slope = ALIBI_SLOPES[0]
    for h in range(1, SWA_HEADS):
        slope = jnp.where(head == h, ALIBI_SLOPES[h], slope)
    qi = lax.broadcasted_iota(jnp.int32, (n, 2 * n), 0)
    kj = lax.broadcasted_iota(jnp.int32, (n, 2 * n), 1)
    steps = qi + n - kj
    visible = (steps >= 0) & (steps <= n)
    first_visible = visible & (has_prev | (kj >= n))
    for g, (_, dil) in enumerate(SWA_PATTERNS):
        alibi = (-float(dil) * slope) * steps.astype(F32)
        bias[2 * g] = jnp.where(visible, alibi, NEG)
        bias[2 * g + 1] = jnp.where(first_visible, alibi, NEG)

    for buf, length in ((k1, s_tok), (v1, s_tok), (k4, l4), (v4, l4), (k16, l16), (v16, l16)):
        lead = buf.shape[:-2]
        idx = (slice(None),) * len(lead)

        @pl.when(has_prev)
        def _():
            buf[idx + (slice(0, n),)] = buf[idx + (slice(length, length + n),)]

        @pl.when(jnp.logical_not(has_prev))
        def _():
            buf[idx + (slice(0, n),)] = jnp.zeros(lead + (n, SWA_HD), BF16)

    for t, src in enumerate((q_ref, k_ref, v_ref)):
        for r in range(QUAD):
            x4[t, r] = src[pl.ds(r, l4, stride=QUAD), :]
    q1[...] = q_ref[...].astype(BF16)
    k1[n:, :] = k_ref[...].astype(BF16)
    v1[n:, :] = v_ref[...].astype(BF16)
    for r in range(QUAD):
        q4[r] = x4[0, r].astype(BF16)
        k4[r, n:, :] = x4[1, r].astype(BF16)
        v4[r, n:, :] = x4[2, r].astype(BF16)
    for r in range(QUAD * QUAD):
        sub = pl.ds(r // QUAD, l16, stride=QUAD)
        q16[r] = x4[0, r % QUAD, sub, :].astype(BF16)
        k16[r, n:, :] = x4[1, r % QUAD, sub, :].astype(BF16)
        v16[r, n:, :] = x4[2, r % QUAD, sub, :].astype(BF16)

    def run_tiles(tiles):
        for g0 in range(0, len(tiles), SWA_GROUP):
            group = tiles[g0:g0 + SWA_GROUP]
            scores = [_dot_nt(q(), keys()) * scale + table() for q, keys, _, table, _, _ in group]
            probs, lses = [], []
            for s in scores:
                m = jnp.max(s, axis=1, keepdims=True)
                p = jnp.exp(s - m)
                l = jnp.sum(p, axis=1, keepdims=True)
                probs.append(p.astype(BF16))
                lses.append((1.0 / l, m + jnp.log(l)))
            accs = [_dot(p, tile[2]()) for p, tile in zip(probs, group)]
            for acc, (inv, lse), (_, _, _, _, branch, rows) in zip(accs, lses, group):
                ob[branch, rows, :] = acc * inv
                lb[branch, rows, :] = jnp.broadcast_to(lse, (n, SWA_HD))

    def window(ref, lead, start, size):
        return lambda: ref[lead + (pl.ds(start, size), slice(None))]

    def d1_body(c, carry):
        tiles = []
        for a in range(SWA_TILES):
            i = c * SWA_TILES + a
            row0 = pl.multiple_of(i * n, n)
            tiles.append((window(q1, (), row0, n), window(k1, (), row0, 2 * n), window(v1, (), row0, 2 * n),
                          lambda i=i: bias[jnp.where(i == 0, 1, 0)], 0, pl.ds(row0, n)))
        run_tiles(tiles)
        return carry

    lax.fori_loop(0, s_tok // n // SWA_TILES, d1_body, 0)

    per4 = l4 // n

    def d4_body(c, carry):
        tiles = []
        for a in range(SWA_TILES // per4):
            r = c * (SWA_TILES // per4) + a
            for i in range(per4):
                tiles.append((window(q4, (r,), i * n, n), window(k4, (r,), i * n, 2 * n),
                              window(v4, (r,), i * n, 2 * n), lambda i=i: bias[2 + (1 if i == 0 else 0)], 1,
                              pl.ds(pl.multiple_of(r * l4 + i * n, n), n)))
        run_tiles(tiles)
        return carry

    lax.fori_loop(0, QUAD * per4 // SWA_TILES, d4_body, 0)

    per16 = l16 // n

    def d16_body(c, carry):
        tiles = []
        for a in range(SWA_TILES // per16):
            r4 = a % QUAD
            j = c * (SWA_TILES // per16 // QUAD) + a // QUAD
            r = r4 + QUAD * j
            for i in range(per16):
                tiles.append((window(q16, (r,), i * n, n), window(k16, (r,), i * n, 2 * n),
                              window(v16, (r,), i * n, 2 * n), lambda i=i: bias[4 + (1 if i == 0 else 0)], 2,
                              pl.ds(r4 * l4 + QUAD * i * n + j, n, stride=QUAD)))
        run_tiles(tiles)
        return carry

    lax.fori_loop(0, QUAD * QUAD * per16 // SWA_TILES, d16_body, 0)

    def combine_body(c, carry):
        for r in range(QUAD):
            tok = pl.ds(r + QUAD * c * n, n, stride=QUAD)
            cls = pl.ds(pl.multiple_of(r * l4 + c * n, n), n)
            lses = [lb[0, tok, :], lb[1, cls, :], lb[2, cls, :]]
            outs = [ob[0, tok, :], ob[1, cls, :], ob[2, cls, :]]
            m = jnp.maximum(jnp.maximum(lses[0], lses[1]), lses[2])
            es = [jnp.exp(x - m) for x in lses]
            num = es[0] * outs[0] + es[1] * outs[1] + es[2] * outs[2]
            ob[0, tok, :] = num * (1.0 / (es[0] + es[1] + es[2]))
        return carry

    lax.fori_loop(0, l4 // n, combine_body, 0)
    o_ref[...] = ob[0].astype(o_ref.dtype)


def _swa_prompt(qkv, b, t):
    n, s_tok = SWA_BLOCK, SWA_SUPER
    assert t % s_tok == 0 and [d for _, d in SWA_PATTERNS] == [1, QUAD, QUAD * QUAD]
    assert all(w // d == n for w, d in SWA_PATTERNS)
    l4, l16 = s_tok // QUAD, s_tok // (QUAD * QUAD)
    assert (s_tok // n) % SWA_TILES == 0 and SWA_TILES % (l4 // n) == 0 and SWA_TILES % (QUAD * l16 // n) == 0
    view = qkv.reshape(3, SWA_HEADS, b, t, SWA_HD)

    def part(idx):
        return pl.BlockSpec((None, None, None, s_tok, SWA_HD), lambda bi, h, sb: (idx, h, bi, sb, 0))

    def rows(*lead):
        return pltpu.VMEM(lead + (SWA_HD,), BF16)

    o = pl.pallas_call(
        _swa_prompt_kernel,
        grid=(b, SWA_HEADS, t // s_tok),
        in_specs=[part(0), part(1), part(2)],
        out_specs=pl.BlockSpec((None, s_tok, SWA_HD), lambda bi, h, sb: (bi, sb, h)),
        out_shape=jax.ShapeDtypeStruct((b, t, SWA_W), BF16),
        scratch_shapes=[
            pltpu.VMEM((3, QUAD, l4, SWA_HD), F32),
            rows(s_tok), rows(QUAD, l4), rows(QUAD * QUAD, l16),
            rows(n + s_tok), rows(QUAD, n + l4), rows(QUAD * QUAD, n + l16),
            rows(n + s_tok), rows(QUAD, n + l4), rows(QUAD * QUAD, n + l16),
            pltpu.VMEM((3, s_tok, SWA_HD), F32), pltpu.VMEM((3, s_tok, SWA_HD), F32),
            pltpu.VMEM((2 * len(SWA_PATTERNS), n, 2 * n), F32),
        ],
        compiler_params=_params(("parallel", "parallel", "arbitrary")),
        name="swa_prompt",
    )(view, view, view)
    return o.reshape(b * t, SWA_W)


def _branch_multiplicity(d):
    mult = jnp.zeros(d.shape, F32)
    for window, dil in SWA_PATTERNS:
        hit = (d >= 0) & (d <= window) & ((d & (dil - 1)) == 0)
        mult = mult + jnp.where(hit, 1.0, 0.0)
    return mult


SAMPLE_NEAR = SWA_PATTERNS[1][0]
SAMPLE_GROUP = SWA_PATTERNS[2][1]


def _swa_sample_kernel(q_ref, kn_ref, vn_ref, kfar_ref, vfar_ref, knear_ref, vnear_ref, o_ref, *, w, tn):
    tp = q_ref.shape[1]
    groups, per = kfar_ref.shape[1], kfar_ref.shape[2]
    n_far, n_near = groups * tn, knear_ref.shape[1] // SWA_HEADS
    kfar, vfar = kfar_ref.reshape(groups * per, SWA_HD), vfar_ref.reshape(groups * per, SWA_HD)

    def query_row(shape):
        return lax.broadcasted_iota(jnp.int32, shape, 0)

    e_far = lax.broadcasted_iota(jnp.int32, (tp, n_far), 1)
    pos_far = (e_far // tn) * SAMPLE_GROUP + e_far % tn
    pos_near = (w - n_near) + lax.broadcasted_iota(jnp.int32, (tp, n_near), 1)
    j_new = lax.broadcasted_iota(jnp.int32, (tp, tp), 1)
    dists = [w + query_row((tp, n_far)) - pos_far,
             w + query_row((tp, n_near)) - pos_near,
             jnp.where(j_new < tn, query_row((tp, tp)) - j_new, -1)]
    mults = [_branch_multiplicity(d) for d in dists]
    dists = [d.astype(F32) for d in dists]
    scale = SWA_HD ** -0.5

    for h in range(SWA_HEADS):
        hs = slice(h * SWA_HD, (h + 1) * SWA_HD)
        far_rows = pl.ds(h, n_far, stride=SWA_HEADS)
        near_rows = pl.ds(h, n_near, stride=SWA_HEADS)
        slope = ALIBI_SLOPES[h]
        q = q_ref[0, :, hs]
        keys = [kfar[far_rows, :].astype(BF16), knear_ref[0, near_rows, :].astype(BF16), kn_ref[0, :, hs]]
        values = [vfar[far_rows, :].astype(BF16), vnear_ref[0, near_rows, :].astype(BF16), vn_ref[0, :, hs]]
        scores = [jnp.where(mult > 0.0, _dot_nt(q, k) * scale - slope * d, NEG)
                  for k, d, mult in zip(keys, dists, mults)]
        m = functools.reduce(jnp.maximum, [jnp.max(s, axis=1, keepdims=True) for s in scores])
        probs = [mult * jnp.exp(s - m) for s, mult in zip(scores, mults)]
        l = functools.reduce(lambda a, b: a + b, [jnp.sum(p, axis=1, keepdims=True) for p in probs])
        acc = functools.reduce(lambda a, b: a + b, [_dot(p.astype(BF16), v) for p, v in zip(probs, values)])
        o_ref[0, :, hs] = (acc / l).astype(o_ref.dtype)


def _swa_sample(projp, cache_k, cache_v, *, tn):
    b, tp, _ = projp.shape
    rows = cache_k.shape[1]
    w = rows // SWA_HEADS
    group_rows = SAMPLE_GROUP * SWA_HEADS
    near_rows = SAMPLE_NEAR * SWA_HEADS
    assert w % SAMPLE_NEAR == 0 and SAMPLE_NEAR % SAMPLE_GROUP == 0 and tn <= SAMPLE_GROUP
    assert [wd for wd, _ in SWA_PATTERNS][:2] == [SWA_BLOCK, SAMPLE_NEAR] and w >= SWA_PATTERNS[2][0]
    far_groups = (w - SAMPLE_NEAR) // SAMPLE_GROUP
    grouped = (b, rows // group_rows, group_rows, SWA_HD)

    def new(cb):
        return pl.BlockSpec((1, tp, SWA_W), lambda bi: (bi, 0, cb))

    far = pl.BlockSpec((1, far_groups, tn * SWA_HEADS, SWA_HD), lambda bi: (bi, 0, 0, 0))
    near = pl.BlockSpec((1, near_rows, SWA_HD), lambda bi: (bi, rows // near_rows - 1, 0))
    return pl.pallas_call(
        functools.partial(_swa_sample_kernel, w=w, tn=tn),
        grid=(b,),
        in_specs=[new(0), new(1), new(2), far, far, near, near],
        out_specs=pl.BlockSpec((1, tp, SWA_W), lambda bi: (bi, 0, 0)),
        out_shape=jax.ShapeDtypeStruct((b, tp, SWA_W), BF16),
        compiler_params=_params(("parallel",)),
        name="swa_sample",
    )(projp, projp, projp, cache_k.reshape(grouped), cache_v.reshape(grouped), cache_k, cache_v)


def _outproj_kernel(x_ref, oa_ref, ob_ref, wa_ref, wb_ref, h_ref):
    h_ref[...] = x_ref[...] + _dot(oa_ref[...], wa_ref[...]) + _dot(ob_ref[...], wb_ref[...])


def _outproj(x, o_a, o_b, w_out, *, tm):
    m, d = x.shape
    return pl.pallas_call(
        _outproj_kernel,
        grid=(m // tm,),
        in_specs=[pl.BlockSpec((tm, d), lambda i: (i, 0)),
                  pl.BlockSpec((tm, GLA_V_W), lambda i: (i, 0)),
                  pl.BlockSpec((tm, SWA_W), lambda i: (i, 0)),
                  pl.BlockSpec((GLA_V_W, d), lambda i: (0, 0)),
                  pl.BlockSpec((SWA_W, d), lambda i: (GLA_V_W // SWA_W, 0))],
        out_specs=pl.BlockSpec((tm, d), lambda i: (i, 0)),
        out_shape=jax.ShapeDtypeStruct((m, d), F32),
        compiler_params=_params(("parallel",)),
        name="outproj",
    )(x, o_a, o_b, w_out, w_out)


FFN_HALO = 16
FFN_TF = 512
CACHE_PIECES = 4


def _ffn_prompt_kernel(h_ref, halo_ref, g_ref, wa_ref, wv_ref, cp_ref, wd_ref, gf_ref, kc_hbm, vc_hbm, kn_ref,
                       vn_ref, y_ref, u_ref, ko_hbm, vo_hbm, xn_scr, ea_scr, ev_scr, ks_scr, vs_scr, sems,
                       *, tiles_per_seq):
    i, j = pl.program_id(0), pl.program_id(1)
    tm = h_ref.shape[0]
    last_j = pl.num_programs(1) - 1

    nseq, rows = kc_hbm.shape[0], kc_hbm.shape[1]
    new = kn_ref.shape[1]
    old = rows - new
    piece = old // CACHE_PIECES
    moving = i < nseq

    def load(src_hbm, dst_scr, c, sem):
        part = pl.ds(c * piece, piece)
        return pltpu.make_async_copy(src_hbm.at[i, pl.ds(new + c * piece, piece)], dst_scr.at[part], sem)

    def store(src_scr, dst_hbm, c, sem):
        part = pl.ds(c * piece, piece)
        return pltpu.make_async_copy(src_scr.at[part], dst_hbm.at[i, part], sem)

    loads = [[load(kc_hbm, ks_scr, c, sems.at[0]), load(vc_hbm, vs_scr, c, sems.at[1])]
             for c in range(CACHE_PIECES)]
    stores = [[store(ks_scr, ko_hbm, c, sems.at[2]), store(vs_scr, vo_hbm, c, sems.at[3])]
              for c in range(CACHE_PIECES)]
    stores[0] += [pltpu.make_async_copy(kn_ref.at[0], ko_hbm.at[i, pl.ds(old, new)], sems.at[4]),
                  pltpu.make_async_copy(vn_ref.at[0], vo_hbm.at[i, pl.ds(old, new)], sems.at[5])]

    for c in range(CACHE_PIECES):
        @pl.when(moving & (j == c))
        def _():
            for cp in loads[c]:
                cp.start(priority=1)

        @pl.when(moving & (j == CACHE_PIECES + c))
        def _():
            if c == 0:
                for group in loads:
                    for cp in group:
                        cp.wait()
            for cp in stores[c]:
                cp.start(priority=1)

    @pl.when(j == 0)
    def _():
        first = (i % tiles_per_seq) == 0
        halo = _rms_rows(halo_ref[...], g_ref[...])
        xn_scr[0:FFN_HALO, :] = jnp.where(first, 0.0, halo).astype(BF16)
        xn_scr[FFN_HALO:, :] = _rms_rows(h_ref[...], g_ref[...]).astype(BF16)
        y_ref[...] = h_ref[...]

    xn = xn_scr[...]
    ea_scr[...] = _dot(xn, wa_ref[...])
    ev_scr[...] = _dot(xn, wv_ref[...])

    def conv(e_scr, part):
        out = cp_ref[part, CONV_W:CONV_W + 1, :]
        for tap in range(CONV_W):
            off = FFN_HALO - (CONV_W - 1) + tap
            out = out + cp_ref[part, tap:tap + 1, :] * e_scr[off:off + tm, :]
        return out

    gate = (_silu(conv(ea_scr, 0)) * conv(ev_scr, 1)).astype(BF16)
    y_ref[...] += _dot(gate, wd_ref[...])
    u_ref[0, j, 0] = ea_scr[FFN_HALO + tm - 8:FFN_HALO + tm, :]
    u_ref[0, j, 1] = ev_scr[FFN_HALO + tm - 8:FFN_HALO + tm, :]

    @pl.when(j == last_j)
    def _():
        y_ref[...] = _rms_rows(y_ref[...], gf_ref[...])

    @pl.when(moving & (j == last_j))
    def _():
        for group in stores:
            for cp in group:
                cp.wait()


def _ffn_prompt(h, g_ffn, w_up, conv_params, w_down, g_final, cache_k, cache_v, k_new, v_new, *, seq, tm):
    m, d = h.shape
    f = w_down.shape[0]
    tf = conv_params.shape[-1]
    nf = f // tf
    ntile = m // tm
    hb = tm // FFN_HALO
    nseq, rows, _ = cache_k.shape
    new = k_new.shape[1]
    assert nseq <= ntile and nf > 2 * CACHE_PIECES, "one cache per row tile, moved within its column steps"
    assert (rows - new) % (8 * CACHE_PIECES) == 0
    hbm = pl.BlockSpec(memory_space=pl.ANY)
    fresh = pl.BlockSpec((1, new, SWA_HD), lambda i, j: (jnp.minimum(i, nseq - 1), 0, 0))
    y, u_last, k_all, v_all = pl.pallas_call(
        functools.partial(_ffn_prompt_kernel, tiles_per_seq=seq // tm),
        grid=(ntile, nf),
        in_specs=[
            pl.BlockSpec((tm, d), lambda i, j: (i, 0)),
            pl.BlockSpec((FFN_HALO, d), lambda i, j: (jnp.maximum(i * hb - 1, 0), 0)),
            pl.BlockSpec((1, d), lambda i, j: (0, 0)),
            pl.BlockSpec((d, tf), lambda i, j: (0, j)),
            pl.BlockSpec((d, tf), lambda i, j: (0, nf + j)),
            pl.BlockSpec((None, 2, 8, tf), lambda i, j: (j, 0, 0, 0)),
            pl.BlockSpec((tf, d), lambda i, j: (j, 0)),
            pl.BlockSpec((1, d), lambda i, j: (0, 0)),
            hbm, hbm, fresh, fresh,
        ],
        out_specs=[
            pl.BlockSpec((tm, d), lambda i, j: (i, 0)),
            pl.BlockSpec((1, nf, 2, 8, tf), lambda i, j: (i, 0, 0, 0, 0)),
            hbm, hbm,
        ],
        out_shape=[jax.ShapeDtypeStruct((m, d), F32),
                   jax.ShapeDtypeStruct((ntile, nf, 2, 8, tf), F32),
                   jax.ShapeDtypeStruct(cache_k.shape, cache_k.dtype),
                   jax.ShapeDtypeStruct(cache_v.shape, cache_v.dtype)],
        scratch_shapes=[pltpu.VMEM((tm + FFN_HALO, d), BF16),
                        pltpu.VMEM((tm + FFN_HALO, tf), F32),
                        pltpu.VMEM((tm + FFN_HALO, tf), F32),
                        pltpu.VMEM((rows - new, SWA_HD), F32),
                        pltpu.VMEM((rows - new, SWA_HD), F32),
                        pltpu.SemaphoreType.DMA((6,))],
        compiler_params=_params(("arbitrary", "arbitrary")),
        name="convffn_prompt",
    )(h, h, g_ffn, w_up, w_up, conv_params, w_down, g_final, cache_k, cache_v, k_new, v_new)
    u_last = u_last.transpose(0, 3, 2, 1, 4).reshape(ntile, 8, 2 * f)
    return y, u_last, k_all, v_all


def _ffn_sample_kernel(h_ref, g_ref, sa_ref, sv_ref, wa_ref, wv_ref, ca_ref, cv_ref, wd_ref,
                       gf_ref, y_ref, na_ref, nv_ref, xn_scr, acc_scr, *, nb):
    j = pl.program_id(0)
    m = h_ref.shape[0]
    keep = (CONV_W - 1) * nb

    @pl.when(j == 0)
    def _():
        xn_scr[...] = _rms_rows(h_ref[...], g_ref[...]).astype(BF16)
        acc_scr[...] = jnp.zeros_like(acc_scr)

    xn = xn_scr[...]

    def conv(s_ref, w_ref, c_ref, n_ref):
        u = _dot(xn, w_ref[...])
        ext = jnp.concatenate([s_ref[...], u], axis=0)
        n_ref[...] = ext[m:m + keep, :]
        out = c_ref[CONV_W:CONV_W + 1, :]
        for tap in range(CONV_W):
            out = out + c_ref[tap:tap + 1, :] * ext[tap * nb:tap * nb + m, :]
        return out

    gate = (_silu(conv(sa_ref, wa_ref, ca_ref, na_ref)) * conv(sv_ref, wv_ref, cv_ref, nv_ref)).astype(BF16)
    acc_scr[...] += _dot(gate, wd_ref[...])

    @pl.when(j == pl.num_programs(0) - 1)
    def _():
        y_ref[...] = _rms_rows(h_ref[...] + acc_scr[...], gf_ref[...])


def _ffn_sample(h_tm, state_tm, g_ffn, w_up, conv_params, w_down, g_final, *, nb):
    m, d = h_tm.shape
    f = w_down.shape[0]
    tf = conv_params.shape[-1]
    nf = f // tf
    keep = (CONV_W - 1) * nb
    return pl.pallas_call(
        functools.partial(_ffn_sample_kernel, nb=nb),
        grid=(nf,),
        in_specs=[
            pl.BlockSpec((m, d), lambda j: (0, 0)),
            pl.BlockSpec((1, d), lambda j: (0, 0)),
            pl.BlockSpec((keep, tf), lambda j: (0, j)),
            pl.BlockSpec((keep, tf), lambda j: (0, nf + j)),
            pl.BlockSpec((d, tf), lambda j: (0, j)),
            pl.BlockSpec((d, tf), lambda j: (0, nf + j)),
            pl.BlockSpec((None, None, 8, tf), lambda j: (j, 0, 0, 0)),
            pl.BlockSpec((None, None, 8, tf), lambda j: (j, 1, 0, 0)),
            pl.BlockSpec((tf, d), lambda j: (j, 0)),
            pl.BlockSpec((1, d), lambda j: (0, 0)),
        ],
        out_specs=[
            pl.BlockSpec((m, d), lambda j: (0, 0)),
            pl.BlockSpec((keep, tf), lambda j: (0, j)),
            pl.BlockSpec((keep, tf), lambda j: (0, j)),
        ],
        out_shape=[jax.ShapeDtypeStruct((m, d), F32),
                   jax.ShapeDtypeStruct((keep, f), F32),
                   jax.ShapeDtypeStruct((keep, f), F32)],
        scratch_shapes=[pltpu.VMEM((m, d), BF16), pltpu.VMEM((m, d), F32)],
        compiler_params=_params(("arbitrary",)),
        name="convffn_sample",
    )(h_tm, g_ffn, state_tm, state_tm, w_up, w_up, conv_params, conv_params, w_down, g_final)


def _pack_weights(g_attn_norm, w_in, w_gate_up, b_gate, g_gla_norm, w_out, g_ffn_norm, w_up, w_conv, b_conv,
                  w_down):
    z0, z1 = GLA_COLS, GLA_COLS + GLA_GATE_RANK
    w_t = jnp.swapaxes(w_in, 1, 2)[0]
    w_main = jnp.concatenate([w_t[:z0], w_t[z1:]], axis=0).astype(BF16)
    w_z = jnp.pad(w_t[z0:z1], ((0, LANE - GLA_GATE_RANK), (0, 0))).astype(BF16)
    w_gate = jnp.pad(w_gate_up, ((0, LANE - GLA_GATE_RANK), (0, 0))).astype(BF16)
    conv_rows = jnp.concatenate([w_conv, b_conv[None, :], jnp.zeros((8 - CONV_W - 1, b_conv.shape[0]), F32)])
    conv_params = conv_rows.reshape(8, 2, -1, FFN_TF).transpose(2, 1, 0, 3)
    return dict(
        g_attn=g_attn_norm[None, :], w_main=w_main, w_z=w_z, w_gate=w_gate, b_gate=b_gate[None, :],
        g_gla=g_gla_norm[None, :], w_out=w_out.astype(BF16), g_ffn=g_ffn_norm[None, :],
        w_up=w_up.astype(BF16), conv_params=conv_params, w_down=w_down.astype(BF16))


def _prompt_layer(x, p, g_final, sample_cache):
    b, t, d = x.shape
    m = b * t
    f = p["w_down"].shape[0]
    keep = min(SWA_MAX_WINDOW, t)
    x2 = x.reshape(m, d)
    proj, loga, qkv, k_buf, v_buf = _inproj(x2, p["g_attn"], p["w_main"], p["w_z"], p["w_gate"], p["b_gate"],
                                            tm=1024, seq=t, keep=keep)
    o_a, s_fin = _gla(proj.reshape(b, t, GLA_COLS), loga.reshape(b, t, GLA_QK_W), p["g_gla"], None,
                      chunk=GLA_CHUNK, tc=512)
    o_b = _swa_prompt(qkv, b, t)
    h = _outproj(x2, o_a.reshape(m, GLA_V_W), o_b, p["w_out"], tm=512)
    tm = 512
    y, u_last, k_all, v_all = _ffn_prompt(h, p["g_ffn"], p["w_up"], p["conv_params"], p["w_down"], g_final,
                                          *sample_cache, seq=t, tm=tm)
    shape = (b, keep, SWA_HEADS, SWA_HD)
    last = u_last.reshape(b, t // tm, 8, 2 * f)[:, -1, 8 - (CONV_W - 1):, :]
    return y.reshape(b, t, d), s_fin, k_buf.reshape(shape), v_buf.reshape(shape), last, k_all, v_all


def _sample_layer(x, s0, conv_s0, cache_k, cache_v, p, g_final):
    b, tn, d = x.shape
    m = b * tn
    tp = 8
    w = cache_k.shape[1]
    x2 = x.reshape(m, d)
    proj, loga, qkv, k_new, v_new = _inproj(x2, p["g_attn"], p["w_main"], p["w_z"], p["w_gate"], p["b_gate"],
                                            tm=m, seq=m, keep=m)
    pad = ((0, 0), (0, tp - tn), (0, 0))
    projp = jnp.pad(proj.reshape(b, tn, GLA_COLS), pad)
    logap = jnp.pad(loga.reshape(b, tn, GLA_QK_W), pad)
    o_a, s_fin = _gla(projp, logap, p["g_gla"], s0, chunk=tp, tc=tp)
    qkvp = jnp.pad(qkv.transpose(2, 0, 1, 3).reshape(b, tn, 3 * SWA_W).astype(BF16), pad)
    rows = (b, tn * SWA_HEADS, SWA_HD)
    cache = (cache_k.reshape(b, w * SWA_HEADS, SWA_HD), cache_v.reshape(b, w * SWA_HEADS, SWA_HD),
             k_new.reshape(rows), v_new.reshape(rows))
    o_b = _swa_sample(qkvp, cache[0], cache[1], tn=tn)
    h = _outproj(x2, o_a[:, :tn].reshape(m, GLA_V_W), o_b[:, :tn].reshape(m, SWA_W), p["w_out"], tm=m)
    h_tm = h.reshape(b, tn, d).transpose(1, 0, 2).reshape(m, d)
    state_tm = conv_s0.transpose(1, 0, 2).reshape((CONV_W - 1) * b, -1)
    y_tm, na, nv = _ffn_sample(h_tm, state_tm, p["g_ffn"], p["w_up"], p["conv_params"], p["w_down"], g_final, nb=b)
    y = y_tm.reshape(tn, b, d).transpose(1, 0, 2)
    conv_new = jnp.concatenate([na, nv], axis=-1).reshape(CONV_W - 1, b, -1).transpose(1, 0, 2)
    return y, s_fin, conv_new, cache


def kernel(x_prompt, x_sample, state_gla, cache_swa_k, cache_swa_v, state_ffn_conv, g_attn_norm, w_in, w_gate_up,
           b_gate, g_gla_norm, w_out, g_ffn_norm, w_up, w_conv, b_conv, w_down, g_final):
    depth = w_in.shape[0]
    assert depth == 1, "the final RMSNorm is fused into the last layer's ConvFFN kernel"
    assert cache_swa_k.shape[2] == SWA_MAX_WINDOW and x_sample.shape[1] <= 8
    p = _pack_weights(g_attn_norm[0], w_in, w_gate_up[0], b_gate[0], g_gla_norm[0], w_out[0], g_ffn_norm[0],
                      w_up[0], w_conv[0], b_conv[0], w_down[0])
    gf = g_final[None, :]
    ys, s2, c2, sample_cache = _sample_layer(x_sample, state_gla[0], state_ffn_conv[0], cache_swa_k[0],
                                             cache_swa_v[0], p, gf)
    yp, s1, k1, v1, c1, k2, v2 = _prompt_layer(x_prompt, p, gf, sample_cache)
    k2, v2 = k2.reshape(cache_swa_k.shape[1:]), v2.reshape(cache_swa_v.shape[1:])
    return (yp, ys, s1[None], s2[None], k1[None], k2[None], v1[None], v2[None], c1[None], c2[None])
```

```python
import functools

import jax
import jax.numpy as jnp
from jax import lax
from jax.experimental import pallas as pl
from jax.experimental.pallas import tpu as pltpu

F32 = jnp.float32
BF16 = jnp.bfloat16

GLA_HEADS = 4
GLA_DK = 128
GLA_DV = 256
GLA_GATE_RANK = 16
GLA_TAU = 16.0
GLA_CHUNK = 64
SWA_HEADS = 8
SWA_HD = 128
SWA_PATTERNS = ((128, 1), (512, 4), (2048, 16))
SWA_BLOCK = 128
SWA_MAX_WINDOW = 2048
CONV_W = 3
EPS = 1e-6
NEG = -1e30

GLA_QK_W = GLA_HEADS * GLA_DK
GLA_V_W = GLA_HEADS * GLA_DV
SWA_W = SWA_HEADS * SWA_HD
PROJ_W = 2 * GLA_QK_W + 2 * GLA_V_W + 3 * SWA_W
LANE = 128
VMEM_LIMIT = 56 * 1024 * 1024

ALIBI_SLOPES = tuple(2.0 ** (-8.0 * (h + 1) / SWA_HEADS) for h in range(SWA_HEADS))


def _params(semantics):
    return pltpu.CompilerParams(dimension_semantics=semantics, vmem_limit_bytes=VMEM_LIMIT)


def _rms_rows(x, g):
    return x * lax.rsqrt(jnp.mean(x * x, axis=-1, keepdims=True) + EPS) * g


def _silu(x):
    return x / (1.0 + jnp.exp(-x))


def _log_sigmoid(x):
    return jnp.minimum(x, 0.0) - jnp.log1p(jnp.exp(-jnp.abs(x)))


def _dot(a, b):
    return jnp.dot(a, b, preferred_element_type=F32)


def _dot_nt(a, b):
    return lax.dot_general(a, b, (((1,), (1,)), ((), ())), preferred_element_type=F32)


def _dot_tn(a, b):
    return lax.dot_general(a, b, (((0,), (0,)), ((), ())), preferred_element_type=F32)


GLA_COLS = 2 * GLA_QK_W + 2 * GLA_V_W
GLA_STEPS = GLA_COLS // SWA_W


def _inproj_kernel(x_hbm, g_ref, w_ref, wz_ref, wg_ref, bg_ref, proj_ref, loga_ref, qkv_ref, kc_hbm, vc_hbm,
                   x_buf, xn_ref, kc_buf, vc_buf, sem, *, tiles_per_seq, kept_tiles):
    i, j = pl.program_id(0), pl.program_id(1)
    tm = x_buf.shape[0]
    last_i = pl.num_programs(0) - 1

    def x_copy(tile):
        return pltpu.make_async_copy(x_hbm.at[pl.ds(tile * tm, tm)], x_buf, sem.at[0])

    def kept(tile):
        return tile % tiles_per_seq >= tiles_per_seq - kept_tiles

    def cache_copy(buf, out_hbm, tile, s):
        block = (tile // tiles_per_seq) * kept_tiles + tile % tiles_per_seq - (tiles_per_seq - kept_tiles)
        rows = tm * SWA_HEADS
        return pltpu.make_async_copy(buf, out_hbm.at[pl.ds(block * rows, rows)], sem.at[s])

    @pl.when((j == 1) & (i > 0) & kept(i - 1))
    def _():
        cache_copy(vc_buf, vc_hbm, i - 1, 2).wait()

    @pl.when(j == 0)
    def _():
        @pl.when(i == 0)
        def _():
            x_copy(0).start(priority=1)

        x_copy(i).wait()
        xn_ref[...] = _rms_rows(x_buf[...], g_ref[...]).astype(BF16)

        @pl.when(i < last_i)
        def _():
            x_copy(i + 1).start(priority=1)

        za = _dot_nt(xn_ref[...], wz_ref[...])
        gate = _dot(za.astype(BF16), wg_ref[...]) + bg_ref[...]
        loga_ref[...] = _log_sigmoid(gate) / GLA_TAU

    @pl.when(j < GLA_STEPS)
    def _():
        proj_ref[...] = _dot_nt(xn_ref[...], w_ref[...]).astype(BF16)

    def swa_step(cache):
        res = _dot_nt(xn_ref[...], w_ref[...])
        for h in range(SWA_HEADS):
            qkv_ref[0, h] = res[:, h * SWA_HD:(h + 1) * SWA_HD]
        if cache is not None:
            buf, out_hbm, s = cache

            @pl.when(kept(i))
            def _():
                for h in range(SWA_HEADS):
                    buf[pl.ds(h, tm, stride=SWA_HEADS), :] = res[:, h * SWA_HD:(h + 1) * SWA_HD]
                cache_copy(buf, out_hbm, i, s).start(priority=1)

    for step, cache in enumerate((None, (kc_buf, kc_hbm, 1), (vc_buf, vc_hbm, 2))):
        pl.when(j == GLA_STEPS + step)(functools.partial(swa_step, cache))

    @pl.when((j == GLA_STEPS + 2) & kept(i))
    def _():
        cache_copy(kc_buf, kc_hbm, i, 1).wait()

        @pl.when(i == last_i)
        def _():
            cache_copy(vc_buf, vc_hbm, i, 2).wait()


def _inproj(x, g, w_main, w_z, w_gate, b_gate, *, tm, seq, keep):
    m, d = x.shape
    tn = SWA_W
    tps, kt = seq // tm, keep // tm
    assert seq % tm == 0 and keep % tm == 0 and m % seq == 0

    cache_rows = (m // seq) * keep * SWA_HEADS
    hbm = pl.BlockSpec(memory_space=pl.ANY)
    return pl.pallas_call(
        functools.partial(_inproj_kernel, tiles_per_seq=tps, kept_tiles=kt),
        grid=(m // tm, PROJ_W // tn),
        in_specs=[
            pl.BlockSpec(memory_space=pl.ANY),
            pl.BlockSpec((1, d), lambda i, j: (0, 0)),
            pl.BlockSpec((tn, d), lambda i, j: (j, 0)),
            pl.BlockSpec((LANE, d), lambda i, j: (0, 0)),
            pl.BlockSpec((LANE, GLA_QK_W), lambda i, j: (0, 0)),
            pl.BlockSpec((1, GLA_QK_W), lambda i, j: (0, 0)),
        ],
        out_specs=[
            pl.BlockSpec((tm, tn), lambda i, j: (i, jnp.minimum(j, GLA_STEPS - 1))),
            pl.BlockSpec((tm, GLA_QK_W), lambda i, j: (i, 0)),
            pl.BlockSpec((1, SWA_HEADS, tm, SWA_HD), lambda i, j: (jnp.maximum(j - GLA_STEPS, 0), 0, i, 0)),
            hbm, hbm,
        ],
        out_shape=[jax.ShapeDtypeStruct((m, GLA_COLS), BF16),
                   jax.ShapeDtypeStruct((m, GLA_QK_W), F32),
                   jax.ShapeDtypeStruct((3, SWA_HEADS, m, SWA_HD), F32),
                   jax.ShapeDtypeStruct((cache_rows, SWA_HD), F32),
                   jax.ShapeDtypeStruct((cache_rows, SWA_HD), F32)],
        scratch_shapes=[pltpu.VMEM((tm, d), F32), pltpu.VMEM((tm, d), BF16),
                        pltpu.VMEM((tm * SWA_HEADS, SWA_HD), F32), pltpu.VMEM((tm * SWA_HEADS, SWA_HD), F32),
                        pltpu.SemaphoreType.DMA((3,))],
        compiler_params=_params(("arbitrary", "arbitrary")),
        name="norm_inproj",
    )(x, g, w_main, w_z, w_gate, b_gate)


def _split3(x):
    hi = x.astype(BF16)
    r1 = x - hi.astype(F32)
    mid = r1.astype(BF16)
    lo = (r1 - mid.astype(F32)).astype(BF16)
    return hi, mid, lo


def _gla_kernel(*refs, chunk, nsub, has_s0):
    scratch = refs[-5:]
    s_scr, qd_scr, oi_scr, u_scr, dec_scr = scratch
    if has_s0:
        q_ref, k_ref, v_ref, ga_ref, la_ref, gn_ref, s0_ref, o_ref, sfin_ref = refs[:-5]
    else:
        q_ref, k_ref, v_ref, ga_ref, la_ref, gn_ref, o_ref, sfin_ref = refs[:-5]
    c = pl.program_id(1)

    @pl.when(c == 0)
    def _():
        if has_s0:
            s_scr[...] = s0_ref[0]
        else:
            s_scr[...] = jnp.zeros_like(s_scr)

    row = lax.broadcasted_iota(jnp.int32, (chunk, chunk), 0)
    col = lax.broadcasted_iota(jnp.int32, (chunk, chunk), 1)
    causal = row >= col
    tril = jnp.where(causal, 1.0, 0.0).astype(BF16)
    eye = (lax.broadcasted_iota(jnp.int32, (GLA_DK, GLA_DK), 0)
           == lax.broadcasted_iota(jnp.int32, (GLA_DK, GLA_DK), 1))

    group = min(nsub, 4)
    heads = range(GLA_HEADS)
    ks = [slice(h * GLA_DK, (h + 1) * GLA_DK) for h in heads]
    vs = [slice(h * GLA_DV, (h + 1) * GLA_DV) for h in heads]

    def local(c2, carry):
        subs = [c2 * group + a for a in range(group)]
        sls = [pl.ds(pl.multiple_of(i * chunk, chunk), chunk) for i in subs]
        pieces = [_split3(la_ref[0, sl, :]) for sl in sls]
        cums = [_dot(tril, hi) + _dot(tril, mid) + _dot(tril, lo) for hi, mid, lo in pieces]
        q_dec, k_dec, k_end, grow = [], [], [], []
        for sl, cum in zip(sls, cums):
            total = cum[chunk - 1:chunk, :]
            k = k_ref[0, sl, :].astype(F32)
            q_dec.append((q_ref[0, sl, :].astype(F32) * (GLA_DK ** -0.5) * jnp.exp(cum)).astype(BF16))
            k_dec.append((k * jnp.exp(-cum)).astype(BF16))
            k_end.append((k * jnp.exp(total - cum)).astype(BF16))
            grow.append(jnp.exp(total))
            qd_scr[sl, :] = q_dec[-1]
        units = [(a, h) for a in range(group) for h in heads]
        vals = {(a, h): v_ref[0, sls[a], vs[h]] for a, h in units}
        raw = {(a, h): _dot_nt(q_dec[a][:, ks[h]], k_dec[a][:, ks[h]]) for a, h in units}
        for a, h in units:
            u_scr[subs[a], h] = _dot_tn(k_end[a][:, ks[h]], vals[a, h])
        attn = {(a, h): jnp.where(causal, raw[a, h], 0.0).astype(BF16) for a, h in units}
        for a, h in units:
            oi_scr[sls[a], vs[h]] = _dot(attn[a, h], vals[a, h])
        for a, h in units:
            decay = jnp.sum(jnp.where(eye, grow[a][:, ks[h]], 0.0), axis=1, keepdims=True)
            dec_scr[subs[a], h] = jnp.broadcast_to(decay, (GLA_DK, GLA_DK))
        return carry

    if nsub == group:
        local(0, 0)
    else:
        lax.fori_loop(0, nsub // group, local, 0)

    for i in range(nsub):
        sl = slice(i * chunk, (i + 1) * chunk)
        states = [s_scr[h] for h in heads]
        inter = [_dot(qd_scr[sl, ks[h]], states[h].astype(BF16)) for h in heads]
        for h in heads:
            decay = dec_scr[i, h]
            s_scr[h] = states[h] * jnp.concatenate([decay] * (GLA_DV // GLA_DK), axis=1) + u_scr[i, h]
        for h in heads:
            on = _rms_rows(oi_scr[sl, vs[h]] + inter[h], gn_ref[:, vs[h]])
            o_ref[0, sl, vs[h]] = (on * _silu(ga_ref[0, sl, vs[h]].astype(F32))).astype(o_ref.dtype)

    @pl.when(c == pl.num_programs(1) - 1)
    def _():
        sfin_ref[0] = s_scr[...]


def _gla(proj3, loga3, g_norm, s0, *, chunk, tc):
    b, t, _ = proj3.shape
    nsub = tc // chunk
    state = pl.BlockSpec((1, GLA_HEADS, GLA_DK, GLA_DV), lambda bi, c: (bi, 0, 0, 0))
    in_specs = [
        pl.BlockSpec((1, tc, GLA_QK_W), lambda bi, c: (bi, c, 0)),
        pl.BlockSpec((1, tc, GLA_QK_W), lambda bi, c: (bi, c, 1)),
        pl.BlockSpec((1, tc, GLA_V_W), lambda bi, c: (bi, c, 1)),
        pl.BlockSpec((1, tc, GLA_V_W), lambda bi, c: (bi, c, 2)),
        pl.BlockSpec((1, tc, GLA_QK_W), lambda bi, c: (bi, c, 0)),
        pl.BlockSpec((1, GLA_V_W), lambda bi, c: (0, 0)),
    ]
    args = [proj3, proj3, proj3, proj3, loga3, g_norm]
    if s0 is not None:
        in_specs.append(state)
        args.append(s0)
    return pl.pallas_call(
        functools.partial(_gla_kernel, chunk=chunk, nsub=nsub, has_s0=s0 is not None),
        grid=(b, t // tc),
        in_specs=in_specs,
        out_specs=[pl.BlockSpec((1, tc, GLA_V_W), lambda bi, c: (bi, c, 0)), state],
        out_shape=[jax.ShapeDtypeStruct((b, t, GLA_V_W), BF16),
                   jax.ShapeDtypeStruct((b, GLA_HEADS, GLA_DK, GLA_DV), F32)],
        scratch_shapes=[pltpu.VMEM((GLA_HEADS, GLA_DK, GLA_DV), F32),
                        pltpu.VMEM((tc, GLA_QK_W), BF16),
                        pltpu.VMEM((tc, GLA_V_W), F32),
                        pltpu.VMEM((nsub, GLA_HEADS, GLA_DK, GLA_DV), F32),
                        pltpu.VMEM((nsub, GLA_HEADS, GLA_DK, GLA_DK), F32)],
        compiler_params=_params(("parallel", "arbitrary")),
        name="gla",
    )(*args)


SWA_SUPER = 2048
SWA_TILES = 16
SWA_GROUP = 8
QUAD = 4


def _swa_prompt_kernel(q_ref, k_ref, v_ref, o_ref, x4, q1, q4, q16, k1, k4, k16, v1, v4, v16, ob, lb, bias):
    n = SWA_BLOCK
    s_tok = q_ref.shape[0]
    l4, l16 = s_tok // QUAD, s_tok // (QUAD * QUAD)
    head, has_prev = pl.program_id(1), pl.program_id(2) > 0
    scale = SWA_HD ** -0.5

    slope = ALIBI_SLOPES[0]
    for h in range(1, SWA_HEADS):
        slope = jnp.where(head == h, ALIBI_SLOPES[h], slope)
    qi = lax.broadcasted_iota(jnp.int32, (n, 2 * n), 0)
    kj = lax.broadcasted_iota(jnp.int32, (n, 2 * n), 1)
    steps = qi + n - kj
    visible = (steps >= 0) & (steps <= n)
    first_visible = visible & (has_prev | (kj >= n))
    for g, (_, dil) in enumerate(SWA_PATTERNS):
        alibi = (-float(dil) * slope) * steps.astype(F32)
        bias[2 * g] = jnp.where(visible, alibi, NEG)
        bias[2 * g + 1] = jnp.where(first_visible, alibi, NEG)

    for buf, length in ((k1, s_tok), (v1, s_tok), (k4, l4), (v4, l4), (k16, l16), (v16, l16)):
        lead = buf.shape[:-2]
        idx = (slice(None),) * len(lead)

        @pl.when(has_prev)
        def _():
            buf[idx + (slice(0, n),)] = buf[idx + (slice(length, length + n),)]

        @pl.when(jnp.logical_not(has_prev))
        def _():
            buf[idx + (slice(0, n),)] = jnp.zeros(lead + (n, SWA_HD), BF16)

    for t, src in enumerate((q_ref, k_ref, v_ref)):
        for r in range(QUAD):
            x4[t, r] = src[pl.ds(r, l4, stride=QUAD), :]
    q1[...] = q_ref[...].astype(BF16)
    k1[n:, :] = k_ref[...].astype(BF16)
    v1[n:, :] = v_ref[...].astype(BF16)
    for r in range(QUAD):
        q4[r] = x4[0, r].astype(BF16)
        k4[r, n:, :] = x4[1, r].astype(BF16)
        v4[r, n:, :] = x4[2, r].astype(BF16)
    for r in range(QUAD * QUAD):
        sub = pl.ds(r // QUAD, l16, stride=QUAD)
        q16[r] = x4[0, r % QUAD, sub, :].astype(BF16)
        k16[r, n:, :] = x4[1, r % QUAD, sub, :].astype(BF16)
        v16[r, n:, :] = x4[2, r % QUAD, sub, :].astype(BF16)

    def run_tiles(tiles):
        for g0 in range(0, len(tiles), SWA_GROUP):
            group = tiles[g0:g0 + SWA_GROUP]
            scores = [_dot_nt(q(), keys()) * scale + table() for q, keys, _, table, _, _ in group]
            probs, lses = [], []
            for s in scores:
                m = jnp.max(s, axis=1, keepdims=True)
                p = jnp.exp(s - m)
                l = jnp.sum(p, axis=1, keepdims=True)
                probs.append(p.astype(BF16))
                lses.append((1.0 / l, m + jnp.log(l)))
            accs = [_dot(p, tile[2]()) for p, tile in zip(probs, group)]
            for acc, (inv, lse), (_, _, _, _, branch, rows) in zip(accs, lses, group):
                ob[branch, rows, :] = acc * inv
                lb[branch, rows, :] = jnp.broadcast_to(lse, (n, SWA_HD))

    def window(ref, lead, start, size):
        return lambda: ref[lead + (pl.ds(start, size), slice(None))]

    def d1_body(c, carry):
        tiles = []
        for a in range(SWA_TILES):
            i = c * SWA_TILES + a
            row0 = pl.multiple_of(i * n, n)
            tiles.append((window(q1, (), row0, n), window(k1, (), row0, 2 * n), window(v1, (), row0, 2 * n),
                          lambda i=i: bias[jnp.where(i == 0, 1, 0)], 0, pl.ds(row0, n)))
        run_tiles(tiles)
        return carry

    lax.fori_loop(0, s_tok // n // SWA_TILES, d1_body, 0)

    per4 = l4 // n

    def d4_body(c, carry):
        tiles = []
        for a in range(SWA_TILES // per4):
            r = c * (SWA_TILES // per4) + a
            for i in range(per4):
                tiles.append((window(q4, (r,), i * n, n), window(k4, (r,), i * n, 2 * n),
                              window(v4, (r,), i * n, 2 * n), lambda i=i: bias[2 + (1 if i == 0 else 0)], 1,
                              pl.ds(pl.multiple_of(r * l4 + i * n, n), n)))
        run_tiles(tiles)
        return carry

    lax.fori_loop(0, QUAD * per4 // SWA_TILES, d4_body, 0)

    per16 = l16 // n

    def d16_body(c, carry):
        tiles = []
        for a in range(SWA_TILES // per16):
            r4 = a % QUAD
            j = c * (SWA_TILES // per16 // QUAD) + a // QUAD
            r = r4 + QUAD * j
            for i in range(per16):
                tiles.append((window(q16, (r,), i * n, n), window(k16, (r,), i * n, 2 * n),
                              window(v16, (r,), i * n, 2 * n), lambda i=i: bias[4 + (1 if i == 0 else 0)], 2,
                              pl.ds(r4 * l4 + QUAD * i * n + j, n, stride=QUAD)))
        run_tiles(tiles)
        return carry

    lax.fori_loop(0, QUAD * QUAD * per16 // SWA_TILES, d16_body, 0)

    def combine_body(c, carry):
        for r in range(QUAD):
            tok = pl.ds(r + QUAD * c * n, n, stride=QUAD)
            cls = pl.ds(pl.multiple_of(r * l4 + c * n, n), n)
            lses = [lb[0, tok, :], lb[1, cls, :], lb[2, cls, :]]
            outs = [ob[0, tok, :], ob[1, cls, :], ob[2, cls, :]]
            m = jnp.maximum(jnp.maximum(lses[0], lses[1]), lses[2])
            es = [jnp.exp(x - m) for x in lses]
            num = es[0] * outs[0] + es[1] * outs[1] + es[2] * outs[2]
            ob[0, tok, :] = num * (1.0 / (es[0] + es[1] + es[2]))
        return carry

    lax.fori_loop(0, l4 // n, combine_body, 0)
    o_ref[...] = ob[0].astype(o_ref.dtype)


def _swa_prompt(qkv, b, t):
    n, s_tok = SWA_BLOCK, SWA_SUPER
    assert t % s_tok == 0 and [d for _, d in SWA_PATTERNS] == [1, QUAD, QUAD * QUAD]
    assert all(w // d == n for w, d in SWA_PATTERNS)
    l4, l16 = s_tok // QUAD, s_tok // (QUAD * QUAD)
    assert (s_tok // n) % SWA_TILES == 0 and SWA_TILES % (l4 // n) == 0 and SWA_TILES % (QUAD * l16 // n) == 0
    view = qkv.reshape(3, SWA_HEADS, b, t, SWA_HD)

    def part(idx):
        return pl.BlockSpec((None, None, None, s_tok, SWA_HD), lambda bi, h, sb: (idx, h, bi, sb, 0))

    def rows(*lead):
        return pltpu.VMEM(lead + (SWA_HD,), BF16)

    o = pl.pallas_call(
        _swa_prompt_kernel,
        grid=(b, SWA_HEADS, t // s_tok),
        in_specs=[part(0), part(1), part(2)],
        out_specs=pl.BlockSpec((None, s_tok, SWA_HD), lambda bi, h, sb: (bi, sb, h)),
        out_shape=jax.ShapeDtypeStruct((b, t, SWA_W), BF16),
        scratch_shapes=[
            pltpu.VMEM((3, QUAD, l4, SWA_HD), F32),
            rows(s_tok), rows(QUAD, l4), rows(QUAD * QUAD, l16),
            rows(n + s_tok), rows(QUAD, n + l4), rows(QUAD * QUAD, n + l16),
            rows(n + s_tok), rows(QUAD, n + l4), rows(QUAD * QUAD, n + l16),
            pltpu.VMEM((3, s_tok, SWA_HD), F32), pltpu.VMEM((3, s_tok, SWA_HD), F32),
            pltpu.VMEM((2 * len(SWA_PATTERNS), n, 2 * n), F32),
        ],
        compiler_params=_params(("parallel", "parallel", "arbitrary")),
        name="swa_prompt",
    )(view, view, view)
    return o.reshape(b * t, SWA_W)


def _branch_multiplicity(d):
    mult = jnp.zeros(d.shape, F32)
    for window, dil in SWA_PATTERNS:
        hit = (d >= 0) & (d <= window) & ((d & (dil - 1)) == 0)
        mult = mult + jnp.where(hit, 1.0, 0.0)
    return mult


SAMPLE_NEAR = SWA_PATTERNS[1][0]
SAMPLE_GROUP = SWA_PATTERNS[2][1]


def _swa_sample_kernel(q_ref, kn_ref, vn_ref, kfar_ref, vfar_ref, knear_ref, vnear_ref, o_ref, *, w, tn):
    tp = q_ref.shape[1]
    groups, per = kfar_ref.shape[1], kfar_ref.shape[2]
    n_far, n_near = groups * tn, knear_ref.shape[1] // SWA_HEADS
    kfar, vfar = kfar_ref.reshape(groups * per, SWA_HD), vfar_ref.reshape(groups * per, SWA_HD)

    def query_row(shape):
        return lax.broadcasted_iota(jnp.int32, shape, 0)

    e_far = lax.broadcasted_iota(jnp.int32, (tp, n_far), 1)
    pos_far = (e_far // tn) * SAMPLE_GROUP + e_far % tn
    pos_near = (w - n_near) + lax.broadcasted_iota(jnp.int32, (tp, n_near), 1)
    j_new = lax.broadcasted_iota(jnp.int32, (tp, tp), 1)
    dists = [w + query_row((tp, n_far)) - pos_far,
             w + query_row((tp, n_near)) - pos_near,
             jnp.where(j_new < tn, query_row((tp, tp)) - j_new, -1)]
    mults = [_branch_multiplicity(d) for d in dists]
    dists = [d.astype(F32) for d in dists]
    scale = SWA_HD ** -0.5

    for h in range(SWA_HEADS):
        hs = slice(h * SWA_HD, (h + 1) * SWA_HD)
        far_rows = pl.ds(h, n_far, stride=SWA_HEADS)
        near_rows = pl.ds(h, n_near, stride=SWA_HEADS)
        slope = ALIBI_SLOPES[h]
        q = q_ref[0, :, hs]
        keys = [kfar[far_rows, :].astype(BF16), knear_ref[0, near_rows, :].astype(BF16), kn_ref[0, :, hs]]
        values = [vfar[far_rows, :].astype(BF16), vnear_ref[0, near_rows, :].astype(BF16), vn_ref[0, :, hs]]
        scores = [jnp.where(mult > 0.0, _dot_nt(q, k) * scale - slope * d, NEG)
                  for k, d, mult in zip(keys, dists, mults)]
        m = functools.reduce(jnp.maximum, [jnp.max(s, axis=1, keepdims=True) for s in scores])
        probs = [mult * jnp.exp(s - m) for s, mult in zip(scores, mults)]
        l = functools.reduce(lambda a, b: a + b, [jnp.sum(p, axis=1, keepdims=True) for p in probs])
        acc = functools.reduce(lambda a, b: a + b, [_dot(p.astype(BF16), v) for p, v in zip(probs, values)])
        o_ref[0, :, hs] = (acc / l).astype(o_ref.dtype)


def _swa_sample(projp, cache_k, cache_v, *, tn):
    b, tp, _ = projp.shape
    rows = cache_k.shape[1]
    w = rows // SWA_HEADS
    group_rows = SAMPLE_GROUP * SWA_HEADS
    near_rows = SAMPLE_NEAR * SWA_HEADS
    assert w % SAMPLE_NEAR == 0 and SAMPLE_NEAR % SAMPLE_GROUP == 0 and tn <= SAMPLE_GROUP
    assert [wd for wd, _ in SWA_PATTERNS][:2] == [SWA_BLOCK, SAMPLE_NEAR] and w >= SWA_PATTERNS[2][0]
    far_groups = (w - SAMPLE_NEAR) // SAMPLE_GROUP
    grouped = (b, rows // group_rows, group_rows, SWA_HD)

    def new(cb):
        return pl.BlockSpec((1, tp, SWA_W), lambda bi: (bi, 0, cb))

    far = pl.BlockSpec((1, far_groups, tn * SWA_HEADS, SWA_HD), lambda bi: (bi, 0, 0, 0))
    near = pl.BlockSpec((1, near_rows, SWA_HD), lambda bi: (bi, rows // near_rows - 1, 0))
    return pl.pallas_call(
        functools.partial(_swa_sample_kernel, w=w, tn=tn),
        grid=(b,),
        in_specs=[new(0), new(1), new(2), far, far, near, near],
        out_specs=pl.BlockSpec((1, tp, SWA_W), lambda bi: (bi, 0, 0)),
        out_shape=jax.ShapeDtypeStruct((b, tp, SWA_W), BF16),
        compiler_params=_params(("parallel",)),
        name="swa_sample",
    )(projp, projp, projp, cache_k.reshape(grouped), cache_v.reshape(grouped), cache_k, cache_v)


def _outproj_kernel(x_ref, oa_ref, ob_ref, wa_ref, wb_ref, h_ref):
    h_ref[...] = x_ref[...] + _dot(oa_ref[...], wa_ref[...]) + _dot(ob_ref[...], wb_ref[...])


def _outproj(x, o_a, o_b, w_out, *, tm):
    m, d = x.shape
    return pl.pallas_call(
        _outproj_kernel,
        grid=(m // tm,),
        in_specs=[pl.BlockSpec((tm, d), lambda i: (i, 0)),
                  pl.BlockSpec((tm, GLA_V_W), lambda i: (i, 0)),
                  pl.BlockSpec((tm, SWA_W), lambda i: (i, 0)),
                  pl.BlockSpec((GLA_V_W, d), lambda i: (0, 0)),
                  pl.BlockSpec((SWA_W, d), lambda i: (GLA_V_W // SWA_W, 0))],
        out_specs=pl.BlockSpec((tm, d), lambda i: (i, 0)),
        out_shape=jax.ShapeDtypeStruct((m, d), F32),
        compiler_params=_params(("parallel",)),
        name="outproj",
    )(x, o_a, o_b, w_out, w_out)


FFN_HALO = 16
FFN_TF = 512
CACHE_PIECES = 4


def _ffn_prompt_kernel(h_ref, halo_ref, g_ref, wa_ref, wv_ref, cp_ref, wd_ref, gf_ref, kc_hbm, vc_hbm, kn_ref,
                       vn_ref, y_ref, u_ref, ko_hbm, vo_hbm, xn_scr, ea_scr, ev_scr, ks_scr, vs_scr, sems,
                       *, tiles_per_seq):
    i, j = pl.program_id(0), pl.program_id(1)
    tm = h_ref.shape[0]
    last_j = pl.num_programs(1) - 1

    nseq, rows = kc_hbm.shape[0], kc_hbm.shape[1]
    new = kn_ref.shape[1]
    old = rows - new
    piece = old // CACHE_PIECES
    moving = i < nseq

    def load(src_hbm, dst_scr, c, sem):
        part = pl.ds(c * piece, piece)
        return pltpu.make_async_copy(src_hbm.at[i, pl.ds(new + c * piece, piece)], dst_scr.at[part], sem)

    def store(src_scr, dst_hbm, c, sem):
        part = pl.ds(c * piece, piece)
        return pltpu.make_async_copy(src_scr.at[part], dst_hbm.at[i, part], sem)

    loads = [[load(kc_hbm, ks_scr, c, sems.at[0]), load(vc_hbm, vs_scr, c, sems.at[1])]
             for c in range(CACHE_PIECES)]
    stores = [[store(ks_scr, ko_hbm, c, sems.at[2]), store(vs_scr, vo_hbm, c, sems.at[3])]
              for c in range(CACHE_PIECES)]
    stores[0] += [pltpu.make_async_copy(kn_ref.at[0], ko_hbm.at[i, pl.ds(old, new)], sems.at[4]),
                  pltpu.make_async_copy(vn_ref.at[0], vo_hbm.at[i, pl.ds(old, new)], sems.at[5])]

    for c in range(CACHE_PIECES):
        @pl.when(moving & (j == c))
        def _():
            for cp in loads[c]:
                cp.start(priority=1)

        @pl.when(moving & (j == CACHE_PIECES + c))
        def _():
            if c == 0:
                for group in loads:
                    for cp in group:
                        cp.wait()
            for cp in stores[c]:
                cp.start(priority=1)

    @pl.when(j == 0)
    def _():
        first = (i % tiles_per_seq) == 0
        halo = _rms_rows(halo_ref[...], g_ref[...])
        xn_scr[0:FFN_HALO, :] = jnp.where(first, 0.0, halo).astype(BF16)
        xn_scr[FFN_HALO:, :] = _rms_rows(h_ref[...], g_ref[...]).astype(BF16)
        y_ref[...] = h_ref[...]

    xn = xn_scr[...]
    ea_scr[...] = _dot(xn, wa_ref[...])
    ev_scr[...] = _dot(xn, wv_ref[...])

    def conv(e_scr, part):
        out = cp_ref[part, CONV_W:CONV_W + 1, :]
        for tap in range(CONV_W):
            off = FFN_HALO - (CONV_W - 1) + tap
            out = out + cp_ref[part, tap:tap + 1, :] * e_scr[off:off + tm, :]
        return out

    gate = (_silu(conv(ea_scr, 0)) * conv(ev_scr, 1)).astype(BF16)
    y_ref[...] += _dot(gate, wd_ref[...])
    u_ref[0, j, 0] = ea_scr[FFN_HALO + tm - 8:FFN_HALO + tm, :]
    u_ref[0, j, 1] = ev_scr[FFN_HALO + tm - 8:FFN_HALO + tm, :]

    @pl.when(j == last_j)
    def _():
        y_ref[...] = _rms_rows(y_ref[...], gf_ref[...])

    @pl.when(moving & (j == last_j))
    def _():
        for group in stores:
            for cp in group:
                cp.wait()


def _ffn_prompt(h, g_ffn, w_up, conv_params, w_down, g_final, cache_k, cache_v, k_new, v_new, *, seq, tm):
    m, d = h.shape
    f = w_down.shape[0]
    tf = conv_params.shape[-1]
    nf = f // tf
    ntile = m // tm
    hb = tm // FFN_HALO
    nseq, rows, _ = cache_k.shape
    new = k_new.shape[1]
    assert nseq <= ntile and nf > 2 * CACHE_PIECES, "one cache per row tile, moved within its column steps"
    assert (rows - new) % (8 * CACHE_PIECES) == 0
    hbm = pl.BlockSpec(memory_space=pl.ANY)
    fresh = pl.BlockSpec((1, new, SWA_HD), lambda i, j: (jnp.minimum(i, nseq - 1), 0, 0))
    y, u_last, k_all, v_all = pl.pallas_call(
        functools.partial(_ffn_prompt_kernel, tiles_per_seq=seq // tm),
        grid=(ntile, nf),
        in_specs=[
            pl.BlockSpec((tm, d), lambda i, j: (i, 0)),
            pl.BlockSpec((FFN_HALO, d), lambda i, j: (jnp.maximum(i * hb - 1, 0), 0)),
            pl.BlockSpec((1, d), lambda i, j: (0, 0)),
            pl.BlockSpec((d, tf), lambda i, j: (0, j)),
            pl.BlockSpec((d, tf), lambda i, j: (0, nf + j)),
            pl.BlockSpec((None, 2, 8, tf), lambda i, j: (j, 0, 0, 0)),
            pl.BlockSpec((tf, d), lambda i, j: (j, 0)),
            pl.BlockSpec((1, d), lambda i, j: (0, 0)),
            hbm, hbm, fresh, fresh,
        ],
        out_specs=[
            pl.BlockSpec((tm, d), lambda i, j: (i, 0)),
            pl.BlockSpec((1, nf, 2, 8, tf), lambda i, j: (i, 0, 0, 0, 0)),
            hbm, hbm,
        ],
        out_shape=[jax.ShapeDtypeStruct((m, d), F32),
                   jax.ShapeDtypeStruct((ntile, nf, 2, 8, tf), F32),
                   jax.ShapeDtypeStruct(cache_k.shape, cache_k.dtype),
                   jax.ShapeDtypeStruct(cache_v.shape, cache_v.dtype)],
        scratch_shapes=[pltpu.VMEM((tm + FFN_HALO, d), BF16),
                        pltpu.VMEM((tm + FFN_HALO, tf), F32),
                        pltpu.VMEM((tm + FFN_HALO, tf), F32),
                        pltpu.VMEM((rows - new, SWA_HD), F32),
                        pltpu.VMEM((rows - new, SWA_HD), F32),
                        pltpu.SemaphoreType.DMA((6,))],
        compiler_params=_params(("arbitrary", "arbitrary")),
        name="convffn_prompt",
    )(h, h, g_ffn, w_up, w_up, conv_params, w_down, g_final, cache_k, cache_v, k_new, v_new)
    u_last = u_last.transpose(0, 3, 2, 1, 4).reshape(ntile, 8, 2 * f)
    return y, u_last, k_all, v_all


def _ffn_sample_kernel(h_ref, g_ref, sa_ref, sv_ref, wa_ref, wv_ref, ca_ref, cv_ref, wd_ref,
                       gf_ref, y_ref, na_ref, nv_ref, xn_scr, acc_scr, *, nb):
    j = pl.program_id(0)
    m = h_ref.shape[0]
    keep = (CONV_W - 1) * nb

    @pl.when(j == 0)
    def _():
        xn_scr[...] = _rms_rows(h_ref[...], g_ref[...]).astype(BF16)
        acc_scr[...] = jnp.zeros_like(acc_scr)

    xn = xn_scr[...]

    def conv(s_ref, w_ref, c_ref, n_ref):
        u = _dot(xn, w_ref[...])
        ext = jnp.concatenate([s_ref[...], u], axis=0)
        n_ref[...] = ext[m:m + keep, :]
        out = c_ref[CONV_W:CONV_W + 1, :]
        for tap in range(CONV_W):
            out = out + c_ref[tap:tap + 1, :] * ext[tap * nb:tap * nb + m, :]
        return out

    gate = (_silu(conv(sa_ref, wa_ref, ca_ref, na_ref)) * conv(sv_ref, wv_ref, cv_ref, nv_ref)).astype(BF16)
    acc_scr[...] += _dot(gate, wd_ref[...])

    @pl.when(j == pl.num_programs(0) - 1)
    def _():
        y_ref[...] = _rms_rows(h_ref[...] + acc_scr[...], gf_ref[...])


def _ffn_sample(h_tm, state_tm, g_ffn, w_up, conv_params, w_down, g_final, *, nb):
    m, d = h_tm.shape
    f = w_down.shape[0]
    tf = conv_params.shape[-1]
    nf = f // tf
    keep = (CONV_W - 1) * nb
    return pl.pallas_call(
        functools.partial(_ffn_sample_kernel, nb=nb),
        grid=(nf,),
        in_specs=[
            pl.BlockSpec((m, d), lambda j: (0, 0)),
            pl.BlockSpec((1, d), lambda j: (0, 0)),
            pl.BlockSpec((keep, tf), lambda j: (0, j)),
            pl.BlockSpec((keep, tf), lambda j: (0, nf + j)),
            pl.BlockSpec((d, tf), lambda j: (0, j)),
            pl.BlockSpec((d, tf), lambda j: (0, nf + j)),
            pl.BlockSpec((None, None, 8, tf), lambda j: (j, 0, 0, 0)),
            pl.BlockSpec((None, None, 8, tf), lambda j: (j, 1, 0, 0)),
            pl.BlockSpec((tf, d), lambda j: (j, 0)),
            pl.BlockSpec((1, d), lambda j: (0, 0)),
        ],
        out_specs=[
            pl.BlockSpec((m, d), lambda j: (0, 0)),
            pl.BlockSpec((keep, tf), lambda j: (0, j)),
            pl.BlockSpec((keep, tf), lambda j: (0, j)),
        ],
        out_shape=[jax.ShapeDtypeStruct((m, d), F32),
                   jax.ShapeDtypeStruct((keep, f), F32),
                   jax.ShapeDtypeStruct((keep, f), F32)],
        scratch_shapes=[pltpu.VMEM((m, d), BF16), pltpu.VMEM((m, d), F32)],
        compiler_params=_params(("arbitrary",)),
        name="convffn_sample",
    )(h_tm, g_ffn, state_tm, state_tm, w_up, w_up, conv_params, conv_params, w_down, g_final)


def _cast_rows_kernel(src_ref, dst_ref):
    dst_ref[...] = src_ref[...].astype(dst_ref.dtype)


def _pack_w_in(w_t, *, tr):
    n, d = w_t.shape
    z0 = GLA_COLS
    assert n == PROJ_W + GLA_GATE_RANK and z0 % tr == 0 and PROJ_W % tr == 0
    return pl.pallas_call(
        _cast_rows_kernel,
        grid=(PROJ_W // tr,),
        in_specs=[pl.BlockSpec((pl.Element(tr), pl.Element(d)),
                               lambda i: (pl.multiple_of(i * tr + jnp.where(i * tr >= z0, GLA_GATE_RANK, 0), 8),
                                          0))],
        out_specs=pl.BlockSpec((tr, d), lambda i: (i, 0)),
        out_shape=jax.ShapeDtypeStruct((PROJ_W, d), BF16),
        compiler_params=_params(("parallel",)),
        name="pack_w_in",
    )(w_t)


def _pack_weights(g_attn_norm, w_in, w_gate_up, b_gate, g_gla_norm, w_out, g_ffn_norm, w_up, w_conv, b_conv,
                  w_down):
    z0, z1 = GLA_COLS, GLA_COLS + GLA_GATE_RANK
    w_t = jnp.swapaxes(w_in, 1, 2)[0]
    w_main = _pack_w_in(w_t, tr=512)
    w_z = jnp.pad(w_t[z0:z1], ((0, LANE - GLA_GATE_RANK), (0, 0))).astype(BF16)
    w_gate = jnp.pad(w_gate_up, ((0, LANE - GLA_GATE_RANK), (0, 0))).astype(BF16)
    conv_rows = jnp.concatenate([w_conv, b_conv[None, :], jnp.zeros((8 - CONV_W - 1, b_conv.shape[0]), F32)])
    conv_params = conv_rows.reshape(8, 2, -1, FFN_TF).transpose(2, 1, 0, 3)
    return dict(
        g_attn=g_attn_norm[None, :], w_main=w_main, w_z=w_z, w_gate=w_gate, b_gate=b_gate[None, :],
        g_gla=g_gla_norm[None, :], w_out=w_out.astype(BF16), g_ffn=g_ffn_norm[None, :],
        w_up=w_up.astype(BF16), conv_params=conv_params, w_down=w_down.astype(BF16))


def _prompt_layer(x, p, g_final, sample_cache):
    b, t, d = x.shape
    m = b * t
    f = p["w_down"].shape[0]
    keep = min(SWA_MAX_WINDOW, t)
    x2 = x.reshape(m, d)
    proj, loga, qkv, k_buf, v_buf = _inproj(x2, p["g_attn"], p["w_main"], p["w_z"], p["w_gate"], p["b_gate"],
                                            tm=1024, seq=t, keep=keep)
    o_a, s_fin = _gla(proj.reshape(b, t, GLA_COLS), loga.reshape(b, t, GLA_QK_W), p["g_gla"], None,
                      chunk=GLA_CHUNK, tc=512)
    o_b = _swa_prompt(qkv, b, t)
    h = _outproj(x2, o_a.reshape(m, GLA_V_W), o_b, p["w_out"], tm=512)
    tm = 512
    y, u_last, k_all, v_all = _ffn_prompt(h, p["g_ffn"], p["w_up"], p["conv_params"], p["w_down"], g_final,
                                          *sample_cache, seq=t, tm=tm)
    shape = (b, keep, SWA_HEADS, SWA_HD)
    last = u_last.reshape(b, t // tm, 8, 2 * f)[:, -1, 8 - (CONV_W - 1):, :]
    return y.reshape(b, t, d), s_fin, k_buf.reshape(shape), v_buf.reshape(shape), last, k_all, v_all


def _sample_layer(x, s0, conv_s0, cache_k, cache_v, p, g_final):
    b, tn, d = x.shape
    m = b * tn
    tp = 8
    w = cache_k.shape[1]
    x2 = x.reshape(m, d)
    proj, loga, qkv, k_new, v_new = _inproj(x2, p["g_attn"], p["w_main"], p["w_z"], p["w_gate"], p["b_gate"],
                                            tm=m, seq=m, keep=m)
    pad = ((0, 0), (0, tp - tn), (0, 0))
    projp = jnp.pad(proj.reshape(b, tn, GLA_COLS), pad)
    logap = jnp.pad(loga.reshape(b, tn, GLA_QK_W), pad)
    o_a, s_fin = _gla(projp, logap, p["g_gla"], s0, chunk=tp, tc=tp)
    qkvp = jnp.pad(qkv.transpose(2, 0, 1, 3).reshape(b, tn, 3 * SWA_W).astype(BF16), pad)
    rows = (b, tn * SWA_HEADS, SWA_HD)
    cache = (cache_k.reshape(b, w * SWA_HEADS, SWA_HD), cache_v.reshape(b, w * SWA_HEADS, SWA_HD),
             k_new.reshape(rows), v_new.reshape(rows))
    o_b = _swa_sample(qkvp, cache[0], cache[1], tn=tn)
    h = _outproj(x2, o_a[:, :tn].reshape(m, GLA_V_W), o_b[:, :tn].reshape(m, SWA_W), p["w_out"], tm=m)
    h_tm = h.reshape(b, tn, d).transpose(1, 0, 2).reshape(m, d)
    state_tm = conv_s0.transpose(1, 0, 2).reshape((CONV_W - 1) * b, -1)
    y_tm, na, nv = _ffn_sample(h_tm, state_tm, p["g_ffn"], p["w_up"], p["conv_params"], p["w_down"], g_final, nb=b)
    y = y_tm.reshape(tn, b, d).transpose(1, 0, 2)
    conv_new = jnp.concatenate([na, nv], axis=-1).reshape(CONV_W - 1, b, -1).transpose(1, 0, 2)
    return y, s_fin, conv_new, cache


def kernel(x_prompt, x_sample, state_gla, cache_swa_k, cache_swa_v, state_ffn_conv, g_attn_norm, w_in, w_gate_up,
           b_gate, g_gla_norm, w_out, g_ffn_norm, w_up, w_conv, b_conv, w_down, g_final):
    depth = w_in.shape[0]
    assert depth == 1, "the final RMSNorm is fused into the last layer's ConvFFN kernel"
    assert cache_swa_k.shape[2] == SWA_MAX_WINDOW and x_sample.shape[1] <= 8
    p = _pack_weights(g_attn_norm[0], w_in, w_gate_up[0], b_gate[0], g_gla_norm[0], w_out[0], g_ffn_norm[0],
                      w_up[0], w_conv[0], b_conv[0], w_down[0])
    gf = g_final[None, :]
    ys, s2, c2, sample_cache = _sample_layer(x_sample, state_gla[0], state_ffn_conv[0], cache_swa_k[0],
                                             cache_swa_v[0], p, gf)
    yp, s1, k1, v1, c1, k2, v2 = _prompt_layer(x_prompt, p, gf, sample_cache)
    k2, v2 = k2.reshape(cache_swa_k.shape[1:]), v2.reshape(cache_swa_v.shape[1:])
    return (yp, ys, s1[None], s2[None], k1[None], k2[None], v1[None], v2[None], c1[None], c2[None])
```

```python
import functools

import jax
import jax.numpy as jnp
from jax import lax
from jax.experimental import pallas as pl
from jax.experimental.pallas import tpu as pltpu

F32 = jnp.float32
BF16 = jnp.bfloat16

GLA_HEADS = 4
GLA_DK = 128
GLA_DV = 256
GLA_GATE_RANK = 16
GLA_TAU = 16.0
GLA_CHUNK = 64
SWA_HEADS = 8
SWA_HD = 128
SWA_PATTERNS = ((128, 1), (512, 4), (2048, 16))
SWA_BLOCK = 128
SWA_MAX_WINDOW = 2048
CONV_W = 3
EPS = 1e-6
NEG = -1e30

GLA_QK_W = GLA_HEADS * GLA_DK
GLA_V_W = GLA_HEADS * GLA_DV
SWA_W = SWA_HEADS * SWA_HD
PROJ_W = 2 * GLA_QK_W + 2 * GLA_V_W + 3 * SWA_W
LANE = 128
VMEM_LIMIT = 56 * 1024 * 1024

ALIBI_SLOPES = tuple(2.0 ** (-8.0 * (h + 1) / SWA_HEADS) for h in range(SWA_HEADS))


def _params(semantics):
    return pltpu.CompilerParams(dimension_semantics=semantics, vmem_limit_bytes=VMEM_LIMIT)


def _rms_rows(x, g):
    return x * lax.rsqrt(jnp.mean(x * x, axis=-1, keepdims=True) + EPS) * g


def _silu(x):
    return x / (1.0 + jnp.exp(-x))


def _log_sigmoid(x):
    return jnp.minimum(x, 0.0) - jnp.log1p(jnp.exp(-jnp.abs(x)))


def _dot(a, b):
    return jnp.dot(a, b, preferred_element_type=F32)


def _dot_nt(a, b):
    return lax.dot_general(a, b, (((1,), (1,)), ((), ())), preferred_element_type=F32)


def _dot_tn(a, b):
    return lax.dot_general(a, b, (((0,), (0,)), ((), ())), preferred_element_type=F32)


GLA_COLS = 2 * GLA_QK_W + 2 * GLA_V_W
GLA_STEPS = GLA_COLS // SWA_W


def _inproj_kernel(x_hbm, g_ref, w_ref, wz_ref, wg_ref, bg_ref, proj_ref, loga_ref, qkv_ref, kc_hbm, vc_hbm,
                   x_buf, xn_ref, kc_buf, vc_buf, sem, *, tiles_per_seq, kept_tiles):
    i, j = pl.program_id(0), pl.program_id(1)
    tm = x_buf.shape[0]
    last_i = pl.num_programs(0) - 1

    def x_copy(tile):
        return pltpu.make_async_copy(x_hbm.at[pl.ds(tile * tm, tm)], x_buf, sem.at[0])

    def kept(tile):
        return tile % tiles_per_seq >= tiles_per_seq - kept_tiles

    def cache_copy(buf, out_hbm, tile, s):
        block = (tile // tiles_per_seq) * kept_tiles + tile % tiles_per_seq - (tiles_per_seq - kept_tiles)
        rows = tm * SWA_HEADS
        return pltpu.make_async_copy(buf, out_hbm.at[pl.ds(block * rows, rows)], sem.at[s])

    @pl.when((j == 1) & (i > 0) & kept(i - 1))
    def _():
        cache_copy(vc_buf, vc_hbm, i - 1, 2).wait()

    @pl.when(j == 0)
    def _():
        @pl.when(i == 0)
        def _():
            x_copy(0).start(priority=1)

        x_copy(i).wait()
        xn_ref[...] = _rms_rows(x_buf[...], g_ref[...]).astype(BF16)

        @pl.when(i < last_i)
        def _():
            x_copy(i + 1).start(priority=1)

        za = _dot_nt(xn_ref[...], wz_ref[...])
        gate = _dot(za.astype(BF16), wg_ref[...]) + bg_ref[...]
        loga_ref[...] = _log_sigmoid(gate) / GLA_TAU

    @pl.when(j < GLA_STEPS)
    def _():
        proj_ref[...] = _dot_nt(xn_ref[...], w_ref[...]).astype(BF16)

    def swa_step(cache):
        res = _dot_nt(xn_ref[...], w_ref[...])
        for h in range(SWA_HEADS):
            qkv_ref[0, h] = res[:, h * SWA_HD:(h + 1) * SWA_HD]
        if cache is not None:
            buf, out_hbm, s = cache

            @pl.when(kept(i))
            def _():
                for h in range(SWA_HEADS):
                    buf[pl.ds(h, tm, stride=SWA_HEADS), :] = res[:, h * SWA_HD:(h + 1) * SWA_HD]
                cache_copy(buf, out_hbm, i, s).start(priority=1)

    for step, cache in enumerate((None, (kc_buf, kc_hbm, 1), (vc_buf, vc_hbm, 2))):
        pl.when(j == GLA_STEPS + step)(functools.partial(swa_step, cache))

    @pl.when((j == GLA_STEPS + 2) & kept(i))
    def _():
        cache_copy(kc_buf, kc_hbm, i, 1).wait()

        @pl.when(i == last_i)
        def _():
            cache_copy(vc_buf, vc_hbm, i, 2).wait()


def _inproj(x, g, w_main, w_z, w_gate, b_gate, *, tm, seq, keep):
    m, d = x.shape
    tn = SWA_W
    tps, kt = seq // tm, keep // tm
    assert seq % tm == 0 and keep % tm == 0 and m % seq == 0

    cache_rows = (m // seq) * keep * SWA_HEADS
    hbm = pl.BlockSpec(memory_space=pl.ANY)
    return pl.pallas_call(
        functools.partial(_inproj_kernel, tiles_per_seq=tps, kept_tiles=kt),
        grid=(m // tm, PROJ_W // tn),
        in_specs=[
            pl.BlockSpec(memory_space=pl.ANY),
            pl.BlockSpec((1, d), lambda i, j: (0, 0)),
            pl.BlockSpec((tn, d), lambda i, j: (j, 0)),
            pl.BlockSpec((LANE, d), lambda i, j: (0, 0)),
            pl.BlockSpec((LANE, GLA_QK_W), lambda i, j: (0, 0)),
            pl.BlockSpec((1, GLA_QK_W), lambda i, j: (0, 0)),
        ],
        out_specs=[
            pl.BlockSpec((tm, tn), lambda i, j: (i, jnp.minimum(j, GLA_STEPS - 1))),
            pl.BlockSpec((tm, GLA_QK_W), lambda i, j: (i, 0)),
            pl.BlockSpec((1, SWA_HEADS, tm, SWA_HD), lambda i, j: (jnp.maximum(j - GLA_STEPS, 0), 0, i, 0)),
            hbm, hbm,
        ],
        out_shape=[jax.ShapeDtypeStruct((m, GLA_COLS), BF16),
                   jax.ShapeDtypeStruct((m, GLA_QK_W), F32),
                   jax.ShapeDtypeStruct((3, SWA_HEADS, m, SWA_HD), F32),
                   jax.ShapeDtypeStruct((cache_rows, SWA_HD), F32),
                   jax.ShapeDtypeStruct((cache_rows, SWA_HD), F32)],
        scratch_shapes=[pltpu.VMEM((tm, d), F32), pltpu.VMEM((tm, d), BF16),
                        pltpu.VMEM((tm * SWA_HEADS, SWA_HD), F32), pltpu.VMEM((tm * SWA_HEADS, SWA_HD), F32),
                        pltpu.SemaphoreType.DMA((3,))],
        compiler_params=_params(("arbitrary", "arbitrary")),
        name="norm_inproj",
    )(x, g, w_main, w_z, w_gate, b_gate)


def _split3(x):
    hi = x.astype(BF16)
    r1 = x - hi.astype(F32)
    mid = r1.astype(BF16)
    lo = (r1 - mid.astype(F32)).astype(BF16)
    return hi, mid, lo


def _gla_kernel(*refs, chunk, nsub, has_s0):
    scratch = refs[-5:]
    s_scr, qd_scr, oi_scr, u_scr, dec_scr = scratch
    if has_s0:
        q_ref, k_ref, v_ref, ga_ref, la_ref, gn_ref, s0_ref, o_ref, sfin_ref = refs[:-5]
    else:
        q_ref, k_ref, v_ref, ga_ref, la_ref, gn_ref, o_ref, sfin_ref = refs[:-5]
    c = pl.program_id(1)

    @pl.when(c == 0)
    def _():
        if has_s0:
            s_scr[...] = s0_ref[0]
        else:
            s_scr[...] = jnp.zeros_like(s_scr)

    row = lax.broadcasted_iota(jnp.int32, (chunk, chunk), 0)
    col = lax.broadcasted_iota(jnp.int32, (chunk, chunk), 1)
    causal = row >= col
    tril = jnp.where(causal, 1.0, 0.0).astype(BF16)
    eye = (lax.broadcasted_iota(jnp.int32, (GLA_DK, GLA_DK), 0)
           == lax.broadcasted_iota(jnp.int32, (GLA_DK, GLA_DK), 1))

    group = nsub
    heads = range(GLA_HEADS)
    ks = [slice(h * GLA_DK, (h + 1) * GLA_DK) for h in heads]
    vs = [slice(h * GLA_DV, (h + 1) * GLA_DV) for h in heads]

    def local(c2, carry):
        subs = [c2 * group + a for a in range(group)]
        sls = [pl.ds(pl.multiple_of(i * chunk, chunk), chunk) for i in subs]
        pieces = [_split3(la_ref[0, sl, :]) for sl in sls]
        cums = [_dot(tril, hi) + _dot(tril, mid) + _dot(tril, lo) for hi, mid, lo in pieces]
        q_dec, k_dec, k_end, grow = [], [], [], []
        for sl, cum in zip(sls, cums):
            total = cum[chunk - 1:chunk, :]
            k = k_ref[0, sl, :].astype(F32)
            q_dec.append((q_ref[0, sl, :].astype(F32) * (GLA_DK ** -0.5) * jnp.exp(cum)).astype(BF16))
            k_dec.append((k * jnp.exp(-cum)).astype(BF16))
            k_end.append((k * jnp.exp(total - cum)).astype(BF16))
            grow.append(jnp.exp(total))
            qd_scr[sl, :] = q_dec[-1]
        units = [(a, h) for a in range(group) for h in heads]
        vals = {(a, h): v_ref[0, sls[a], vs[h]] for a, h in units}
        raw = {(a, h): _dot_nt(q_dec[a][:, ks[h]], k_dec[a][:, ks[h]]) for a, h in units}
        for a, h in units:
            u_scr[subs[a], h] = _dot_tn(k_end[a][:, ks[h]], vals[a, h])
        attn = {(a, h): jnp.where(causal, raw[a, h], 0.0).astype(BF16) for a, h in units}
        for a, h in units:
            oi_scr[sls[a], vs[h]] = _dot(attn[a, h], vals[a, h])
        for a, h in units:
            decay = jnp.sum(jnp.where(eye, grow[a][:, ks[h]], 0.0), axis=1, keepdims=True)
            dec_scr[subs[a], h] = jnp.broadcast_to(decay, (GLA_DK, GLA_DK))
        return carry

    local(0, 0)

    for i in range(nsub):
        sl = slice(i * chunk, (i + 1) * chunk)
        states = [s_scr[h] for h in heads]
        inter = [_dot(qd_scr[sl, ks[h]], states[h].astype(BF16)) for h in heads]
        for h in heads:
            decay = dec_scr[i, h]
            s_scr[h] = states[h] * jnp.concatenate([decay] * (GLA_DV // GLA_DK), axis=1) + u_scr[i, h]
        for h in heads:
            on = _rms_rows(oi_scr[sl, vs[h]] + inter[h], gn_ref[:, vs[h]])
            o_ref[0, sl, vs[h]] = (on * _silu(ga_ref[0, sl, vs[h]].astype(F32))).astype(o_ref.dtype)

    @pl.when(c == pl.num_programs(1) - 1)
    def _():
        sfin_ref[0] = s_scr[...]


def _gla(proj3, loga3, g_norm, s0, *, chunk, tc):
    b, t, _ = proj3.shape
    nsub = tc // chunk
    state = pl.BlockSpec((1, GLA_HEADS, GLA_DK, GLA_DV), lambda bi, c: (bi, 0, 0, 0))
    in_specs = [
        pl.BlockSpec((1, tc, GLA_QK_W), lambda bi, c: (bi, c, 0)),
        pl.BlockSpec((1, tc, GLA_QK_W), lambda bi, c: (bi, c, 1)),
        pl.BlockSpec((1, tc, GLA_V_W), lambda bi, c: (bi, c, 1)),
        pl.BlockSpec((1, tc, GLA_V_W), lambda bi, c: (bi, c, 2)),
        pl.BlockSpec((1, tc, GLA_QK_W), lambda bi, c: (bi, c, 0)),
        pl.BlockSpec((1, GLA_V_W), lambda bi, c: (0, 0)),
    ]
    args = [proj3, proj3, proj3, proj3, loga3, g_norm]
    if s0 is not None:
        in_specs.append(state)
        args.append(s0)
    return pl.pallas_call(
        functools.partial(_gla_kernel, chunk=chunk, nsub=nsub, has_s0=s0 is not None),
        grid=(b, t // tc),
        in_specs=in_specs,
        out_specs=[pl.BlockSpec((1, tc, GLA_V_W), lambda bi, c: (bi, c, 0)), state],
        out_shape=[jax.ShapeDtypeStruct((b, t, GLA_V_W), BF16),
                   jax.ShapeDtypeStruct((b, GLA_HEADS, GLA_DK, GLA_DV), F32)],
        scratch_shapes=[pltpu.VMEM((GLA_HEADS, GLA_DK, GLA_DV), F32),
                        pltpu.VMEM((tc, GLA_QK_W), BF16),
                        pltpu.VMEM((tc, GLA_V_W), F32),
                        pltpu.VMEM((nsub, GLA_HEADS, GLA_DK, GLA_DV), F32),
                        pltpu.VMEM((nsub, GLA_HEADS, GLA_DK, GLA_DK), F32)],
        compiler_params=_params(("parallel", "arbitrary")),
        name="gla",
    )(*args)


SWA_SUPER = 2048
SWA_TILES = 16
SWA_GROUP = 8
QUAD = 4


def _swa_prompt_kernel(q_ref, k_ref, v_ref, o_ref, x4, q1, q4, q16, k1, k4, k16, v1, v4, v16, ob, lb, bias):
    n = SWA_BLOCK
    s_tok = q_ref.shape[0]
    l4, l16 = s_tok // QUAD, s_tok // (QUAD * QUAD)
    head, has_prev = pl.program_id(1), pl.program_id(2) > 0
    scale = SWA_HD ** -0.5

    slope = ALIBI_SLOPES[0]
    for h in range(1, SWA_HEADS):
        slope = jnp.where(head == h, ALIBI_SLOPES[h], slope)
    qi = lax.broadcasted_iota(jnp.int32, (n, 2 * n), 0)
    kj = lax.broadcasted_iota(jnp.int32, (n, 2 * n), 1)
    steps = qi + n - kj
    visible = (steps >= 0) & (steps <= n)
    first_visible = visible & (has_prev | (kj >= n))
    for g, (_, dil) in enumerate(SWA_PATTERNS):
        alibi = (-float(dil) * slope) * steps.astype(F32)
        bias[2 * g] = jnp.where(visible, alibi, NEG)
        bias[2 * g + 1] = jnp.where(first_visible, alibi, NEG)

    for buf, length in ((k1, s_tok), (v1, s_tok), (k4, l4), (v4, l4), (k16, l16), (v16, l16)):
        lead = buf.shape[:-2]
        idx = (slice(None),) * len(lead)

        @pl.when(has_prev)
        def _():
            buf[idx + (slice(0, n),)] = buf[idx + (slice(length, length + n),)]

        @pl.when(jnp.logical_not(has_prev))
        def _():
            buf[idx + (slice(0, n),)] = jnp.zeros(lead + (n, SWA_HD), BF16)

    for t, src in enumerate((q_ref, k_ref, v_ref)):
        for r in range(QUAD):
            x4[t, r] = src[pl.ds(r, l4, stride=QUAD), :]
    q1[...] = q_ref[...].astype(BF16)
    k1[n:, :] = k_ref[...].astype(BF16)
    v1[n:, :] = v_ref[...].astype(BF16)
    for r in range(QUAD):
        q4[r] = x4[0, r].astype(BF16)
        k4[r, n:, :] = x4[1, r].astype(BF16)
        v4[r, n:, :] = x4[2, r].astype(BF16)
    for r in range(QUAD * QUAD):
        sub = pl.ds(r // QUAD, l16, stride=QUAD)
        q16[r] = x4[0, r % QUAD, sub, :].astype(BF16)
        k16[r, n:, :] = x4[1, r % QUAD, sub, :].astype(BF16)
        v16[r, n:, :] = x4[2, r % QUAD, sub, :].astype(BF16)

    def run_tiles(tiles):
        for g0 in range(0, len(tiles), SWA_GROUP):
            group = tiles[g0:g0 + SWA_GROUP]
            scores = [_dot_nt(q(), keys()) * scale + table() for q, keys, _, table, _, _ in group]
            probs, lses = [], []
            for s in scores:
                m = jnp.max(s, axis=1, keepdims=True)
                p = jnp.exp(s - m)
                l = jnp.sum(p, axis=1, keepdims=True)
                probs.append(p.astype(BF16))
                lses.append((1.0 / l, m + jnp.log(l)))
            accs = [_dot(p, tile[2]()) for p, tile in zip(probs, group)]
            for acc, (inv, lse), (_, _, _, _, branch, rows) in zip(accs, lses, group):
                ob[branch, rows, :] = acc * inv
                lb[branch, rows, :] = jnp.broadcast_to(lse, (n, SWA_HD))

    def window(ref, lead, start, size):
        return lambda: ref[lead + (pl.ds(start, size), slice(None))]

    def d1_body(c, carry):
        tiles = []
        for a in range(SWA_TILES):
            i = c * SWA_TILES + a
            row0 = pl.multiple_of(i * n, n)
            tiles.append((window(q1, (), row0, n), window(k1, (), row0, 2 * n), window(v1, (), row0, 2 * n),
                          lambda i=i: bias[jnp.where(i == 0, 1, 0)], 0, pl.ds(row0, n)))
        run_tiles(tiles)
        return carry

    lax.fori_loop(0, s_tok // n // SWA_TILES, d1_body, 0)

    per4 = l4 // n

    def d4_body(c, carry):
        tiles = []
        for a in range(SWA_TILES // per4):
            r = c * (SWA_TILES // per4) + a
            for i in range(per4):
                tiles.append((window(q4, (r,), i * n, n), window(k4, (r,), i * n, 2 * n),
                              window(v4, (r,), i * n, 2 * n), lambda i=i: bias[2 + (1 if i == 0 else 0)], 1,
                              pl.ds(pl.multiple_of(r * l4 + i * n, n), n)))
        run_tiles(tiles)
        return carry

    lax.fori_loop(0, QUAD * per4 // SWA_TILES, d4_body, 0)

    per16 = l16 // n

    def d16_body(c, carry):
        tiles = []
        for a in range(SWA_TILES // per16):
            r4 = a % QUAD
            j = c * (SWA_TILES // per16 // QUAD) + a // QUAD
            r = r4 + QUAD * j
            for i in range(per16):
                tiles.append((window(q16, (r,), i * n, n), window(k16, (r,), i * n, 2 * n),
                              window(v16, (r,), i * n, 2 * n), lambda i=i: bias[4 + (1 if i == 0 else 0)], 2,
                              pl.ds(r4 * l4 + QUAD * i * n + j, n, stride=QUAD)))
        run_tiles(tiles)
        return carry

    lax.fori_loop(0, QUAD * QUAD * per16 // SWA_TILES, d16_body, 0)

    def combine_body(c, carry):
        for r in range(QUAD):
            tok = pl.ds(r + QUAD * c * n, n, stride=QUAD)
            cls = pl.ds(pl.multiple_of(r * l4 + c * n, n), n)
            lses = [lb[0, tok, :], lb[1, cls, :], lb[2, cls, :]]
            outs = [ob[0, tok, :], ob[1, cls, :], ob[2, cls, :]]
            m = jnp.maximum(jnp.maximum(lses[0], lses[1]), lses[2])
            es = [jnp.exp(x - m) for x in lses]
            num = es[0] * outs[0] + es[1] * outs[1] + es[2] * outs[2]
            ob[0, tok, :] = num * (1.0 / (es[0] + es[1] + es[2]))
        return carry

    lax.fori_loop(0, l4 // n, combine_body, 0)
    o_ref[...] = ob[0].astype(o_ref.dtype)


def _swa_prompt(qkv, b, t):
    n, s_tok = SWA_BLOCK, SWA_SUPER
    assert t % s_tok == 0 and [d for _, d in SWA_PATTERNS] == [1, QUAD, QUAD * QUAD]
    assert all(w // d == n for w, d in SWA_PATTERNS)
    l4, l16 = s_tok // QUAD, s_tok // (QUAD * QUAD)
    assert (s_tok // n) % SWA_TILES == 0 and SWA_TILES % (l4 // n) == 0 and SWA_TILES % (QUAD * l16 // n) == 0
    view = qkv.reshape(3, SWA_HEADS, b, t, SWA_HD)

    def part(idx):
        return pl.BlockSpec((None, None, None, s_tok, SWA_HD), lambda bi, h, sb: (idx, h, bi, sb, 0))

    def rows(*lead):
        return pltpu.VMEM(lead + (SWA_HD,), BF16)

    o = pl.pallas_call(
        _swa_prompt_kernel,
        grid=(b, SWA_HEADS, t // s_tok),
        in_specs=[part(0), part(1), part(2)],
        out_specs=pl.BlockSpec((None, s_tok, SWA_HD), lambda bi, h, sb: (bi, sb, h)),
        out_shape=jax.ShapeDtypeStruct((b, t, SWA_W), BF16),
        scratch_shapes=[
            pltpu.VMEM((3, QUAD, l4, SWA_HD), F32),
            rows(s_tok), rows(QUAD, l4), rows(QUAD * QUAD, l16),
            rows(n + s_tok), rows(QUAD, n + l4), rows(QUAD * QUAD, n + l16),
            rows(n + s_tok), rows(QUAD, n + l4), rows(QUAD * QUAD, n + l16),
            pltpu.VMEM((3, s_tok, SWA_HD), F32), pltpu.VMEM((3, s_tok, SWA_HD), F32),
            pltpu.VMEM((2 * len(SWA_PATTERNS), n, 2 * n), F32),
        ],
        compiler_params=_params(("parallel", "parallel", "arbitrary")),
        name="swa_prompt",
    )(view, view, view)
    return o.reshape(b * t, SWA_W)


def _branch_multiplicity(d):
    mult = jnp.zeros(d.shape, F32)
    for window, dil in SWA_PATTERNS:
        hit = (d >= 0) & (d <= window) & ((d & (dil - 1)) == 0)
        mult = mult + jnp.where(hit, 1.0, 0.0)
    return mult


SAMPLE_NEAR = SWA_PATTERNS[1][0]
SAMPLE_GROUP = SWA_PATTERNS[2][1]


def _swa_sample_kernel(q_ref, kn_ref, vn_ref, kfar_ref, vfar_ref, knear_ref, vnear_ref, o_ref, *, w, tn):
    tp = q_ref.shape[1]
    groups, per = kfar_ref.shape[1], kfar_ref.shape[2]
    n_far, n_near = groups * tn, knear_ref.shape[1] // SWA_HEADS
    kfar, vfar = kfar_ref.reshape(groups * per, SWA_HD), vfar_ref.reshape(groups * per, SWA_HD)

    def query_row(shape):
        return lax.broadcasted_iota(jnp.int32, shape, 0)

    e_far = lax.broadcasted_iota(jnp.int32, (tp, n_far), 1)
    pos_far = (e_far // tn) * SAMPLE_GROUP + e_far % tn
    pos_near = (w - n_near) + lax.broadcasted_iota(jnp.int32, (tp, n_near), 1)
    j_new = lax.broadcasted_iota(jnp.int32, (tp, tp), 1)
    dists = [w + query_row((tp, n_far)) - pos_far,
             w + query_row((tp, n_near)) - pos_near,
             jnp.where(j_new < tn, query_row((tp, tp)) - j_new, -1)]
    mults = [_branch_multiplicity(d) for d in dists]
    dists = [d.astype(F32) for d in dists]
    scale = SWA_HD ** -0.5

    for h in range(SWA_HEADS):
        hs = slice(h * SWA_HD, (h + 1) * SWA_HD)
        far_rows = pl.ds(h, n_far, stride=SWA_HEADS)
        near_rows = pl.ds(h, n_near, stride=SWA_HEADS)
        slope = ALIBI_SLOPES[h]
        q = q_ref[0, :, hs]
        keys = [kfar[far_rows, :].astype(BF16), knear_ref[0, near_rows, :].astype(BF16), kn_ref[0, :, hs]]
        values = [vfar[far_rows, :].astype(BF16), vnear_ref[0, near_rows, :].astype(BF16), vn_ref[0, :, hs]]
        scores = [jnp.where(mult > 0.0, _dot_nt(q, k) * scale - slope * d, NEG)
                  for k, d, mult in zip(keys, dists, mults)]
        m = functools.reduce(jnp.maximum, [jnp.max(s, axis=1, keepdims=True) for s in scores])
        probs = [mult * jnp.exp(s - m) for s, mult in zip(scores, mults)]
        l = functools.reduce(lambda a, b: a + b, [jnp.sum(p, axis=1, keepdims=True) for p in probs])
        acc = functools.reduce(lambda a, b: a + b, [_dot(p.astype(BF16), v) for p, v in zip(probs, values)])
        o_ref[0, :, hs] = (acc / l).astype(o_ref.dtype)


def _swa_sample(projp, cache_k, cache_v, *, tn):
    b, tp, _ = projp.shape
    rows = cache_k.shape[1]
    w = rows // SWA_HEADS
    group_rows = SAMPLE_GROUP * SWA_HEADS
    near_rows = SAMPLE_NEAR * SWA_HEADS
    assert w % SAMPLE_NEAR == 0 and SAMPLE_NEAR % SAMPLE_GROUP == 0 and tn <= SAMPLE_GROUP
    assert [wd for wd, _ in SWA_PATTERNS][:2] == [SWA_BLOCK, SAMPLE_NEAR] and w >= SWA_PATTERNS[2][0]
    far_groups = (w - SAMPLE_NEAR) // SAMPLE_GROUP
    grouped = (b, rows // group_rows, group_rows, SWA_HD)

    def new(cb):
        return pl.BlockSpec((1, tp, SWA_W), lambda bi: (bi, 0, cb))

    far = pl.BlockSpec((1, far_groups, tn * SWA_HEADS, SWA_HD), lambda bi: (bi, 0, 0, 0))
    near = pl.BlockSpec((1, near_rows, SWA_HD), lambda bi: (bi, rows // near_rows - 1, 0))
    return pl.pallas_call(
        functools.partial(_swa_sample_kernel, w=w, tn=tn),
        grid=(b,),
        in_specs=[new(0), new(1), new(2), far, far, near, near],
        out_specs=pl.BlockSpec((1, tp, SWA_W), lambda bi: (bi, 0, 0)),
        out_shape=jax.ShapeDtypeStruct((b, tp, SWA_W), BF16),
        compiler_params=_params(("parallel",)),
        name="swa_sample",
    )(projp, projp, projp, cache_k.reshape(grouped), cache_v.reshape(grouped), cache_k, cache_v)


def _outproj_kernel(x_ref, oa_ref, ob_ref, wa_ref, wb_ref, h_ref):
    h_ref[...] = x_ref[...] + _dot(oa_ref[...], wa_ref[...]) + _dot(ob_ref[...], wb_ref[...])


def _outproj(x, o_a, o_b, w_out, *, tm):
    m, d = x.shape
    return pl.pallas_call(
        _outproj_kernel,
        grid=(m // tm,),
        in_specs=[pl.BlockSpec((tm, d), lambda i: (i, 0)),
                  pl.BlockSpec((tm, GLA_V_W), lambda i: (i, 0)),
                  pl.BlockSpec((tm, SWA_W), lambda i: (i, 0)),
                  pl.BlockSpec((GLA_V_W, d), lambda i: (0, 0)),
                  pl.BlockSpec((SWA_W, d), lambda i: (GLA_V_W // SWA_W, 0))],
        out_specs=pl.BlockSpec((tm, d), lambda i: (i, 0)),
        out_shape=jax.ShapeDtypeStruct((m, d), F32),
        compiler_params=_params(("parallel",)),
        name="outproj",
    )(x, o_a, o_b, w_out, w_out)


FFN_HALO = 16
FFN_TF = 512
CACHE_PIECES = 4


def _ffn_prompt_kernel(h_ref, halo_ref, g_ref, wa_ref, wv_ref, cp_ref, wd_ref, gf_ref, kc_hbm, vc_hbm, kn_ref,
                       vn_ref, y_ref, u_ref, ko_hbm, vo_hbm, xn_scr, ea_scr, ev_scr, ks_scr, vs_scr, sems,
                       *, tiles_per_seq):
    i, j = pl.program_id(0), pl.program_id(1)
    tm = h_ref.shape[0]
    last_j = pl.num_programs(1) - 1

    nseq, rows = kc_hbm.shape[0], kc_hbm.shape[1]
    new = kn_ref.shape[1]
    old = rows - new
    piece = old // CACHE_PIECES
    moving = i < nseq

    def load(src_hbm, dst_scr, c, sem):
        part = pl.ds(c * piece, piece)
        return pltpu.make_async_copy(src_hbm.at[i, pl.ds(new + c * piece, piece)], dst_scr.at[part], sem)

    def store(src_scr, dst_hbm, c, sem):
        part = pl.ds(c * piece, piece)
        return pltpu.make_async_copy(src_scr.at[part], dst_hbm.at[i, part], sem)

    loads = [[load(kc_hbm, ks_scr, c, sems.at[0]), load(vc_hbm, vs_scr, c, sems.at[1])]
             for c in range(CACHE_PIECES)]
    stores = [[store(ks_scr, ko_hbm, c, sems.at[2]), store(vs_scr, vo_hbm, c, sems.at[3])]
              for c in range(CACHE_PIECES)]
    stores[0] += [pltpu.make_async_copy(kn_ref.at[0], ko_hbm.at[i, pl.ds(old, new)], sems.at[4]),
                  pltpu.make_async_copy(vn_ref.at[0], vo_hbm.at[i, pl.ds(old, new)], sems.at[5])]

    for c in range(CACHE_PIECES):
        @pl.when(moving & (j == c))
        def _():
            for cp in loads[c]:
                cp.start(priority=1)

        @pl.when(moving & (j == CACHE_PIECES + c))
        def _():
            if c == 0:
                for group in loads:
                    for cp in group:
                        cp.wait()
            for cp in stores[c]:
                cp.start(priority=1)

    @pl.when(j == 0)
    def _():
        first = (i % tiles_per_seq) == 0
        halo = _rms_rows(halo_ref[...], g_ref[...])
        xn_scr[0:FFN_HALO, :] = jnp.where(first, 0.0, halo).astype(BF16)
        xn_scr[FFN_HALO:, :] = _rms_rows(h_ref[...], g_ref[...]).astype(BF16)
        y_ref[...] = h_ref[...]

    xn = xn_scr[...]
    ea_scr[...] = _dot(xn, wa_ref[...])
    ev_scr[...] = _dot(xn, wv_ref[...])

    def conv(e_scr, part):
        out = cp_ref[part, CONV_W:CONV_W + 1, :]
        for tap in range(CONV_W):
            off = FFN_HALO - (CONV_W - 1) + tap
            out = out + cp_ref[part, tap:tap + 1, :] * e_scr[off:off + tm, :]
        return out

    gate = (_silu(conv(ea_scr, 0)) * conv(ev_scr, 1)).astype(BF16)
    y_ref[...] += _dot(gate, wd_ref[...])
    u_ref[0, j, 0] = ea_scr[FFN_HALO + tm - 8:FFN_HALO + tm, :]
    u_ref[0, j, 1] = ev_scr[FFN_HALO + tm - 8:FFN_HALO + tm, :]

    @pl.when(j == last_j)
    def _():
        y_ref[...] = _rms_rows(y_ref[...], gf_ref[...])

    @pl.when(moving & (j == last_j))
    def _():
        for group in stores:
            for cp in group:
                cp.wait()


def _ffn_prompt(h, g_ffn, w_up, conv_params, w_down, g_final, cache_k, cache_v, k_new, v_new, *, seq, tm):
    m, d = h.shape
    f = w_down.shape[0]
    tf = conv_params.shape[-1]
    nf = f // tf
    ntile = m // tm
    hb = tm // FFN_HALO
    nseq, rows, _ = cache_k.shape
    new = k_new.shape[1]
    assert nseq <= ntile and nf > 2 * CACHE_PIECES, "one cache per row tile, moved within its column steps"
    assert (rows - new) % (8 * CACHE_PIECES) == 0
    hbm = pl.BlockSpec(memory_space=pl.ANY)
    fresh = pl.BlockSpec((1, new, SWA_HD), lambda i, j: (jnp.minimum(i, nseq - 1), 0, 0))
    y, u_last, k_all, v_all = pl.pallas_call(
        functools.partial(_ffn_prompt_kernel, tiles_per_seq=seq // tm),
        grid=(ntile, nf),
        in_specs=[
            pl.BlockSpec((tm, d), lambda i, j: (i, 0)),
            pl.BlockSpec((FFN_HALO, d), lambda i, j: (jnp.maximum(i * hb - 1, 0), 0)),
            pl.BlockSpec((1, d), lambda i, j: (0, 0)),
            pl.BlockSpec((d, tf), lambda i, j: (0, j)),
            pl.BlockSpec((d, tf), lambda i, j: (0, nf + j)),
            pl.BlockSpec((None, 2, 8, tf), lambda i, j: (j, 0, 0, 0)),
            pl.BlockSpec((tf, d), lambda i, j: (j, 0)),
            pl.BlockSpec((1, d), lambda i, j: (0, 0)),
            hbm, hbm, fresh, fresh,
        ],
        out_specs=[
            pl.BlockSpec((tm, d), lambda i, j: (i, 0)),
            pl.BlockSpec((1, nf, 2, 8, tf), lambda i, j: (i, 0, 0, 0, 0)),
            hbm, hbm,
        ],
        out_shape=[jax.ShapeDtypeStruct((m, d), F32),
                   jax.ShapeDtypeStruct((ntile, nf, 2, 8, tf), F32),
                   jax.ShapeDtypeStruct(cache_k.shape, cache_k.dtype),
                   jax.ShapeDtypeStruct(cache_v.shape, cache_v.dtype)],
        scratch_shapes=[pltpu.VMEM((tm + FFN_HALO, d), BF16),
                        pltpu.VMEM((tm + FFN_HALO, tf), F32),
                        pltpu.VMEM((tm + FFN_HALO, tf), F32),
                        pltpu.VMEM((rows - new, SWA_HD), F32),
                        pltpu.VMEM((rows - new, SWA_HD), F32),
                        pltpu.SemaphoreType.DMA((6,))],
        compiler_params=_params(("arbitrary", "arbitrary")),
        name="convffn_prompt",
    )(h, h, g_ffn, w_up, w_up, conv_params, w_down, g_final, cache_k, cache_v, k_new, v_new)
    u_last = u_last.transpose(0, 3, 2, 1, 4).reshape(ntile, 8, 2 * f)
    return y, u_last, k_all, v_all


def _ffn_sample_kernel(h_ref, g_ref, sa_ref, sv_ref, wa_ref, wv_ref, ca_ref, cv_ref, wd_ref,
                       gf_ref, y_ref, na_ref, nv_ref, xn_scr, acc_scr, *, nb):
    j = pl.program_id(0)
    m = h_ref.shape[0]
    keep = (CONV_W - 1) * nb

    @pl.when(j == 0)
    def _():
        xn_scr[...] = _rms_rows(h_ref[...], g_ref[...]).astype(BF16)
        acc_scr[...] = jnp.zeros_like(acc_scr)

    xn = xn_scr[...]

    def conv(s_ref, w_ref, c_ref, n_ref):
        u = _dot(xn, w_ref[...])
        ext = jnp.concatenate([s_ref[...], u], axis=0)
        n_ref[...] = ext[m:m + keep, :]
        out = c_ref[CONV_W:CONV_W + 1, :]
        for tap in range(CONV_W):
            out = out + c_ref[tap:tap + 1, :] * ext[tap * nb:tap * nb + m, :]
        return out

    gate = (_silu(conv(sa_ref, wa_ref, ca_ref, na_ref)) * conv(sv_ref, wv_ref, cv_ref, nv_ref)).astype(BF16)
    acc_scr[...] += _dot(gate, wd_ref[...])

    @pl.when(j == pl.num_programs(0) - 1)
    def _():
        y_ref[...] = _rms_rows(h_ref[...] + acc_scr[...], gf_ref[...])


def _ffn_sample(h_tm, state_tm, g_ffn, w_up, conv_params, w_down, g_final, *, nb):
    m, d = h_tm.shape
    f = w_down.shape[0]
    tf = conv_params.shape[-1]
    nf = f // tf
    keep = (CONV_W - 1) * nb
    return pl.pallas_call(
        functools.partial(_ffn_sample_kernel, nb=nb),
        grid=(nf,),
        in_specs=[
            pl.BlockSpec((m, d), lambda j: (0, 0)),
            pl.BlockSpec((1, d), lambda j: (0, 0)),
            pl.BlockSpec((keep, tf), lambda j: (0, j)),
            pl.BlockSpec((keep, tf), lambda j: (0, nf + j)),
            pl.BlockSpec((d, tf), lambda j: (0, j)),
            pl.BlockSpec((d, tf), lambda j: (0, nf + j)),
            pl.BlockSpec((None, None, 8, tf), lambda j: (j, 0, 0, 0)),
            pl.BlockSpec((None, None, 8, tf), lambda j: (j, 1, 0, 0)),
            pl.BlockSpec((tf, d), lambda j: (j, 0)),
            pl.BlockSpec((1, d), lambda j: (0, 0)),
        ],
        out_specs=[
            pl.BlockSpec((m, d), lambda j: (0, 0)),
            pl.BlockSpec((keep, tf), lambda j: (0, j)),
            pl.BlockSpec((keep, tf), lambda j: (0, j)),
        ],
        out_shape=[jax.ShapeDtypeStruct((m, d), F32),
                   jax.ShapeDtypeStruct((keep, f), F32),
                   jax.ShapeDtypeStruct((keep, f), F32)],
        scratch_shapes=[pltpu.VMEM((m, d), BF16), pltpu.VMEM((m, d), F32)],
        compiler_params=_params(("arbitrary",)),
        name="convffn_sample",
    )(h_tm, g_ffn, state_tm, state_tm, w_up, w_up, conv_params, conv_params, w_down, g_final)


def _cast_rows_kernel(src_ref, dst_ref):
    dst_ref[...] = src_ref[...].astype(dst_ref.dtype)


def _pack_w_in(w_t, *, tr):
    n, d = w_t.shape
    z0 = GLA_COLS
    assert n == PROJ_W + GLA_GATE_RANK and z0 % tr == 0 and PROJ_W % tr == 0
    return pl.pallas_call(
        _cast_rows_kernel,
        grid=(PROJ_W // tr,),
        in_specs=[pl.BlockSpec((pl.Element(tr), pl.Element(d)),
                               lambda i: (pl.multiple_of(i * tr + jnp.where(i * tr >= z0, GLA_GATE_RANK, 0), 8),
                                          0))],
        out_specs=pl.BlockSpec((tr, d), lambda i: (i, 0)),
        out_shape=jax.ShapeDtypeStruct((PROJ_W, d), BF16),
        compiler_params=_params(("parallel",)),
        name="pack_w_in",
    )(w_t)


def _pack_weights(g_attn_norm, w_in, w_gate_up, b_gate, g_gla_norm, w_out, g_ffn_norm, w_up, w_conv, b_conv,
                  w_down):
    z0, z1 = GLA_COLS, GLA_COLS + GLA_GATE_RANK
    w_t = jnp.swapaxes(w_in, 1, 2)[0]
    w_main = _pack_w_in(w_t, tr=512)
    w_z = jnp.pad(w_t[z0:z1], ((0, LANE - GLA_GATE_RANK), (0, 0))).astype(BF16)
    w_gate = jnp.pad(w_gate_up, ((0, LANE - GLA_GATE_RANK), (0, 0))).astype(BF16)
    conv_rows = jnp.concatenate([w_conv, b_conv[None, :], jnp.zeros((8 - CONV_W - 1, b_conv.shape[0]), F32)])
    conv_params = conv_rows.reshape(8, 2, -1, FFN_TF).transpose(2, 1, 0, 3)
    return dict(
        g_attn=g_attn_norm[None, :], w_main=w_main, w_z=w_z, w_gate=w_gate, b_gate=b_gate[None, :],
        g_gla=g_gla_norm[None, :], w_out=w_out.astype(BF16), g_ffn=g_ffn_norm[None, :],
        w_up=w_up.astype(BF16), conv_params=conv_params, w_down=w_down.astype(BF16))


def _prompt_layer(x, p, g_final, sample_cache):
    b, t, d = x.shape
    m = b * t
    f = p["w_down"].shape[0]
    keep = min(SWA_MAX_WINDOW, t)
    x2 = x.reshape(m, d)
    proj, loga, qkv, k_buf, v_buf = _inproj(x2, p["g_attn"], p["w_main"], p["w_z"], p["w_gate"], p["b_gate"],
                                            tm=1024, seq=t, keep=keep)
    o_a, s_fin = _gla(proj.reshape(b, t, GLA_COLS), loga.reshape(b, t, GLA_QK_W), p["g_gla"], None,
                      chunk=GLA_CHUNK, tc=512)
    o_b = _swa_prompt(qkv, b, t)
    h = _outproj(x2, o_a.reshape(m, GLA_V_W), o_b, p["w_out"], tm=512)
    tm = 512
    y, u_last, k_all, v_all = _ffn_prompt(h, p["g_ffn"], p["w_up"], p["conv_params"], p["w_down"], g_final,
                                          *sample_cache, seq=t, tm=tm)
    shape = (b, keep, SWA_HEADS, SWA_HD)
    last = u_last.reshape(b, t // tm, 8, 2 * f)[:, -1, 8 - (CONV_W - 1):, :]
    return y.reshape(b, t, d), s_fin, k_buf.reshape(shape), v_buf.reshape(shape), last, k_all, v_all


def _sample_layer(x, s0, conv_s0, cache_k, cache_v, p, g_final):
    b, tn, d = x.shape
    m = b * tn
    tp = 8
    w = cache_k.shape[1]
    x2 = x.reshape(m, d)
    proj, loga, qkv, k_new, v_new = _inproj(x2, p["g_attn"], p["w_main"], p["w_z"], p["w_gate"], p["b_gate"],
                                            tm=m, seq=m, keep=m)
    pad = ((0, 0), (0, tp - tn), (0, 0))
    projp = jnp.pad(proj.reshape(b, tn, GLA_COLS), pad)
    logap = jnp.pad(loga.reshape(b, tn, GLA_QK_W), pad)
    o_a, s_fin = _gla(projp, logap, p["g_gla"], s0, chunk=tp, tc=tp)
    qkvp = jnp.pad(qkv.transpose(2, 0, 1, 3).reshape(b, tn, 3 * SWA_W).astype(BF16), pad)
    rows = (b, tn * SWA_HEADS, SWA_HD)
    cache = (cache_k.reshape(b, w * SWA_HEADS, SWA_HD), cache_v.reshape(b, w * SWA_HEADS, SWA_HD),
             k_new.reshape(rows), v_new.reshape(rows))
    o_b = _swa_sample(qkvp, cache[0], cache[1], tn=tn)
    h = _outproj(x2, o_a[:, :tn].reshape(m, GLA_V_W), o_b[:, :tn].reshape(m, SWA_W), p["w_out"], tm=m)
    h_tm = h.reshape(b, tn, d).transpose(1, 0, 2).reshape(m, d)
    state_tm = conv_s0.transpose(1, 0, 2).reshape((CONV_W - 1) * b, -1)
    y_tm, na, nv = _ffn_sample(h_tm, state_tm, p["g_ffn"], p["w_up"], p["conv_params"], p["w_down"], g_final, nb=b)
    y = y_tm.reshape(tn, b, d).transpose(1, 0, 2)
    conv_new = jnp.concatenate([na, nv], axis=-1).reshape(CONV_W - 1, b, -1).transpose(1, 0, 2)
    return y, s_fin, conv_new, cache


def kernel(x_prompt, x_sample, state_gla, cache_swa_k, cache_swa_v, state_ffn_conv, g_attn_norm, w_in, w_gate_up,
           b_gate, g_gla_norm, w_out, g_ffn_norm, w_up, w_conv, b_conv, w_down, g_final):
    depth = w_in.shape[0]
    assert depth == 1, "the final RMSNorm is fused into the last layer's ConvFFN kernel"
    assert cache_swa_k.shape[2] == SWA_MAX_WINDOW and x_sample.shape[1] <= 8
    p = _pack_weights(g_attn_norm[0], w_in, w_gate_up[0], b_gate[0], g_gla_norm[0], w_out[0], g_ffn_norm[0],
                      w_up[0], w_conv[0], b_conv[0], w_down[0])
    gf = g_final[None, :]
    ys, s2, c2, sample_cache = _sample_layer(x_sample, state_gla[0], state_ffn_conv[0], cache_swa_k[0],
                                             cache_swa_v[0], p, gf)
    yp, s1, k1, v1, c1, k2, v2 = _prompt_layer(x_prompt, p, gf, sample_cache)
    k2, v2 = k2.reshape(cache_swa_k.shape[1:]), v2.reshape(cache_swa_v.shape[1:])
    return (yp, ys, s1[None], s2[None], k1[None], k2[None], v1[None], v2[None], c1[None], c2[None])
```
